```python
import jax
import jax.numpy as jnp
from jax import lax
import numpy as np

D_MODEL = 1024
BATCH = 8
SEQ = 2048
DEPTH = 2

GRID_W = 64
CTX_LEN = 256
N_ADA = 9
D_FF = 2816
NORM_EPS = 1e-6

RET_HEADS = 8
RET_QK_DIM = 64
RET_V_DIM = 128
RET_CHUNK = 128
RET_ROPE_BASE = 10000.0

ATT_HEADS = 8
ATT_KV_HEADS = 2
ATT_HEAD_DIM = 64
ATT_GROUP = ATT_HEADS // ATT_KV_HEADS
WINDOW = 128
ATT_BLOCK = 128
ROPE_BASE = 10000.0

RWKV_HEADS = 8
RWKV_HEAD_DIM = 64
DECAY_LORA = 64
AAA_LORA = 64
MV_LORA = 32
GATE_LORA = 128
RWKV_GN_EPS = 64e-5
RWKV_DECAY_SCALE = 0.6065306597126334

RET_QK_W = RET_HEADS * RET_QK_DIM
RET_V_W = RET_HEADS * RET_V_DIM
ATT_Q_W = ATT_HEADS * ATT_HEAD_DIM
ATT_KV_W = ATT_KV_HEADS * ATT_HEAD_DIM
RWKV_W = RWKV_HEADS * RWKV_HEAD_DIM
RWKV_SPLITS = (RWKV_W, RWKV_W, RWKV_W, DECAY_LORA, DECAY_LORA, AAA_LORA, AAA_LORA, GATE_LORA)
RWKV_IN = sum(RWKV_SPLITS)
N_BRANCH = 3
IN_SPLITS = (RET_QK_W, RET_QK_W, RET_V_W, RET_V_W, ATT_Q_W, ATT_KV_W, ATT_KV_W, RWKV_IN, N_BRANCH * D_MODEL)
N_IN = sum(IN_SPLITS)

kernel_name = 'hybrid_retention_swa_rwkv7_macaron_dit'


def split_cols(t, sizes):
    idx = np.cumsum(sizes)[:-1].tolist()
    return jnp.split(t, idx, axis=-1)


def split_heads(t, n_heads):
    return t.reshape(t.shape[0], t.shape[1], n_heads, -1)


def rms_norm(x, w, eps=NORM_EPS):
    xf = x.astype(jnp.float32)
    y = xf * lax.rsqrt(jnp.mean(jnp.square(xf), -1, keepdims=True) + eps)
    return (y * w.astype(jnp.float32)).astype(x.dtype)


def head_layer_norm(x, eps):
    xf = x.astype(jnp.float32)
    mu = jnp.mean(xf, -1, keepdims=True)
    var = jnp.mean(jnp.square(xf - mu), -1, keepdims=True)
    return (xf - mu) * lax.rsqrt(var + eps)


def modulate(h, shift, scale):
    return h * (1.0 + scale) + shift


def swiglu(h, w_in, w_out):
    gate, up = jnp.split(h @ w_in, 2, axis=-1)
    return (jax.nn.silu(gate) * up) @ w_out


def ffn_sublayer(x, mod, base, norm_w, w_in, w_out):
    h = modulate(rms_norm(x, norm_w), mod[:, :, base], mod[:, :, base + 1])
    return x + 0.5 * mod[:, :, base + 2] * swiglu(h, w_in, w_out)


def rope_angles_1d(pos, dim, base):
    n_freq = dim // 2
    inv = jnp.power(base, -(jnp.arange(n_freq, dtype=jnp.float32) / n_freq))
    return pos.astype(jnp.float32)[:, None] * inv[None, :]


def axial_rope_angles(rows, dim):
    row = jnp.repeat(jnp.arange(rows), GRID_W)
    col = jnp.arange(rows * GRID_W) % GRID_W
    return jnp.concatenate([rope_angles_1d(row, dim // 2, ROPE_BASE),
                            rope_angles_1d(col, dim // 2, ROPE_BASE)], -1)


def apply_rope(t, ang):
    half = t.shape[-1] // 2
    cos = jnp.cos(ang)[:, None, :]
    sin = jnp.sin(ang)[:, None, :]
    t1, t2 = t[..., :half], t[..., half:]
    return jnp.concatenate([t1 * cos - t2 * sin, t1 * sin + t2 * cos], -1)


def flip_seq(t):
    return jnp.flip(t, axis=2)


def centred_conv3(t, w):
    zero = jnp.zeros_like(t[:, :1])
    prev = jnp.concatenate([zero, t[:, :-1]], 1)
    nxt = jnp.concatenate([t[:, 1:], zero], 1)
    return prev * w[0] + t * w[1] + nxt * w[2]


def retention_qkv(q, k, v, ang):
    f32 = jnp.float32
    q = apply_rope(split_heads(q, RET_HEADS).astype(f32), ang)
    k = apply_rope(split_heads(k, RET_HEADS).astype(f32), ang) * (RET_QK_DIM ** -0.5)
    v = split_heads(v, RET_HEADS).astype(f32)
    return q.transpose(0, 2, 1, 3), k.transpose(0, 2, 1, 3), v.transpose(0, 2, 1, 3)


def retention_direction(q, k, v, log_g, s0):
    b, h, n, dk = q.shape
    dv = v.shape[-1]
    nc = n // RET_CHUNK
    qc = q.reshape(b, h, nc, RET_CHUNK, dk)
    kc = k.reshape(b, h, nc, RET_CHUNK, dk)
    vc = v.reshape(b, h, nc, RET_CHUNK, dv)
    pos = jnp.arange(RET_CHUNK, dtype=jnp.float32)
    lg = log_g[:, None]
    diff = pos[:, None] - pos[None, :]
    decay_in = jnp.where(diff >= 0, jnp.exp(lg[:, :, None] * jnp.maximum(diff, 0.0)), 0.0)
    zeta = jnp.exp(lg * (RET_CHUNK - 1.0 - pos))
    xi = jnp.exp(lg * (pos + 1.0))
    chunk_decay = jnp.exp(log_g * RET_CHUNK)[None, :, None, None]
    u = jnp.einsum('bhcmk,hm,bhcmv->bhckv', kc, zeta, vc)

    def step(state, u_c):
        return chunk_decay * state + u_c, state

    s_last, s_prev = lax.scan(step, s0, jnp.moveaxis(u, 2, 0))
    s_prev = jnp.moveaxis(s_prev, 0, 2)
    scores = jnp.einsum('bhcjk,bhcmk->bhcjm', qc, kc) * decay_in[None, :, None]
    out = (jnp.einsum('bhcjm,bhcmv->bhcjv', scores, vc)
           + jnp.einsum('bhcjk,bhckv->bhcjv', qc, s_prev) * xi[None, :, None, :, None])
    return out.reshape(b, h, n, dv), s_last


def retention_output(o, g):
    y = head_layer_norm(o, NORM_EPS).transpose(0, 2, 1, 3)
    y = y.reshape(y.shape[0], y.shape[1], RET_V_W)
    return (jax.nn.silu(g.astype(jnp.float32)) * y).astype(g.dtype)


def attention_qkv(q, k, v, p, ang):
    f32 = jnp.float32
    q = rms_norm(split_heads(q, ATT_HEADS), p['q_norm']).astype(f32)
    k = rms_norm(split_heads(k, ATT_KV_HEADS), p['k_norm']).astype(f32)
    if ang is not None:
        q = apply_rope(q, ang)
        k = apply_rope(k, ang)
    b, n = q.shape[:2]
    q = q.reshape(b, n, ATT_KV_HEADS, ATT_GROUP, ATT_HEAD_DIM).transpose(0, 2, 3, 1, 4)
    k = k.transpose(0, 2, 1, 3)
    v = split_heads(v, ATT_KV_HEADS).astype(f32).transpose(0, 2, 1, 3)
    return q, k, v


def window_context_attention(q, k, v, kc, vc, sink):
    b, hk, g, n, d = q.shape
    nb = n // ATT_BLOCK
    halo = WINDOW // ATT_BLOCK
    span = (2 * halo + 1) * ATT_BLOCK
    pad = ((0, 0), (0, 0), (halo * ATT_BLOCK, halo * ATT_BLOCK), (0, 0))
    kb = jnp.pad(k, pad).reshape(b, hk, nb + 2 * halo, ATT_BLOCK, d)
    vb = jnp.pad(v, pad).reshape(b, hk, nb + 2 * halo, ATT_BLOCK, d)
    kwin = jnp.concatenate([kb[:, :, o:o + nb] for o in range(2 * halo + 1)], axis=3)
    vwin = jnp.concatenate([vb[:, :, o:o + nb] for o in range(2 * halo + 1)], axis=3)
    qb = q.reshape(b, hk, g, nb, ATT_BLOCK, d)
    scale = d ** -0.5
    s_win = jnp.einsum('bhgnqd,bhnkd->bhgnqk', qb, kwin) * scale
    qi = jnp.arange(ATT_BLOCK)
    kj = jnp.arange(span)
    rel = kj[None, :] - halo * ATT_BLOCK - qi[:, None]
    kpos = (jnp.arange(nb)[:, None] - halo) * ATT_BLOCK + kj[None, :]
    valid = (jnp.abs(rel) <= WINDOW)[None] & ((kpos >= 0) & (kpos < n))[:, None, :]
    s_win = jnp.where(valid, s_win, -jnp.inf)
    s_ctx = jnp.einsum('bhgnqd,bhld->bhgnql', qb, kc) * scale
    s_sink = jnp.broadcast_to(sink[None, :, :, None, None, None], s_win.shape[:-1] + (1,))
    prob = jax.nn.softmax(jnp.concatenate([s_win, s_ctx, s_sink], -1), axis=-1)
    n_ctx = kc.shape[2]
    out = (jnp.einsum('bhgnqk,bhnkd->bhgnqd', prob[..., :span], vwin)
           + jnp.einsum('bhgnql,bhld->bhgnqd', prob[..., span:span + n_ctx], vc))
    return out.reshape(b, hk, g, n, d)


def context_attention(q, k, v, sink):
    s = jnp.einsum('bhgqd,bhkd->bhgqk', q, k) * (q.shape[-1] ** -0.5)
    s_sink = jnp.broadcast_to(sink[None, :, :, None, None], s.shape[:-1] + (1,))
    prob = jax.nn.softmax(jnp.concatenate([s, s_sink], -1), axis=-1)
    return jnp.einsum('bhgqk,bhkd->bhgqd', prob[..., :-1], v)


def merge_att_heads(o, dtype):
    b, hk, g, n, d = o.shape
    return o.transpose(0, 3, 1, 2, 4).reshape(b, n, ATT_Q_W).astype(dtype)


def rwkv_prepare(cols, h, v_first, p):
    f32 = jnp.float32

    def heads(t):
        return split_heads(t, RWKV_HEADS).astype(f32)

    cols = centred_conv3(cols, p['shift'])
    r, k, v, wd_f, wd_b, ad_f, ad_b, gd = split_cols(cols, RWKV_SPLITS)
    if v_first is None:
        v_first = v
    else:
        v = v + (v_first - v) * jax.nn.sigmoid(p['v0'] + (h @ p['v_down']) @ p['v_up'])
    g = jax.nn.sigmoid(gd) @ p['g_up']
    kk = heads(k * p['k_k'])
    kk = kk / jnp.maximum(jnp.sqrt(jnp.sum(kk * kk, -1, keepdims=True)), 1e-12)
    dirs = []
    for i, (wd, ad) in enumerate(((wd_f, ad_f), (wd_b, ad_b))):
        z = (p['w0'][i] + jnp.tanh(wd) @ p['w_up'][i]).astype(f32)
        w = jnp.exp(-RWKV_DECAY_SCALE * jax.nn.sigmoid(z))
        a = jax.nn.sigmoid(p['a0'][i] + ad @ p['a_up'][i])
        k_d = k * (1.0 + (a - 1.0) * p['k_a'])
        dirs.append((heads(w), heads(k_d), heads(a)))
    return heads(r), heads(v), kk, g, dirs, v_first


def rwkv7_scan(r, w, k, a, v, kk, s0, reverse):
    def step(S, inp):
        r_t, w_t, k_t, a_t, v_t, kk_t = inp
        sa = -jnp.einsum('bhvk,bhk->bhv', S, kk_t)
        S = (S * w_t[:, :, None, :] + sa[..., None] * (kk_t * a_t)[:, :, None, :]
             + v_t[..., None] * k_t[:, :, None, :])
        return S, jnp.einsum('bhvk,bhk->bhv', S, r_t)

    xs = tuple(jnp.moveaxis(t, 1, 0) for t in (r, w, k, a, v, kk))
    S, ys = lax.scan(step, s0, xs, reverse=reverse)
    return jnp.moveaxis(ys, 0, 1), S


def rwkv_output(y, r, v, dirs, g, p):
    b, n = y.shape[:2]
    yn = head_layer_norm(y, RWKV_GN_EPS).reshape(b, n, RWKV_W) * p['gn_w'] + p['gn_b']
    (_, k_f, _), (_, k_b, _) = dirs
    bonus = (jnp.sum(r * k_f * p['r_k'], -1, keepdims=True)
             + jnp.sum(r * k_b * p['r_k'], -1, keepdims=True)) * v
    return ((yn + bonus.reshape(b, n, RWKV_W)) * g).astype(g.dtype)


def branch_merge(ret, att, rw, gate_logits, p):
    g_ret, g_att, g_rw = jnp.split(jax.nn.sigmoid(gate_logits), N_BRANCH, axis=-1)
    merged = g_ret * (ret @ p['w_ret']) + g_att * (att @ p['w_att']) + g_rw * (rw @ p['w_rwkv'])
    return merged @ p['w_out']


def token_mixing(hx, hs, p, v_first, att_ang, ret_ang_x, ret_ang_s, want_ctx):
    f32 = jnp.float32
    bsz = hx.shape[0]
    cols_x = split_cols(hx @ p['w_in'], IN_SPLITS)
    cols_s = split_cols(hs @ p['w_in'], IN_SPLITS)

    log_g = jax.nn.log_sigmoid(p['ret_decay_logit'].astype(f32))
    qx, kx, vx = retention_qkv(cols_x[0], cols_x[1], cols_x[2], ret_ang_x)
    qs, ks, vs = retention_qkv(cols_s[0], cols_s[1], cols_s[2], ret_ang_s)
    zero_ret = jnp.zeros((bsz, RET_HEADS, RET_QK_DIM, RET_V_DIM), f32)
    os_f, st_f = retention_direction(qs, ks, vs, log_g[0], zero_ret)
    os_b, st_b = retention_direction(flip_seq(qs), flip_seq(ks), flip_seq(vs), log_g[1], zero_ret)
    ox_f, _ = retention_direction(qx, kx, vx, log_g[0], st_f)
    ox_b, _ = retention_direction(flip_seq(qx), flip_seq(kx), flip_seq(vx), log_g[1], st_b)
    ret_x = retention_output(ox_f + flip_seq(ox_b), cols_x[3])

    sink = p['att_sink'].astype(f32).reshape(ATT_KV_HEADS, ATT_GROUP)
    aqx, akx, avx = attention_qkv(cols_x[4], cols_x[5], cols_x[6], p, att_ang)
    aqs, aks, avs = attention_qkv(cols_s[4], cols_s[5], cols_s[6], p, None)
    att_x = merge_att_heads(window_context_attention(aqx, akx, avx, aks, avs, sink), hx.dtype)

    r_x, v_x, kk_x, g_x, dirs_x, vf_x = rwkv_prepare(cols_x[7], hx, v_first[0], p)
    r_s, v_s, kk_s, g_s, dirs_s, vf_s = rwkv_prepare(cols_s[7], hs, v_first[1], p)
    zero_rw = jnp.zeros((bsz, RWKV_HEADS, RWKV_HEAD_DIM, RWKV_HEAD_DIM), f32)
    ys_f, S_f = rwkv7_scan(r_s, *dirs_s[0], v_s, kk_s, zero_rw, False)
    ys_b, S_b = rwkv7_scan(r_s, *dirs_s[1], v_s, kk_s, zero_rw, True)
    yx_f, _ = rwkv7_scan(r_x, *dirs_x[0], v_x, kk_x, S_f, False)
    yx_b, _ = rwkv7_scan(r_x, *dirs_x[1], v_x, kk_x, S_b, True)
    rwkv_x = rwkv_output(yx_f + yx_b, r_x, v_x, dirs_x, g_x, p)

    out_x = branch_merge(ret_x, att_x, rwkv_x, cols_x[8], p)
    out_s = None
    if want_ctx:
        ret_s = retention_output(os_f + flip_seq(os_b), cols_s[3])
        att_s = merge_att_heads(context_attention(aqs, aks, avs, sink), hs.dtype)
        rwkv_s = rwkv_output(ys_f + ys_b, r_s, v_s, dirs_s, g_s, p)
        out_s = branch_merge(ret_s, att_s, rwkv_s, cols_s[8], p)
    return out_x, out_s, (vf_x, vf_s)


def setup_inputs(seed: int = 0) -> dict:
    key = jax.random.key(seed)
    keys = iter(jax.random.split(key, 48))
    f32 = jnp.float32

    def nrm(shape, scale):
        return jax.random.normal(next(keys), shape, f32) * scale

    D = D_MODEL
    h_idx = jnp.arange(RET_HEADS, dtype=f32)
    ret_logit = jnp.log(jnp.exp2(5.0 + h_idx) - 1.0)
    ratio = jnp.linspace(0.0, 1.0, RWKV_W, dtype=f32)
    w0_base = -6.5 + 5.0 * ratio ** 0.85
    shift_base = jnp.array([0.25, 0.5, 0.25], f32)[None, :, None]
    return {
        'x': nrm((BATCH, SEQ, D), 1.0),
        'c': nrm((BATCH, D), 1.0),
        'ctx': nrm((BATCH, CTX_LEN, D), 1.0),
        'c_ctx': nrm((D,), 1.0),
        'ada_w': nrm((DEPTH, D, N_ADA * D), 0.5 * D ** -0.5),
        'ada_b': nrm((DEPTH, N_ADA * D), 0.02),
        'norm_w': 1.0 + nrm((DEPTH, 3, D), 0.05),
        'ffn1_w_in': nrm((DEPTH, D, 2 * D_FF), D ** -0.5),
        'ffn1_w_out': nrm((DEPTH, D_FF, D), D_FF ** -0.5),
        'ffn2_w_in': nrm((DEPTH, D, 2 * D_FF), D ** -0.5),
        'ffn2_w_out': nrm((DEPTH, D_FF, D), D_FF ** -0.5),
        'mix_w_in': nrm((DEPTH, D, N_IN), D ** -0.5),
        'ret_decay_logit': ret_logit[None, None, :] + nrm((DEPTH, 2, RET_HEADS), 0.05),
        'att_q_norm': 1.0 + nrm((DEPTH, ATT_HEAD_DIM), 0.05),
        'att_k_norm': 1.0 + nrm((DEPTH, ATT_HEAD_DIM), 0.05),
        'att_sink': nrm((DEPTH, ATT_HEADS), 0.5),
        'rwkv_shift': shift_base + nrm((DEPTH, 3, RWKV_IN), 0.05),
        'rwkv_w0': w0_base + nrm((DEPTH, 2, RWKV_W), 0.1),
        'rwkv_w_up': nrm((DEPTH, 2, DECAY_LORA, RWKV_W), 0.5 * DECAY_LORA ** -0.5),
        'rwkv_a0': nrm((DEPTH, 2, RWKV_W), 0.1),
        'rwkv_a_up': nrm((DEPTH, 2, AAA_LORA, RWKV_W), 0.5 * AAA_LORA ** -0.5),
        'rwkv_g_up': nrm((DEPTH, GATE_LORA, RWKV_W), GATE_LORA ** -0.5),
        'rwkv_k_k': 0.85 + nrm((DEPTH, RWKV_W), 0.02),
        'rwkv_k_a': 1.0 + nrm((DEPTH, RWKV_W), 0.02),
        'rwkv_r_k': -0.04 + nrm((DEPTH, RWKV_HEADS, RWKV_HEAD_DIM), 0.02),
        'rwkv_v0': 1.0 + nrm((DEPTH - 1, RWKV_W), 0.1),
        'rwkv_v_down': nrm((DEPTH - 1, D, MV_LORA), D ** -0.5),
        'rwkv_v_up': nrm((DEPTH - 1, MV_LORA, RWKV_W), MV_LORA ** -0.5),
        'rwkv_gn_w': 1.0 + nrm((DEPTH, RWKV_W), 0.05),
        'rwkv_gn_b': nrm((DEPTH, RWKV_W), 0.01),
        'w_branch_ret': nrm((DEPTH, RET_V_W, D), RET_V_W ** -0.5),
        'w_branch_att': nrm((DEPTH, ATT_Q_W, D), ATT_Q_W ** -0.5),
        'w_branch_rwkv': nrm((DEPTH, RWKV_W, D), RWKV_W ** -0.5),
        'w_out': nrm((DEPTH, D, D), D ** -0.5),
    }


def reference(x, c, ctx, c_ctx, ada_w, ada_b, norm_w, ffn1_w_in, ffn1_w_out, ffn2_w_in, ffn2_w_out,
              mix_w_in, ret_decay_logit, att_q_norm, att_k_norm, att_sink, rwkv_shift, rwkv_w0, rwkv_w_up,
              rwkv_a0, rwkv_a_up, rwkv_g_up, rwkv_k_k, rwkv_k_a, rwkv_r_k, rwkv_v0, rwkv_v_down, rwkv_v_up,
              rwkv_gn_w, rwkv_gn_b, w_branch_ret, w_branch_att, w_branch_rwkv, w_out):
    n_lat = x.shape[1]
    n_ctx = ctx.shape[1]
    rows = n_lat // GRID_W
    att_ang = axial_rope_angles(rows, ATT_HEAD_DIM)
    ret_ang_s = rope_angles_1d(jnp.arange(n_ctx), RET_QK_DIM, RET_ROPE_BASE)
    ret_ang_x = rope_angles_1d(n_ctx + jnp.arange(n_lat), RET_QK_DIM, RET_ROPE_BASE)
    cond_x = jax.nn.silu(c)[:, None, :]
    cond_s = jax.nn.silu(c_ctx)[None, None, :]
    s = ctx
    v_first = (None, None)
    for l in range(DEPTH):
        last = l == DEPTH - 1
        mod_x = (cond_x @ ada_w[l] + ada_b[l]).reshape(x.shape[0], 1, N_ADA, D_MODEL)
        mod_s = (cond_s @ ada_w[l] + ada_b[l]).reshape(1, 1, N_ADA, D_MODEL)
        x = ffn_sublayer(x, mod_x, 0, norm_w[l, 0], ffn1_w_in[l], ffn1_w_out[l])
        s = ffn_sublayer(s, mod_s, 0, norm_w[l, 0], ffn1_w_in[l], ffn1_w_out[l])
        p = {
            'w_in': mix_w_in[l], 'ret_decay_logit': ret_decay_logit[l],
            'q_norm': att_q_norm[l], 'k_norm': att_k_norm[l], 'att_sink': att_sink[l],
            'shift': rwkv_shift[l], 'w0': rwkv_w0[l], 'w_up': rwkv_w_up[l],
            'a0': rwkv_a0[l], 'a_up': rwkv_a_up[l], 'g_up': rwkv_g_up[l],
            'k_k': rwkv_k_k[l], 'k_a': rwkv_k_a[l], 'r_k': rwkv_r_k[l],
            'gn_w': rwkv_gn_w[l], 'gn_b': rwkv_gn_b[l],
            'w_ret': w_branch_ret[l], 'w_att': w_branch_att[l], 'w_rwkv': w_branch_rwkv[l],
            'w_out': w_out[l],
        }
        if l > 0:
            p['v0'] = rwkv_v0[l - 1]
            p['v_down'] = rwkv_v_down[l - 1]
            p['v_up'] = rwkv_v_up[l - 1]
        hx = modulate(rms_norm(x, norm_w[l, 1]), mod_x[:, :, 3], mod_x[:, :, 4])
        hs = modulate(rms_norm(s, norm_w[l, 1]), mod_s[:, :, 3], mod_s[:, :, 4])
        out_x, out_s, v_first = token_mixing(hx, hs, p, v_first, att_ang, ret_ang_x, ret_ang_s, not last)
        x = x + mod_x[:, :, 5] * out_x
        x = ffn_sublayer(x, mod_x, 6, norm_w[l, 2], ffn2_w_in[l], ffn2_w_out[l])
        if not last:
            s = s + mod_s[:, :, 5] * out_s
            s = ffn_sublayer(s, mod_s, 6, norm_w[l, 2], ffn2_w_in[l], ffn2_w_out[l])
    return x
```

```python
import functools
import math

import numpy as np
import jax
import jax.numpy as jnp
from jax import lax
from jax.experimental import pallas as pl
from jax.experimental.pallas import tpu as pltpu

F32 = jnp.float32
BF16 = jnp.bfloat16
HI = lax.Precision.HIGHEST

D = 1024
N_LAT = 2048
N_CTX = 256
GRID_W = 64
N_ADA = 9
D_FF = 2816
NORM_EPS = 1e-6
RET_CHUNK = 128
ATT_BLOCK = 128
WINDOW = 128
RW_CHUNK = 64
RWKV_GN_EPS = 64e-5
RWKV_DECAY_SCALE = 0.6065306597126334
HEAD64 = 64
LANES = 128

C_RW = 0
C_RQ = 2048
C_RV = 4096
C_RG = 5120
C_AQ = 6144
C_AK = 7168
C_AV = 7680
C_GT = 8192
N_PROJ = 11264

VMEM_LIMIT = 56 * 1024 * 1024


def _dot(a, b, prec=None):
    return jnp.dot(a, b, preferred_element_type=F32, precision=prec)


def _dot_nt(a, b, prec=None):
    return lax.dot_general(a, b, (((1,), (1,)), ((), ())), preferred_element_type=F32, precision=prec)


def _dot_tn(a, b, prec=None):
    return lax.dot_general(a, b, (((0,), (0,)), ((), ())), preferred_element_type=F32, precision=prec)


def _sigmoid(x):
    return 1.0 / (1.0 + jnp.exp(-x))


def _params(sem):
    return pltpu.CompilerParams(dimension_semantics=sem, vmem_limit_bytes=VMEM_LIMIT)


def _ada_kernel(c_ref, w_ref, b_ref, o_ref):
    c = c_ref[...]
    o_ref[...] = _dot(c * _sigmoid(c), w_ref[...], HI) + b_ref[...]


def _ada_call(cond, w, b):
    rows = cond.shape[0]
    tn = 1024
    n = w.shape[1]
    return pl.pallas_call(
        _ada_kernel,
        grid=(n // tn,),
        in_specs=[pl.BlockSpec((rows, D), lambda j: (0, 0)),
                  pl.BlockSpec((D, tn), lambda j: (0, j)),
                  pl.BlockSpec((1, tn), lambda j: (0, j))],
        out_specs=pl.BlockSpec((rows, tn), lambda j: (0, j)),
        out_shape=jax.ShapeDtypeStruct((rows, n), F32),
        compiler_params=_params(("arbitrary",)),
        name="ada",
    )(cond, w, b.reshape(1, n))


def _mod_row(tok0, ct):
    return jnp.where(tok0 < ct, 0, 1 + (tok0 - ct) // N_LAT)


def _norm_mod(x, nw, shift, scale):
    y = x * lax.rsqrt(jnp.mean(x * x, -1, keepdims=True) + NORM_EPS) * nw
    return y * (1.0 + scale) + shift


def _ffn_kernel(x_ref, mod_ref, nw_ref, wg_ref, wu_ref, wo_ref, o_ref, h_sc, acc_sc, *, base, n_ff):
    j = pl.program_id(1)

    @pl.when(j == 0)
    def _():
        h = _norm_mod(x_ref[...], nw_ref[...], mod_ref[0, base:base + 1, :], mod_ref[0, base + 1:base + 2, :])
        h_sc[...] = h.astype(BF16)
        acc_sc[...] = jnp.zeros_like(acc_sc)

    h = h_sc[...]
    g = _dot(h, wg_ref[...])
    u = _dot(h, wu_ref[...])
    act = g * _sigmoid(g) * u
    acc_sc[...] += _dot(act.astype(BF16), wo_ref[...])

    @pl.when(j == n_ff - 1)
    def _():
        o_ref[...] = x_ref[...] + 0.5 * mod_ref[0, base + 2:base + 3, :] * acc_sc[...]


def _ffn_call(x, mod, nw, w_in, w_out, *, base, ct, tm, tile0, n_tiles):
    tf = 256
    n_ff = D_FF // tf
    nt = x.shape[0]
    kern = functools.partial(_ffn_kernel, base=base, n_ff=n_ff)
    return pl.pallas_call(
        kern,
        grid=(n_tiles, n_ff),
        in_specs=[pl.BlockSpec((tm, D), lambda i, j: (i + tile0, 0)),
                  pl.BlockSpec((1, N_ADA, D), lambda i, j: (_mod_row((i + tile0) * tm, ct), 0, 0)),
                  pl.BlockSpec((1, D), lambda i, j: (0, 0)),
                  pl.BlockSpec((D, tf), lambda i, j: (0, j)),
                  pl.BlockSpec((D, tf), lambda i, j: (0, j + n_ff)),
                  pl.BlockSpec((tf, D), lambda i, j: (j, 0))],
        out_specs=pl.BlockSpec((tm, D), lambda i, j: (i, 0)),
        out_shape=jax.ShapeDtypeStruct((n_tiles * tm, D), F32),
        scratch_shapes=[pltpu.VMEM((tm, D), BF16), pltpu.VMEM((tm, D), F32)],
        compiler_params=_params(("parallel", "arbitrary")),
        name="ffn",
    )(x, mod, nw.reshape(1, D), w_in, w_in, w_out)


def _proj_kernel(x_ref, mod_ref, nw_ref, w_ref, o_ref, h_sc):
    @pl.when(pl.program_id(1) == 0)
    def _():
        h = _norm_mod(x_ref[...], nw_ref[...], mod_ref[0, 3:4, :], mod_ref[0, 4:5, :])
        h_sc[...] = h.astype(BF16)

    o_ref[...] = _dot(h_sc[...], w_ref[...])


def _proj_call(x, mod, nw, w, *, ct, tm):
    tn = 512
    nt = x.shape[0]
    return pl.pallas_call(
        _proj_kernel,
        grid=(nt // tm, N_PROJ // tn),
        in_specs=[pl.BlockSpec((tm, D), lambda i, j: (i, 0)),
                  pl.BlockSpec((1, N_ADA, D), lambda i, j: (_mod_row(i * tm, ct), 0, 0)),
                  pl.BlockSpec((1, D), lambda i, j: (0, 0)),
                  pl.BlockSpec((D, tn), lambda i, j: (0, j))],
        out_specs=pl.BlockSpec((tm, tn), lambda i, j: (i, j)),
        out_shape=jax.ShapeDtypeStruct((nt, N_PROJ), F32),
        scratch_shapes=[pltpu.VMEM((tm, D), BF16)],
        compiler_params=_params(("parallel", "arbitrary")),
        name="proj",
    )(x, mod, nw.reshape(1, D), w)


def _scan_blocks(b, d, s, *, nc_ctx, nc_lat, bsz):
    in_ctx = s < nc_ctx
    cs = jnp.where(d == 0, s, nc_ctx - 1 - s)
    ls = jnp.where(d == 0, s - nc_ctx, nc_lat - 1 - (s - nc_ctx))
    tok = jnp.where(in_ctx, b * nc_ctx + cs, bsz * nc_ctx + b * nc_lat + ls)
    pos = jnp.where(in_ctx, cs, nc_ctx + ls)
    return tok, pos


def _ret_kernel(lg_ref, q_ref, qr_ref, k_ref, kr_ref, v_ref, cos_ref, sin_ref, dist_ref, ze_ref, xe_ref,
                o_ref, st_sc):
    d = pl.program_id(1)
    s = pl.program_id(2)

    @pl.when(s == 0)
    def _():
        st_sc[...] = jnp.zeros_like(st_sc)

    cos = cos_ref[...]
    sin = sin_ref[...]
    q = q_ref[...] * cos + qr_ref[...] * sin
    k = (k_ref[...] * cos + kr_ref[...] * sin) * (HEAD64 ** -0.5)
    dist = dist_ref[...]
    valid = dist >= 0.0
    ze = ze_ref[...]
    xe = xe_ref[...]
    lane = lax.broadcasted_iota(jnp.int32, (1, LANES), 1)
    for h in range(8):
        lg = lg_ref[d, h]
        p = h // 2
        hmask = ((lane // HEAD64) == (h % 2)).astype(F32)
        qm = (q[:, p * LANES:(p + 1) * LANES] * hmask).astype(BF16)
        kp = k[:, p * LANES:(p + 1) * LANES].astype(BF16)
        vh = v_ref[:, h * LANES:(h + 1) * LANES]
        decay = jnp.where(valid, jnp.exp(lg * dist), 0.0)
        sc = _dot_nt(qm, kp) * decay
        st = st_sc[h]
        o = _dot(sc.astype(BF16), vh.astype(BF16)) + _dot(qm, st.astype(BF16)) * jnp.exp(lg * xe)
        o_ref[:, h * LANES:(h + 1) * LANES] = o
        u = _dot_tn(kp, (vh * jnp.exp(lg * ze)).astype(BF16))
        st_sc[h] = st * jnp.exp(lg * jnp.full((1, LANES), float(RET_CHUNK), F32)) + u


def _ret_call(proj, log_g, cos, sin, dist, ze, xe, *, bsz):
    nt = proj.shape[0]
    c = RET_CHUNK
    nc_ctx, nc_lat = N_CTX // c, N_LAT // c
    steps = nc_ctx + nc_lat
    blocks = functools.partial(_scan_blocks, nc_ctx=nc_ctx, nc_lat=nc_lat, bsz=bsz)

    def tok_map(col):
        return lambda b, d, s, lg: (blocks(b, d, s)[0], col)

    pos_map = lambda b, d, s, lg: (blocks(b, d, s)[1], 0)
    dir_map = lambda b, d, s, lg: (d, 0, 0)
    grid_spec = pltpu.PrefetchScalarGridSpec(
        num_scalar_prefetch=1,
        grid=(bsz, 2, steps),
        in_specs=[pl.BlockSpec((c, 512), tok_map(C_RQ // 512)),
                  pl.BlockSpec((c, 512), tok_map(C_RQ // 512 + 1)),
                  pl.BlockSpec((c, 512), tok_map(C_RQ // 512 + 2)),
                  pl.BlockSpec((c, 512), tok_map(C_RQ // 512 + 3)),
                  pl.BlockSpec((c, 1024), tok_map(C_RV // 1024)),
                  pl.BlockSpec((c, 512), pos_map),
                  pl.BlockSpec((c, 512), pos_map),
                  pl.BlockSpec((None, c, c), dir_map),
                  pl.BlockSpec((None, c, LANES), dir_map),
                  pl.BlockSpec((None, c, LANES), dir_map)],
        out_specs=pl.BlockSpec((None, c, 1024), lambda b, d, s, lg: (d, blocks(b, d, s)[0], 0)),
        scratch_shapes=[pltpu.VMEM((8, LANES, LANES), F32)],
    )
    return pl.pallas_call(
        _ret_kernel,
        grid_spec=grid_spec,
        out_shape=jax.ShapeDtypeStruct((2, nt, 1024), F32),
        compiler_params=_params(("parallel", "parallel", "arbitrary")),
        name="retention",
    )(log_g, proj, proj, proj, proj, proj, cos, sin, dist, ze, xe)


NEG = -1e30


def _att_kernel(sink_ref, q_ref, qr_ref, qcos_ref, qsin_ref, qw_ref, qwr_ref, bo_ref,
                kl_ref, klr_ref, vl_ref, kcos_ref, ksin_ref, kc_ref, vc_ref, kw_ref, kwr_ref,
                o_ref, kn_sc, kcn_sc, *, n_ctx_blk):
    hk = pl.program_id(1)
    jb = pl.program_id(2)
    blk = ATT_BLOCK

    @pl.when(jb == 0)
    def _():
        kl = kl_ref[...]
        rs = lax.rsqrt(jnp.sum(kl * kl, -1, keepdims=True) * (0.5 / HEAD64) + NORM_EPS)
        kn_sc[...] = ((kl * kw_ref[...]) * kcos_ref[...] + (klr_ref[...] * kwr_ref[...]) * ksin_ref[...]) * rs
        kc = kc_ref[...]
        rc = lax.rsqrt(jnp.sum(kc * kc, -1, keepdims=True) * (0.5 / HEAD64) + NORM_EPS)
        kcn_sc[...] = kc * kw_ref[...] * rc

    q = q_ref[...]
    ms = _dot(q * q, bo_ref[...], HI) * (1.0 / HEAD64)
    rs = lax.rsqrt(ms + NORM_EPS)
    qn = ((q * qw_ref[...]) * qcos_ref[...] + (qr_ref[...] * qwr_ref[...]) * qsin_ref[...]) * rs * (HEAD64 ** -0.5)

    reach = jnp.where(jb >= n_ctx_blk, WINDOW, -1)
    lb = jnp.maximum(jb - n_ctx_blk, 0)
    n_win = 3 * blk
    start = jnp.clip((lb - 1) * blk, 0, N_LAT - n_win)
    start = pl.multiple_of(start, blk)
    kwin = kn_sc[pl.ds(start, n_win), :].astype(BF16)
    vwin = vl_ref[pl.ds(start, n_win), :].astype(BF16)
    kctx = kcn_sc[...].astype(BF16)
    vctx = vc_ref[...].astype(BF16)
    qpos = lb * blk + lax.broadcasted_iota(jnp.int32, (blk, n_win), 0)
    kpos = start + lax.broadcasted_iota(jnp.int32, (blk, n_win), 1)
    valid = jnp.abs(kpos - qpos) <= reach
    lane = lax.broadcasted_iota(jnp.int32, (1, LANES), 1)
    outs = []
    for p in range(2):
        acc = jnp.zeros((blk, LANES), F32)
        for e in range(2):
            g = 2 * p + e
            hmask = ((lane // HEAD64) == e).astype(F32)
            qg = (qn[:, p * LANES:(p + 1) * LANES] * hmask).astype(BF16)
            s_win = jnp.where(valid, _dot_nt(qg, kwin), NEG)
            s_ctx = _dot_nt(qg, kctx)
            sink = sink_ref[hk * 4 + g]
            m = jnp.maximum(jnp.maximum(jnp.max(s_win, -1, keepdims=True), jnp.max(s_ctx, -1, keepdims=True)), sink)
            p_win = jnp.exp(s_win - m)
            p_ctx = jnp.exp(s_ctx - m)
            den = jnp.sum(p_win, -1, keepdims=True) + jnp.sum(p_ctx, -1, keepdims=True) + jnp.exp(sink - m)
            og = (_dot(p_win.astype(BF16), vwin) + _dot(p_ctx.astype(BF16), vctx)) / den
            acc = acc + og * hmask
        outs.append(acc)
    o_ref[...] = jnp.concatenate(outs, axis=-1)


def _att_call(proj, sink, qcos, qsin, kcos, ksin, qw, qwr, kw, kwr, bo, *, bsz):
    nt = proj.shape[0]
    blk = ATT_BLOCK
    n_ctx_blk, n_lat_blk = N_CTX // blk, N_LAT // blk
    steps = n_ctx_blk + n_lat_blk

    def qtok(b, jb):
        return jnp.where(jb < n_ctx_blk, b * n_ctx_blk + jb, bsz * n_ctx_blk + b * n_lat_blk + (jb - n_ctx_blk))

    def qmap(col0):
        return lambda b, hk, jb, sk: (qtok(b, jb), col0 + hk)

    qpos_map = lambda b, hk, jb, sk: (jb, 0)
    const = lambda b, hk, jb, sk: (0, 0)
    ct_rows = bsz * N_CTX

    def lat_map(col0):
        return lambda b, hk, jb, sk: (ct_rows // N_LAT + b, col0 + hk)

    def ctx_map(col0):
        return lambda b, hk, jb, sk: (b, col0 + hk)

    grid_spec = pltpu.PrefetchScalarGridSpec(
        num_scalar_prefetch=1,
        grid=(bsz, 2, steps),
        in_specs=[pl.BlockSpec((blk, 256), qmap(C_AQ // 256)),
                  pl.BlockSpec((blk, 256), qmap(C_AQ // 256 + 2)),
                  pl.BlockSpec((blk, 256), qpos_map),
                  pl.BlockSpec((blk, 256), qpos_map),
                  pl.BlockSpec((1, 256), const),
                  pl.BlockSpec((1, 256), const),
                  pl.BlockSpec((256, 256), const),
                  pl.BlockSpec((N_LAT, LANES), lat_map(C_AK // LANES)),
                  pl.BlockSpec((N_LAT, LANES), lat_map(C_AK // LANES + 2)),
                  pl.BlockSpec((N_LAT, LANES), lat_map(C_AV // LANES)),
                  pl.BlockSpec((N_LAT, LANES), const),
                  pl.BlockSpec((N_LAT, LANES), const),
                  pl.BlockSpec((N_CTX, LANES), ctx_map(C_AK // LANES)),
                  pl.BlockSpec((N_CTX, LANES), ctx_map(C_AV // LANES)),
                  pl.BlockSpec((1, LANES), const),
                  pl.BlockSpec((1, LANES), const)],
        out_specs=pl.BlockSpec((blk, 256), lambda b, hk, jb, sk: (qtok(b, jb), hk)),
        scratch_shapes=[pltpu.VMEM((N_LAT, LANES), F32), pltpu.VMEM((N_CTX, LANES), F32)],
    )
    kern = functools.partial(_att_kernel, n_ctx_blk=n_ctx_blk)
    return pl.pallas_call(
        kern,
        grid_spec=grid_spec,
        out_shape=jax.ShapeDtypeStruct((nt, 512), F32),
        compiler_params=_params(("parallel", "parallel", "arbitrary")),
        name="attention",
    )(sink, proj, proj, qcos, qsin, qw, qwr, bo, proj, proj, proj, kcos, ksin, proj, proj, kw, kwr)


RW_TM = 256


def _rw_prep_kernel(cur_ref, prev_ref, next_ref, shift_ref, wlo_ref, gup_ref, vup_ref, bo_ref, vec_ref, vf_ref,
                    r_o, v_o, kk_o, g_o, bonus_o, lw_o, kd_o, bv_o, *, ct, mix_v):
    i = pl.program_id(0)
    tm = RW_TM
    tok0 = i * tm
    lat_off = tok0 - ct
    is_start = jnp.where(tok0 < ct, tok0 % N_CTX == 0, lat_off % N_LAT == 0)
    is_end = jnp.where(tok0 < ct, (tok0 + tm) % N_CTX == 0, (lat_off + tm) % N_LAT == 0)
    cur = cur_ref[...]
    row = lax.broadcasted_iota(jnp.int32, (tm, 1), 0)
    prev_row = jnp.where(is_start, 0.0, prev_ref[7:8, :])
    next_row = jnp.where(is_end, 0.0, next_ref[0:1, :])
    prev = jnp.where(row == 0, prev_row, pltpu.roll(cur, 1, 0))
    nxt = jnp.where(row == tm - 1, next_row, pltpu.roll(cur, tm - 1, 0))
    c = prev * shift_ref[0:1, :] + cur * shift_ref[1:2, :] + nxt * shift_ref[2:3, :]

    r = c[:, 0:512]
    k = c[:, 512:1024]
    v = c[:, 1024:1536]
    lo = c[:, 1536:1792]
    gd = c[:, 1792:1920]
    vl = c[:, 1920:2048]

    vec = vec_ref[...]
    k_k, k_a, r_k, v0 = vec[0:1], vec[1:2], vec[2:3], vec[3:4]
    if mix_v:
        v = v + (vf_ref[...] - v) * _sigmoid(v0 + _dot(vl, vup_ref[...], HI))
    g = _dot(_sigmoid(gd), gup_ref[...], HI)
    kk = k * k_k
    ss = _dot(kk * kk, bo_ref[...], HI)
    kk = kk / jnp.maximum(jnp.sqrt(ss), 1e-12)
    lora = _dot(jnp.concatenate([jnp.tanh(lo), lo], axis=-1), wlo_ref[...], HI)
    ksum = jnp.zeros_like(k)
    for d in range(2):
        z = vec[4 + d:5 + d] + lora[:, d * 512:(d + 1) * 512]
        a = _sigmoid(vec[6 + d:7 + d] + lora[:, (2 + d) * 512:(3 + d) * 512])
        kd = k * (1.0 + (a - 1.0) * k_a)
        lw_o[d] = -RWKV_DECAY_SCALE * _sigmoid(z)
        kd_o[d] = kd
        bv_o[d] = a * kk
        ksum = ksum + kd
    bonus_o[...] = _dot(r * ksum * r_k, bo_ref[...], HI) * v
    r_o[...] = r
    v_o[...] = v
    kk_o[...] = kk
    g_o[...] = g


def _rw_prep_call(proj, shift, wlo, gup, vup, bo, vec, vfirst, *, ct, mix_v):
    nt = proj.shape[0]
    tm = RW_TM
    nblk8 = nt // 8
    tok = pl.BlockSpec((tm, 512), lambda i: (i, 0))
    dirs = pl.BlockSpec((2, tm, 512), lambda i: (0, i, 0))
    const = lambda i: (0, 0)
    kern = functools.partial(_rw_prep_kernel, ct=ct, mix_v=mix_v)
    one = jax.ShapeDtypeStruct((nt, 512), F32)
    two = jax.ShapeDtypeStruct((2, nt, 512), F32)
    return pl.pallas_call(
        kern,
        grid=(nt // tm,),
        in_specs=[pl.BlockSpec((tm, 2048), lambda i: (i, C_RW // 2048)),
                  pl.BlockSpec((8, 2048), lambda i: (jnp.maximum(i * (tm // 8) - 1, 0), C_RW // 2048)),
                  pl.BlockSpec((8, 2048), lambda i: (jnp.minimum((i + 1) * (tm // 8), nblk8 - 1), C_RW // 2048)),
                  pl.BlockSpec((3, 2048), const),
                  pl.BlockSpec((512, 2048), const),
                  pl.BlockSpec((128, 512), const),
                  pl.BlockSpec((128, 512), const),
                  pl.BlockSpec((512, 512), const),
                  pl.BlockSpec((8, 512), const),
                  tok],
        out_specs=[tok, tok, tok, tok, tok, dirs, dirs, dirs],
        out_shape=[one, one, one, one, one, two, two, two],
        compiler_params=_params(("parallel",)),
        name="rwkv_prep",
    )(proj, proj, proj, shift, wlo, gup, vup, bo, vec, vfirst)


def _rw_scan_kernel(r_ref, v_ref, kk_ref, lw_ref, kd_ref, bv_ref, lcum_ref, mbig_ref, eye_ref, y_ref, st_sc):
    s = pl.program_id(2)
    t = RW_CHUNK

    @pl.when(s == 0)
    def _():
        st_sc[...] = jnp.zeros_like(st_sc)

    lw = lw_ref[...]
    cum = _dot(lcum_ref[...], lw, HI)
    tot = jnp.sum(lw, axis=0, keepdims=True)
    e_incl = jnp.exp(cum)
    e_excl = jnp.exp(cum - lw)
    e_inv = jnp.exp(-cum)
    e_rem = jnp.exp(tot - cum)
    w_tot = jnp.exp(tot)
    kk = kk_ref[...]
    kd = kd_ref[...]
    bv = bv_ref[...]
    aw = -kk * e_excl
    rt = r_ref[...] * e_incl
    bi = bv * e_inv
    ki = kd * e_inv
    bp = bv * e_rem
    kp = kd * e_rem
    v = v_ref[...]
    mbig = mbig_ref[...]
    eye = eye_ref[...]
    lane = lax.broadcasted_iota(jnp.int32, (1, LANES), 1)
    m_e = (lane < HEAD64).astype(F32)
    m_o = 1.0 - m_e

    def stack2(x):
        return jnp.concatenate([x * m_e, x * m_o], axis=0)

    for p in range(4):
        sl = slice(p * LANES, (p + 1) * LANES)
        aw2, rt2 = stack2(aw[:, sl]), stack2(rt[:, sl])
        v2 = stack2(v[:, sl])
        big = _dot_nt(jnp.concatenate([aw2, rt2], axis=0),
                      jnp.concatenate([stack2(bi[:, sl]), stack2(ki[:, sl])], axis=0), HI) * mbig
        a_ab, a_ak = big[0:2 * t, 0:2 * t], big[0:2 * t, 2 * t:4 * t]
        a_rb, a_rk = big[2 * t:4 * t, 0:2 * t], big[2 * t:4 * t, 2 * t:4 * t]
        rhs = jnp.concatenate([aw2, _dot(a_ak, v2, HI)], axis=1)
        apow = a_ab
        n_sq = int(math.log2(t))
        for it in range(n_sq):
            rhs = rhs + _dot(apow, rhs, HI)
            if it < n_sq - 1:
                apow = _dot(apow, apow, HI)
        tmp = _dot(a_rb, rhs, HI)
        yq2 = rt2 + tmp[:, 0:LANES]
        y02 = tmp[:, LANES:2 * LANES] + _dot(a_rk, v2, HI)
        gu = _dot_tn(stack2(bp[:, sl]), rhs, HI)
        g_mat = gu[:, 0:LANES] + eye * w_tot[:, sl]
        u_mat = gu[:, LANES:2 * LANES] + _dot_tn(stack2(kp[:, sl]), v2, HI)
        st = st_sc[p]
        y2 = _dot(yq2, st, HI) + y02
        y_ref[:, sl] = y2[0:t] + y2[t:2 * t]
        st_sc[p] = _dot(g_mat, st, HI) + u_mat


def _rw_scan_call(r, v, kk, lw, kd, bv, lcum, mbig, eye, *, bsz):
    nt = r.shape[0]
    t = RW_CHUNK
    nc_ctx, nc_lat = N_CTX // t, N_LAT // t
    steps = nc_ctx + nc_lat
    blocks = functools.partial(_scan_blocks, nc_ctx=nc_ctx, nc_lat=nc_lat, bsz=bsz)
    tok = pl.BlockSpec((t, 512), lambda b, d, s: (blocks(b, d, s)[0], 0))
    tokd = pl.BlockSpec((None, t, 512), lambda b, d, s: (d, blocks(b, d, s)[0], 0))
    return pl.pallas_call(
        _rw_scan_kernel,
        grid=(bsz, 2, steps),
        in_specs=[tok, tok, tok, tokd, tokd, tokd,
                  pl.BlockSpec((None, t, t), lambda b, d, s: (d, 0, 0)),
                  pl.BlockSpec((None, 4 * t, 4 * t), lambda b, d, s: (d, 0, 0)),
                  pl.BlockSpec((LANES, LANES), lambda b, d, s: (0, 0))],
        out_specs=tokd,
        out_shape=jax.ShapeDtypeStruct((2, nt, 512), F32),
        scratch_shapes=[pltpu.VMEM((4, LANES, LANES), F32)],
        compiler_params=_params(("parallel", "parallel", "arbitrary")),
        name="rwkv_scan",
    )(r, v, kk, lw, kd, bv, lcum, mbig, eye)


MERGE_TM = 256


def _merge_kernel(x_ref, mod_ref, oret_ref, rg_ref, att_ref, y_ref, bonus_ref, g_ref, gt0_ref, gt1_ref, gt2_ref,
                  bo_ref, gn_ref, wr_ref, wa_ref, ww_ref, wo_ref, o_ref):
    o = oret_ref[0] + oret_ref[1]
    parts = []
    for h in range(8):
        oh = o[:, h * LANES:(h + 1) * LANES]
        mu = jnp.mean(oh, -1, keepdims=True)
        dv = oh - mu
        var = jnp.mean(dv * dv, -1, keepdims=True)
        parts.append(dv * lax.rsqrt(var + NORM_EPS))
    rg = rg_ref[...]
    ret = (rg * _sigmoid(rg)) * jnp.concatenate(parts, axis=-1)

    y = y_ref[0] + y_ref[1]
    bo = bo_ref[...]
    mu = _dot(y, bo, HI) * (1.0 / HEAD64)
    dy = y - mu
    var = _dot(dy * dy, bo, HI) * (1.0 / HEAD64)
    yn = dy * lax.rsqrt(var + RWKV_GN_EPS) * gn_ref[0:1, :] + gn_ref[1:2, :]
    rw = (yn + bonus_ref[...]) * g_ref[...]

    merged = (_sigmoid(gt0_ref[...]) * _dot(ret.astype(BF16), wr_ref[...])
              + _sigmoid(gt1_ref[...]) * _dot(att_ref[...].astype(BF16), wa_ref[...])
              + _sigmoid(gt2_ref[...]) * _dot(rw.astype(BF16), ww_ref[...]))
    out = _dot(merged.astype(BF16), wo_ref[...])
    o_ref[...] = x_ref[...] + mod_ref[0, 5:6, :] * out


def _merge_call(x, mod, proj, oret, att, y, bonus, g, bo, gn, wr, wa, ww, wo, *, ct, tm, tile0, n_tiles):
    tok = lambda w, col=0: pl.BlockSpec((tm, w), lambda i: (i + tile0, col))
    tok2 = lambda w: pl.BlockSpec((2, tm, w), lambda i: (0, i + tile0, 0))
    const = lambda i: (0, 0)
    return pl.pallas_call(
        _merge_kernel,
        grid=(n_tiles,),
        in_specs=[tok(D),
                  pl.BlockSpec((1, N_ADA, D), lambda i: (_mod_row((i + tile0) * tm, ct), 0, 0)),
                  tok2(1024), tok(1024, C_RG // 1024), tok(512), tok2(512), tok(512), tok(512),
                  tok(D, C_GT // D), tok(D, C_GT // D + 1), tok(D, C_GT // D + 2),
                  pl.BlockSpec((512, 512), const),
                  pl.BlockSpec((2, 512), const),
                  pl.BlockSpec((1024, D), const),
                  pl.BlockSpec((512, D), const),
                  pl.BlockSpec((512, D), const),
                  pl.BlockSpec((D, D), const)],
        out_specs=pl.BlockSpec((tm, D), lambda i: (i, 0)),
        out_shape=jax.ShapeDtypeStruct((n_tiles * tm, D), F32),
        compiler_params=_params(("parallel",)),
        name="merge",
    )(x, mod, oret, proj, att, y, bonus, g, proj, proj, proj, bo, gn, wr, wa, ww, wo)


def _rope_1d(pos, dim, base):
    n_freq = dim // 2
    inv = np.power(np.float32(base), -(np.arange(n_freq, dtype=np.float32) / np.float32(n_freq))).astype(np.float32)
    return pos.astype(np.float32)[:, None] * inv[None, :]


def _tables():
    ang = _rope_1d(np.arange(N_CTX + N_LAT), HEAD64, 10000.0)
    rcos = np.tile(np.concatenate([np.cos(ang), np.cos(ang)], -1), (1, 8)).astype(np.float32)
    rsin = np.tile(np.concatenate([np.sin(ang), np.sin(ang)], -1), (1, 8)).astype(np.float32)
    rows = N_LAT // GRID_W
    row = np.repeat(np.arange(rows), GRID_W)
    col = np.arange(rows * GRID_W) % GRID_W
    aang = np.concatenate([_rope_1d(row, HEAD64 // 2, 10000.0), _rope_1d(col, HEAD64 // 2, 10000.0)], -1)
    ac = np.concatenate([np.cos(aang), np.cos(aang)], -1).astype(np.float32)
    asn = np.concatenate([np.sin(aang), np.sin(aang)], -1).astype(np.float32)
    qcos = np.concatenate([np.ones((N_CTX, 256), np.float32), np.tile(ac, (1, 4))], 0)
    qsin = np.concatenate([np.zeros((N_CTX, 256), np.float32), np.tile(asn, (1, 4))], 0)
    kcos = np.tile(ac, (1, 2))
    ksin = np.tile(asn, (1, 2))
    c = RET_CHUNK
    pos = np.arange(c, dtype=np.float32)
    diff = pos[:, None] - pos[None, :]
    dist = np.stack([np.where(diff >= 0, diff, -1.0), np.where(diff <= 0, -diff, -1.0)]).astype(np.float32)
    ze = np.stack([c - 1.0 - pos, pos]).astype(np.float32)
    xe = np.stack([pos + 1.0, c - pos]).astype(np.float32)
    ze = np.broadcast_to(ze[:, :, None], (2, c, LANES)).copy()
    xe = np.broadcast_to(xe[:, :, None], (2, c, LANES)).copy()
    t = RW_CHUNK
    ti = np.arange(t)
    low_incl = (ti[None, :] <= ti[:, None]).astype(np.float32)
    low_strict = (ti[None, :] < ti[:, None]).astype(np.float32)
    lcum = np.stack([low_incl, low_incl.T])
    eye2 = np.eye(2, dtype=np.float32)
    mbig = []
    for strict, incl in ((low_strict, low_incl), (low_strict.T, low_incl.T)):
        s2 = np.kron(eye2, strict)
        i2 = np.kron(eye2, incl)
        mbig.append(np.block([[s2, s2], [i2, i2]]))
    mbig = np.stack(mbig).astype(np.float32)
    eye = np.eye(LANES, dtype=np.float32)
    bo64 = np.kron(np.eye(8, dtype=np.float32), np.ones((HEAD64, HEAD64), np.float32))
    return dict(rcos=rcos, rsin=rsin, qcos=qcos, qsin=qsin, kcos=kcos, ksin=ksin, dist=dist, ze=ze, xe=xe,
                lcum=lcum, mbig=mbig, eye=eye, bo64=bo64)


def _rot_cols(n_heads):
    half = HEAD64 // 2
    idx, sgn = [], []
    for h in range(n_heads):
        base = h * HEAD64
        idx += list(range(base + half, base + HEAD64)) + list(range(base, base + half))
        sgn += [-1.0] * half + [1.0] * half
    return np.array(idx), np.array(sgn, np.float32)


def _proj_weight(w_in, v_down, dtype=BF16):
    o = np.cumsum([0, 512, 512, 1024, 1024, 512, 128, 128, 1920, 3072])
    rq, rk, rv, rg, aq, ak, av, rw, gt = (w_in[:, o[i]:o[i + 1]] for i in range(9))
    i8, s8 = _rot_cols(8)
    i2, s2 = _rot_cols(2)
    dup = lambda m: jnp.concatenate([m[:, 0:64], m[:, 0:64], m[:, 64:128], m[:, 64:128]], -1)
    z = lambda n: jnp.zeros((D, n), w_in.dtype)
    vd = z(32) if v_down is None else v_down
    cols = [rw, vd, z(96),
            rq, rq[:, i8] * s8, rk, rk[:, i8] * s8,
            rv, rg,
            aq, aq[:, i8] * s8,
            dup(ak), dup(ak[:, i2] * s2),
            dup(av), z(256),
            gt]
    w = jnp.concatenate(cols, -1)
    assert w.shape == (D, N_PROJ), w.shape
    return w.astype(dtype)


def kernel(x, c, ctx, c_ctx, ada_w, ada_b, norm_w, ffn1_w_in, ffn1_w_out, ffn2_w_in, ffn2_w_out, mix_w_in, ret_decay_logit, att_q_norm, att_k_norm, att_sink, rwkv_shift, rwkv_w0, rwkv_w_up, rwkv_a0, rwkv_a_up, rwkv_g_up, rwkv_k_k, rwkv_k_a, rwkv_r_k, rwkv_v0, rwkv_v_down, rwkv_v_up, rwkv_gn_w, rwkv_gn_b, w_branch_ret, w_branch_att, w_branch_rwkv, w_out):
    bsz = x.shape[0]
    depth = ada_w.shape[0]
    ct = bsz * N_CTX
    nt = ct + bsz * N_LAT
    tm = math.gcd(1024, ct)
    tb = {k: jnp.asarray(v) for k, v in _tables().items()}

    xs = jnp.concatenate([ctx.reshape(ct, D), x.reshape(bsz * N_LAT, D)], 0)
    rows = 8 * ((bsz + 1 + 7) // 8)
    cond = jnp.zeros((rows, D), F32).at[0].set(c_ctx).at[1:bsz + 1].set(c)
    i8, _ = _rot_cols(8)
    i2, _ = _rot_cols(2)
    vfirst = None
    for l in range(depth):
        last = l == depth - 1
        mod = _ada_call(cond, ada_w[l], ada_b[l]).reshape(rows, N_ADA, D)
        f1_in, f1_out = ffn1_w_in[l].astype(BF16), ffn1_w_out[l].astype(BF16)
        f2_in, f2_out = ffn2_w_in[l].astype(BF16), ffn2_w_out[l].astype(BF16)
        xs = _ffn_call(xs, mod, norm_w[l, 0], f1_in, f1_out, base=0, ct=ct, tm=tm, tile0=0, n_tiles=nt // tm)

        proj = _proj_call(xs, mod, norm_w[l, 1], _proj_weight(mix_w_in[l], rwkv_v_down[l - 1] if l > 0 else None),
                          ct=ct, tm=tm)

        log_g = jax.nn.log_sigmoid(ret_decay_logit[l].astype(F32))
        oret = _ret_call(proj, log_g, tb["rcos"], tb["rsin"], tb["dist"], tb["ze"], tb["xe"], bsz=bsz)

        qw = jnp.tile(att_q_norm[l], 4).reshape(1, 256)
        qwr = jnp.tile(att_q_norm[l][i2[:64]], 4).reshape(1, 256)
        kw = jnp.tile(att_k_norm[l], 2).reshape(1, 128)
        kwr = jnp.tile(att_k_norm[l][i2[:64]], 2).reshape(1, 128)
        att = _att_call(proj, att_sink[l].astype(F32), tb["qcos"], tb["qsin"], tb["kcos"], tb["ksin"],
                        qw, qwr, kw, kwr, tb["bo64"][:256, :256], bsz=bsz)

        shift = jnp.concatenate([rwkv_shift[l], jnp.tile(jnp.array([[0.0], [1.0], [0.0]], F32), (1, 128))], -1)
        wlo = jnp.zeros((512, 2048), F32)
        for d in range(2):
            wlo = wlo.at[d * 64:(d + 1) * 64, d * 512:(d + 1) * 512].set(rwkv_w_up[l, d])
            wlo = wlo.at[256 + 128 + d * 64:256 + 128 + (d + 1) * 64, (2 + d) * 512:(3 + d) * 512].set(rwkv_a_up[l, d])
        if l > 0:
            vup = jnp.zeros((128, 512), F32).at[0:32].set(rwkv_v_up[l - 1])
            v0 = rwkv_v0[l - 1]
        else:
            vup = jnp.zeros((128, 512), F32)
            v0 = jnp.zeros((512,), F32)
        vec = jnp.stack([rwkv_k_k[l], rwkv_k_a[l], rwkv_r_k[l].reshape(512), v0,
                         rwkv_w0[l, 0], rwkv_w0[l, 1], rwkv_a0[l, 0], rwkv_a0[l, 1]])
        vf_in = vfirst if l > 0 else jnp.zeros((nt, 512), F32)
        r, v, kk, g, bonus, lw, kd, bv = _rw_prep_call(proj, shift, wlo, rwkv_g_up[l], vup, tb["bo64"], vec, vf_in,
                                                        ct=ct, mix_v=l > 0)
        if l == 0:
            vfirst = v
        y = _rw_scan_call(r, v, kk, lw, kd, bv, tb["lcum"], tb["mbig"], tb["eye"], bsz=bsz)

        gn = jnp.stack([rwkv_gn_w[l], rwkv_gn_b[l]])
        tile0 = ct // MERGE_TM if last else 0
        xs = _merge_call(xs, mod, proj, oret, att, y, bonus, g, tb["bo64"], gn,
                         w_branch_ret[l].astype(BF16), w_branch_att[l].astype(BF16),
                         w_branch_rwkv[l].astype(BF16), w_out[l].astype(BF16),
                         ct=ct, tm=MERGE_TM, tile0=tile0, n_tiles=nt // MERGE_TM - tile0)
        xs = _ffn_call(xs, mod, norm_w[l, 2], f2_in, f2_out, base=6, ct=0 if last else ct, tm=tm, tile0=0,
                       n_tiles=xs.shape[0] // tm)
    return xs.reshape(bsz, N_LAT, D)
```

```python
import functools
import math

import numpy as np
import jax
import jax.numpy as jnp
from jax import lax
from jax.experimental import pallas as pl
from jax.experimental.pallas import tpu as pltpu

F32 = jnp.float32
BF16 = jnp.bfloat16
HI = lax.Precision.HIGHEST

D = 1024
N_LAT = 2048
N_CTX = 256
GRID_W = 64
N_ADA = 9
D_FF = 2816
NORM_EPS = 1e-6
RET_CHUNK = 128
ATT_BLOCK = 128
WINDOW = 128
RW_CHUNK = 64
RWKV_GN_EPS = 64e-5
RWKV_DECAY_SCALE = 0.6065306597126334
HEAD64 = 64
LANES = 128

C_RW = 0
C_RQ = 2048
C_RV = 4096
C_RG = 5120
C_AQ = 6144
C_AK = 7168
C_AV = 7680
C_GT = 8192
N_PROJ = 11264

VMEM_LIMIT = 56 * 1024 * 1024


def _dot(a, b, prec=None):
    return jnp.dot(a, b, preferred_element_type=F32, precision=prec)


def _dot_nt(a, b, prec=None):
    return lax.dot_general(a, b, (((1,), (1,)), ((), ())), preferred_element_type=F32, precision=prec)


def _dot_tn(a, b, prec=None):
    return lax.dot_general(a, b, (((0,), (0,)), ((), ())), preferred_element_type=F32, precision=prec)


NN = (((1,), (0,)), ((), ()))
NT = (((1,), (1,)), ((), ()))


def _split(x, terms=2):
    out = []
    for _ in range(terms):
        piece = x.astype(BF16)
        out.append(piece)
        x = x - piece.astype(F32)
    return tuple(out)


def _mm(a, b, dims=NN):
    acc = None
    for i, ai in enumerate(a):
        for j, bj in enumerate(b):
            if i + j < max(len(a), len(b)):
                term = lax.dot_general(ai, bj, dims, preferred_element_type=F32)
                acc = term if acc is None else acc + term
    return acc


def _sigmoid(x):
    return 1.0 / (1.0 + jnp.exp(-x))


def _params(sem):
    return pltpu.CompilerParams(dimension_semantics=sem, vmem_limit_bytes=VMEM_LIMIT)


def _ada_kernel(c_ref, w_ref, b_ref, o_ref):
    c = c_ref[...]
    o_ref[...] = _dot(c * _sigmoid(c), w_ref[...], HI) + b_ref[...]


def _ada_call(cond, w, b):
    rows = cond.shape[0]
    tn = 1024
    n = w.shape[1]
    return pl.pallas_call(
        _ada_kernel,
        grid=(n // tn,),
        in_specs=[pl.BlockSpec((rows, D), lambda j: (0, 0)),
                  pl.BlockSpec((D, tn), lambda j: (0, j)),
                  pl.BlockSpec((1, tn), lambda j: (0, j))],
        out_specs=pl.BlockSpec((rows, tn), lambda j: (0, j)),
        out_shape=jax.ShapeDtypeStruct((rows, n), F32),
        compiler_params=_params(("arbitrary",)),
        name="ada",
    )(cond, w, b.reshape(1, n))


def _mod_row(tok0, ct):
    return jnp.where(tok0 < ct, 0, 1 + (tok0 - ct) // N_LAT)


def _norm_mod(x, nw, shift, scale):
    y = x * lax.rsqrt(jnp.mean(x * x, -1, keepdims=True) + NORM_EPS) * nw
    return y * (1.0 + scale) + shift


def _ffn_kernel(x_ref, mod_ref, nw_ref, wg_ref, wu_ref, wo_ref, o_ref, h_sc, acc_sc, *, base, n_ff):
    j = pl.program_id(1)

    @pl.when(j == 0)
    def _():
        h = _norm_mod(x_ref[...], nw_ref[...], mod_ref[0, base:base + 1, :], mod_ref[0, base + 1:base + 2, :])
        h_sc[...] = h.astype(BF16)
        acc_sc[...] = jnp.zeros_like(acc_sc)

    h = h_sc[...]
    g = _dot(h, wg_ref[...])
    u = _dot(h, wu_ref[...])
    act = g * _sigmoid(g) * u
    acc_sc[...] += _dot(act.astype(BF16), wo_ref[...])

    @pl.when(j == n_ff - 1)
    def _():
        o_ref[...] = x_ref[...] + 0.5 * mod_ref[0, base + 2:base + 3, :] * acc_sc[...]


def _ffn_call(x, mod, nw, w_in, w_out, *, base, ct, tm, tile0, n_tiles):
    tf = 256
    n_ff = D_FF // tf
    nt = x.shape[0]
    kern = functools.partial(_ffn_kernel, base=base, n_ff=n_ff)
    return pl.pallas_call(
        kern,
        grid=(n_tiles, n_ff),
        in_specs=[pl.BlockSpec((tm, D), lambda i, j: (i + tile0, 0)),
                  pl.BlockSpec((1, N_ADA, D), lambda i, j: (_mod_row((i + tile0) * tm, ct), 0, 0)),
                  pl.BlockSpec((1, D), lambda i, j: (0, 0)),
                  pl.BlockSpec((D, tf), lambda i, j: (0, j)),
                  pl.BlockSpec((D, tf), lambda i, j: (0, j + n_ff)),
                  pl.BlockSpec((tf, D), lambda i, j: (j, 0))],
        out_specs=pl.BlockSpec((tm, D), lambda i, j: (i, 0)),
        out_shape=jax.ShapeDtypeStruct((n_tiles * tm, D), F32),
        scratch_shapes=[pltpu.VMEM((tm, D), BF16), pltpu.VMEM((tm, D), F32)],
        compiler_params=_params(("parallel", "arbitrary")),
        name="ffn",
    )(x, mod, nw.reshape(1, D), w_in, w_in, w_out)


def _proj_kernel(x_ref, mod_ref, nw_ref, w_ref, o_ref, h_sc):
    @pl.when(pl.program_id(1) == 0)
    def _():
        h = _norm_mod(x_ref[...], nw_ref[...], mod_ref[0, 3:4, :], mod_ref[0, 4:5, :])
        h_sc[...] = h.astype(BF16)

    o_ref[...] = _dot(h_sc[...], w_ref[...])


def _proj_call(x, mod, nw, w, *, ct, tm):
    tn = 512
    nt = x.shape[0]
    return pl.pallas_call(
        _proj_kernel,
        grid=(nt // tm, N_PROJ // tn),
        in_specs=[pl.BlockSpec((tm, D), lambda i, j: (i, 0)),
                  pl.BlockSpec((1, N_ADA, D), lambda i, j: (_mod_row(i * tm, ct), 0, 0)),
                  pl.BlockSpec((1, D), lambda i, j: (0, 0)),
                  pl.BlockSpec((D, tn), lambda i, j: (0, j))],
        out_specs=pl.BlockSpec((tm, tn), lambda i, j: (i, j)),
        out_shape=jax.ShapeDtypeStruct((nt, N_PROJ), F32),
        scratch_shapes=[pltpu.VMEM((tm, D), BF16)],
        compiler_params=_params(("parallel", "arbitrary")),
        name="proj",
    )(x, mod, nw.reshape(1, D), w)


def _scan_blocks(b, d, s, *, nc_ctx, nc_lat, bsz):
    in_ctx = s < nc_ctx
    cs = jnp.where(d == 0, s, nc_ctx - 1 - s)
    ls = jnp.where(d == 0, s - nc_ctx, nc_lat - 1 - (s - nc_ctx))
    tok = jnp.where(in_ctx, b * nc_ctx + cs, bsz * nc_ctx + b * nc_lat + ls)
    pos = jnp.where(in_ctx, cs, nc_ctx + ls)
    return tok, pos


def _ret_kernel(lg_ref, q_ref, qr_ref, k_ref, kr_ref, v_ref, cos_ref, sin_ref, dist_ref, ze_ref, xe_ref,
                o_ref, st_sc):
    d = pl.program_id(1)
    s = pl.program_id(2)

    @pl.when(s == 0)
    def _():
        st_sc[...] = jnp.zeros_like(st_sc)

    cos = cos_ref[...]
    sin = sin_ref[...]
    q = q_ref[...] * cos + qr_ref[...] * sin
    k = (k_ref[...] * cos + kr_ref[...] * sin) * (HEAD64 ** -0.5)
    dist = dist_ref[...]
    valid = dist >= 0.0
    ze = ze_ref[...]
    xe = xe_ref[...]
    lane = lax.broadcasted_iota(jnp.int32, (1, LANES), 1)
    heads = range(8)
    lgs = [lg_ref[d, h] for h in heads]
    hmask = [((lane // HEAD64) == e).astype(F32) for e in range(2)]
    kps = [k[:, p * LANES:(p + 1) * LANES].astype(BF16) for p in range(4)]
    qms = [(q[:, (h // 2) * LANES:(h // 2 + 1) * LANES] * hmask[h % 2]).astype(BF16) for h in heads]
    vhs = [v_ref[:, h * LANES:(h + 1) * LANES] for h in heads]
    sts = [st_sc[h] for h in heads]
    scs = [(_dot_nt(qms[h], kps[h // 2]) * jnp.where(valid, jnp.exp(lgs[h] * dist), 0.0)).astype(BF16) for h in heads]
    inter = [_dot(qms[h], sts[h].astype(BF16)) * jnp.exp(lgs[h] * xe) for h in heads]
    for h in heads:
        o_ref[:, h * LANES:(h + 1) * LANES] = _dot(scs[h], vhs[h].astype(BF16)) + inter[h]
    us = [_dot_tn(kps[h // 2], (vhs[h] * jnp.exp(lgs[h] * ze)).astype(BF16)) for h in heads]
    chunk_len = jnp.full((1, LANES), float(RET_CHUNK), F32)
    for h in heads:
        st_sc[h] = sts[h] * jnp.exp(lgs[h] * chunk_len) + us[h]


def _ret_call(proj, log_g, cos, sin, dist, ze, xe, *, bsz):
    nt = proj.shape[0]
    c = RET_CHUNK
    nc_ctx, nc_lat = N_CTX // c, N_LAT // c
    steps = nc_ctx + nc_lat
    blocks = functools.partial(_scan_blocks, nc_ctx=nc_ctx, nc_lat=nc_lat, bsz=bsz)

    def tok_map(col):
        return lambda b, d, s, lg: (blocks(b, d, s)[0], col)

    pos_map = lambda b, d, s, lg: (blocks(b, d, s)[1], 0)
    dir_map = lambda b, d, s, lg: (d, 0, 0)
    grid_spec = pltpu.PrefetchScalarGridSpec(
        num_scalar_prefetch=1,
        grid=(bsz, 2, steps),
        in_specs=[pl.BlockSpec((c, 512), tok_map(C_RQ // 512)),
                  pl.BlockSpec((c, 512), tok_map(C_RQ // 512 + 1)),
                  pl.BlockSpec((c, 512), tok_map(C_RQ // 512 + 2)),
                  pl.BlockSpec((c, 512), tok_map(C_RQ // 512 + 3)),
                  pl.BlockSpec((c, 1024), tok_map(C_RV // 1024)),
                  pl.BlockSpec((c, 512), pos_map),
                  pl.BlockSpec((c, 512), pos_map),
                  pl.BlockSpec((None, c, c), dir_map),
                  pl.BlockSpec((None, c, LANES), dir_map),
                  pl.BlockSpec((None, c, LANES), dir_map)],
        out_specs=pl.BlockSpec((None, c, 1024), lambda b, d, s, lg: (d, blocks(b, d, s)[0], 0)),
        scratch_shapes=[pltpu.VMEM((8, LANES, LANES), F32)],
    )
    return pl.pallas_call(
        _ret_kernel,
        grid_spec=grid_spec,
        out_shape=jax.ShapeDtypeStruct((2, nt, 1024), F32),
        compiler_params=_params(("parallel", "parallel", "arbitrary")),
        name="retention",
    )(log_g, proj, proj, proj, proj, proj, cos, sin, dist, ze, xe)


NEG = -1e30


def _att_kernel(sink_ref, q_ref, qr_ref, qcos_ref, qsin_ref, qw_ref, qwr_ref, bo_ref,
                kl_ref, klr_ref, vl_ref, kcos_ref, ksin_ref, kc_ref, vc_ref, kw_ref, kwr_ref,
                o_ref, kn_sc, kcn_sc, *, n_ctx_blk):
    hk = pl.program_id(1)
    jb = pl.program_id(2)
    blk = ATT_BLOCK

    @pl.when(jb == 0)
    def _():
        kl = kl_ref[...]
        rs = lax.rsqrt(jnp.sum(kl * kl, -1, keepdims=True) * (0.5 / HEAD64) + NORM_EPS)
        kn_sc[...] = ((kl * kw_ref[...]) * kcos_ref[...] + (klr_ref[...] * kwr_ref[...]) * ksin_ref[...]) * rs
        kc = kc_ref[...]
        rc = lax.rsqrt(jnp.sum(kc * kc, -1, keepdims=True) * (0.5 / HEAD64) + NORM_EPS)
        kcn_sc[...] = kc * kw_ref[...] * rc

    q = q_ref[...]
    mean_sq = _mm(_split(q * q), (bo_ref[...],)) * (1.0 / HEAD64)
    rs = lax.rsqrt(mean_sq + NORM_EPS)
    qn = ((q * qw_ref[...]) * qcos_ref[...] + (qr_ref[...] * qwr_ref[...]) * qsin_ref[...]) * rs * (HEAD64 ** -0.5)

    reach = jnp.where(jb >= n_ctx_blk, WINDOW, -1)
    lb = jnp.maximum(jb - n_ctx_blk, 0)
    n_win = 3 * blk
    start = jnp.clip((lb - 1) * blk, 0, N_LAT - n_win)
    start = pl.multiple_of(start, blk)
    n_keys = n_win + N_CTX
    kall = jnp.concatenate([kn_sc[pl.ds(start, n_win), :], kcn_sc[...]], axis=0).astype(BF16)
    vall = jnp.concatenate([vl_ref[pl.ds(start, n_win), :], vc_ref[...]], axis=0).astype(BF16)
    qpos = lb * blk + lax.broadcasted_iota(jnp.int32, (blk, n_keys), 0)
    col = lax.broadcasted_iota(jnp.int32, (blk, n_keys), 1)
    valid = (col >= n_win) | (jnp.abs(start + col - qpos) <= reach)
    lane = lax.broadcasted_iota(jnp.int32, (1, LANES), 1)
    hmask = [((lane // HEAD64) == e).astype(F32) for e in range(2)]
    groups = range(4)
    sinks = [sink_ref[hk * 4 + g] for g in groups]
    qgs = [(qn[:, (g // 2) * LANES:(g // 2 + 1) * LANES] * hmask[g % 2]).astype(BF16) for g in groups]
    ss = [jnp.where(valid, _dot_nt(qgs[g], kall), NEG) for g in groups]
    ms = [jnp.maximum(jnp.max(ss[g], -1, keepdims=True), sinks[g]) for g in groups]
    ps = [jnp.exp(ss[g] - ms[g]) for g in groups]
    dens = [jnp.sum(ps[g], -1, keepdims=True) + jnp.exp(sinks[g] - ms[g]) for g in groups]
    ogs = [_dot(ps[g].astype(BF16), vall) / dens[g] * hmask[g % 2] for g in groups]
    o_ref[...] = jnp.concatenate([ogs[0] + ogs[1], ogs[2] + ogs[3]], axis=-1)


def _att_call(proj, sink, qcos, qsin, kcos, ksin, qw, qwr, kw, kwr, bo, *, bsz):
    nt = proj.shape[0]
    blk = ATT_BLOCK
    n_ctx_blk, n_lat_blk = N_CTX // blk, N_LAT // blk
    steps = n_ctx_blk + n_lat_blk

    def qtok(b, jb):
        return jnp.where(jb < n_ctx_blk, b * n_ctx_blk + jb, bsz * n_ctx_blk + b * n_lat_blk + (jb - n_ctx_blk))

    def qmap(col0):
        return lambda b, hk, jb, sk: (qtok(b, jb), col0 + hk)

    qpos_map = lambda b, hk, jb, sk: (jb, 0)
    const = lambda b, hk, jb, sk: (0, 0)
    ct_rows = bsz * N_CTX

    def lat_map(col0):
        return lambda b, hk, jb, sk: (ct_rows // N_LAT + b, col0 + hk)

    def ctx_map(col0):
        return lambda b, hk, jb, sk: (b, col0 + hk)

    grid_spec = pltpu.PrefetchScalarGridSpec(
        num_scalar_prefetch=1,
        grid=(bsz, 2, steps),
        in_specs=[pl.BlockSpec((blk, 256), qmap(C_AQ // 256)),
                  pl.BlockSpec((blk, 256), qmap(C_AQ // 256 + 2)),
                  pl.BlockSpec((blk, 256), qpos_map),
                  pl.BlockSpec((blk, 256), qpos_map),
                  pl.BlockSpec((1, 256), const),
                  pl.BlockSpec((1, 256), const),
                  pl.BlockSpec((256, 256), const),
                  pl.BlockSpec((N_LAT, LANES), lat_map(C_AK // LANES)),
                  pl.BlockSpec((N_LAT, LANES), lat_map(C_AK // LANES + 2)),
                  pl.BlockSpec((N_LAT, LANES), lat_map(C_AV // LANES)),
                  pl.BlockSpec((N_LAT, LANES), const),
                  pl.BlockSpec((N_LAT, LANES), const),
                  pl.BlockSpec((N_CTX, LANES), ctx_map(C_AK // LANES)),
                  pl.BlockSpec((N_CTX, LANES), ctx_map(C_AV // LANES)),
                  pl.BlockSpec((1, LANES), const),
                  pl.BlockSpec((1, LANES), const)],
        out_specs=pl.BlockSpec((blk, 256), lambda b, hk, jb, sk: (qtok(b, jb), hk)),
        scratch_shapes=[pltpu.VMEM((N_LAT, LANES), F32), pltpu.VMEM((N_CTX, LANES), F32)],
    )
    kern = functools.partial(_att_kernel, n_ctx_blk=n_ctx_blk)
    return pl.pallas_call(
        kern,
        grid_spec=grid_spec,
        out_shape=jax.ShapeDtypeStruct((nt, 512), F32),
        compiler_params=_params(("parallel", "parallel", "arbitrary")),
        name="attention",
    )(sink, proj, proj, qcos, qsin, qw, qwr, bo, proj, proj, proj, kcos, ksin, proj, proj, kw, kwr)


RW_TM = 256


def _rw_prep_kernel(cur_ref, prev_ref, next_ref, shift_ref, wlo_ref, gup_ref, vup_ref, bo_ref, vec_ref, vf_ref,
                    r_o, v_o, kk_o, g_o, bonus_o, lw_o, kd_o, bv_o, *, ct, mix_v):
    i = pl.program_id(0)
    tm = RW_TM
    tok0 = i * tm
    lat_off = tok0 - ct
    is_start = jnp.where(tok0 < ct, tok0 % N_CTX == 0, lat_off % N_LAT == 0)
    is_end = jnp.where(tok0 < ct, (tok0 + tm) % N_CTX == 0, (lat_off + tm) % N_LAT == 0)
    cur = cur_ref[...]
    row = lax.broadcasted_iota(jnp.int32, (tm, 1), 0)
    prev_row = jnp.where(is_start, 0.0, prev_ref[7:8, :])
    next_row = jnp.where(is_end, 0.0, next_ref[0:1, :])
    prev = jnp.where(row == 0, prev_row, pltpu.roll(cur, 1, 0))
    nxt = jnp.where(row == tm - 1, next_row, pltpu.roll(cur, tm - 1, 0))
    c = prev * shift_ref[0:1, :] + cur * shift_ref[1:2, :] + nxt * shift_ref[2:3, :]

    r = c[:, 0:512]
    k = c[:, 512:1024]
    v = c[:, 1024:1536]
    lo = c[:, 1536:1792]
    gd = c[:, 1792:1920]
    vl = c[:, 1920:2048]

    vec = vec_ref[...]
    k_k, k_a, r_k, v0 = vec[0:1], vec[1:2], vec[2:3], vec[3:4]
    bo = (bo_ref[...],)
    if mix_v:
        v = v + (vf_ref[...] - v) * _sigmoid(v0 + _mm(_split(vl), (vup_ref[0], vup_ref[1])))
    g = _mm(_split(_sigmoid(gd)), (gup_ref[0], gup_ref[1]))
    kk = k * k_k
    ss = _mm(_split(kk * kk), bo)
    kk = kk / jnp.maximum(jnp.sqrt(ss), 1e-12)
    lora = _mm(_split(jnp.concatenate([jnp.tanh(lo), lo], axis=-1)),
               (wlo_ref[0], wlo_ref[1]))
    ksum = jnp.zeros_like(k)
    for d in range(2):
        z = vec[4 + d:5 + d] + lora[:, d * 512:(d + 1) * 512]
        a = _sigmoid(vec[6 + d:7 + d] + lora[:, (2 + d) * 512:(3 + d) * 512])
        kd = k * (1.0 + (a - 1.0) * k_a)
        lw_o[d] = -RWKV_DECAY_SCALE * _sigmoid(z)
        kd_o[d] = kd
        bv_o[d] = a * kk
        ksum = ksum + kd
    bonus_o[...] = _mm(_split(r * ksum * r_k), bo) * v
    r_o[...] = r
    v_o[...] = v
    kk_o[...] = kk
    g_o[...] = g


def _rw_prep_call(proj, shift, wlo, gup, vup, bo, vec, vfirst, *, ct, mix_v):
    nt = proj.shape[0]
    tm = RW_TM
    nblk8 = nt // 8
    tok = pl.BlockSpec((tm, 512), lambda i: (i, 0))
    dirs = pl.BlockSpec((2, tm, 512), lambda i: (0, i, 0))
    const = lambda i: (0, 0)
    kern = functools.partial(_rw_prep_kernel, ct=ct, mix_v=mix_v)
    one = jax.ShapeDtypeStruct((nt, 512), F32)
    two = jax.ShapeDtypeStruct((2, nt, 512), F32)
    return pl.pallas_call(
        kern,
        grid=(nt // tm,),
        in_specs=[pl.BlockSpec((tm, 2048), lambda i: (i, C_RW // 2048)),
                  pl.BlockSpec((8, 2048), lambda i: (jnp.maximum(i * (tm // 8) - 1, 0), C_RW // 2048)),
                  pl.BlockSpec((8, 2048), lambda i: (jnp.minimum((i + 1) * (tm // 8), nblk8 - 1), C_RW // 2048)),
                  pl.BlockSpec((3, 2048), const),
                  pl.BlockSpec((2, 512, 2048), lambda i: (0, 0, 0)),
                  pl.BlockSpec((2, 128, 512), lambda i: (0, 0, 0)),
                  pl.BlockSpec((2, 128, 512), lambda i: (0, 0, 0)),
                  pl.BlockSpec((512, 512), const),
                  pl.BlockSpec((8, 512), const),
                  tok],
        out_specs=[tok, tok, tok, tok, tok, dirs, dirs, dirs],
        out_shape=[one, one, one, one, one, two, two, two],
        compiler_params=_params(("parallel",)),
        name="rwkv_prep",
    )(proj, proj, proj, shift, wlo, gup, vup, bo, vec, vfirst)


def _rw_scan_kernel(r_ref, v_ref, kk_ref, lw_ref, kd_ref, bv_ref, lcum_ref, mbig_ref, eye_ref, y_ref, st_sc):
    s = pl.program_id(2)
    t = RW_CHUNK

    @pl.when(s == 0)
    def _():
        st_sc[...] = jnp.zeros_like(st_sc)

    lw = lw_ref[...]
    cum = _mm((lcum_ref[...].astype(BF16),), _split(lw, 3))
    tot = jnp.sum(lw, axis=0, keepdims=True)
    e_incl = jnp.exp(cum)
    e_excl = jnp.exp(cum - lw)
    e_inv = jnp.exp(-cum)
    e_rem = jnp.exp(tot - cum)
    w_tot = jnp.exp(tot)
    kk = kk_ref[...]
    kd = kd_ref[...]
    bv = bv_ref[...]
    aw = -kk * e_excl
    rt = r_ref[...] * e_incl
    bi = bv * e_inv
    ki = kd * e_inv
    bp = bv * e_rem
    kp = kd * e_rem
    v = v_ref[...]
    mbig = mbig_ref[...]
    eye = eye_ref[...]
    lane = lax.broadcasted_iota(jnp.int32, (1, LANES), 1)
    m_e = (lane < HEAD64).astype(F32)
    m_o = 1.0 - m_e

    def stack2(x):
        return jnp.concatenate([x * m_e, x * m_o], axis=0)

    pairs = range(4)
    sls = [slice(p * LANES, (p + 1) * LANES) for p in pairs]
    aw2 = [stack2(aw[:, sl]) for sl in sls]
    rt2 = [stack2(rt[:, sl]) for sl in sls]
    v2s = [_split(stack2(v[:, sl])) for sl in sls]
    big = [_mm(_split(jnp.concatenate([aw2[p], rt2[p]], axis=0)),
               _split(jnp.concatenate([stack2(bi[:, sls[p]]), stack2(ki[:, sls[p]])], axis=0)), NT) * mbig
           for p in pairs]
    a_ab = [b[0:2 * t, 0:2 * t] for b in big]
    a_ak = [b[0:2 * t, 2 * t:4 * t] for b in big]
    a_rb = [b[2 * t:4 * t, 0:2 * t] for b in big]
    a_rk = [b[2 * t:4 * t, 2 * t:4 * t] for b in big]
    akv = [_mm(_split(a_ak[p]), v2s[p]) for p in pairs]
    inv = [eye + a for a in a_ab]
    pws = [_split(a) for a in a_ab]
    for _ in range(int(math.log2(t)) - 1):
        pws = [_split(_mm(w, w)) for w in pws]
        inv = [inv[p] + _mm(_split(inv[p]), pws[p]) for p in pairs]
    pqs = [_split(_mm(_split(inv[p]), _split(jnp.concatenate([aw2[p], akv[p]], axis=1)))) for p in pairs]
    tmp = [_mm(_split(a_rb[p]), pqs[p]) for p in pairs]
    ark_v = [_mm(_split(a_rk[p]), v2s[p]) for p in pairs]
    gu = [_mm(_split(stack2(bp[:, sls[p]]).T), pqs[p]) for p in pairs]
    kpv = [_mm(_split(stack2(kp[:, sls[p]]).T), v2s[p]) for p in pairs]
    sts = [_split(st_sc[p]) for p in pairs]
    y2 = [_mm(_split(rt2[p] + tmp[p][:, 0:LANES]), sts[p]) + (tmp[p][:, LANES:2 * LANES] + ark_v[p]) for p in pairs]
    st_new = [_mm(_split(gu[p][:, 0:LANES] + eye * w_tot[:, sls[p]]), sts[p]) + (gu[p][:, LANES:2 * LANES] + kpv[p])
              for p in pairs]
    for p in pairs:
        y_ref[:, sls[p]] = y2[p][0:t] + y2[p][t:2 * t]
        st_sc[p] = st_new[p]


def _rw_scan_call(r, v, kk, lw, kd, bv, lcum, mbig, eye, *, bsz):
    nt = r.shape[0]
    t = RW_CHUNK
    nc_ctx, nc_lat = N_CTX // t, N_LAT // t
    steps = nc_ctx + nc_lat
    blocks = functools.partial(_scan_blocks, nc_ctx=nc_ctx, nc_lat=nc_lat, bsz=bsz)
    tok = pl.BlockSpec((t, 512), lambda b, d, s: (blocks(b, d, s)[0], 0))
    tokd = pl.BlockSpec((None, t, 512), lambda b, d, s: (d, blocks(b, d, s)[0], 0))
    return pl.pallas_call(
        _rw_scan_kernel,
        grid=(bsz, 2, steps),
        in_specs=[tok, tok, tok, tokd, tokd, tokd,
                  pl.BlockSpec((None, t, t), lambda b, d, s: (d, 0, 0)),
                  pl.BlockSpec((None, 4 * t, 4 * t), lambda b, d, s: (d, 0, 0)),
                  pl.BlockSpec((LANES, LANES), lambda b, d, s: (0, 0))],
        out_specs=tokd,
        out_shape=jax.ShapeDtypeStruct((2, nt, 512), F32),
        scratch_shapes=[pltpu.VMEM((4, LANES, LANES), F32)],
        compiler_params=_params(("parallel", "parallel", "arbitrary")),
        name="rwkv_scan",
    )(r, v, kk, lw, kd, bv, lcum, mbig, eye)


MERGE_TM = 256


def _merge_kernel(x_ref, mod_ref, oret_ref, rg_ref, att_ref, y_ref, bonus_ref, g_ref, gt0_ref, gt1_ref, gt2_ref,
                  bo_ref, gn_ref, wr_ref, wa_ref, ww_ref, wo_ref, o_ref):
    o = oret_ref[0] + oret_ref[1]
    parts = []
    for h in range(8):
        oh = o[:, h * LANES:(h + 1) * LANES]
        mu = jnp.mean(oh, -1, keepdims=True)
        dv = oh - mu
        var = jnp.mean(dv * dv, -1, keepdims=True)
        parts.append(dv * lax.rsqrt(var + NORM_EPS))
    rg = rg_ref[...]
    ret = (rg * _sigmoid(rg)) * jnp.concatenate(parts, axis=-1)

    y = y_ref[0] + y_ref[1]
    bo = (bo_ref[...],)
    mu = _mm(_split(y), bo) * (1.0 / HEAD64)
    dy = y - mu
    var = _mm(_split(dy * dy), bo) * (1.0 / HEAD64)
    yn = dy * lax.rsqrt(var + RWKV_GN_EPS) * gn_ref[0:1, :] + gn_ref[1:2, :]
    rw = (yn + bonus_ref[...]) * g_ref[...]

    merged = (_sigmoid(gt0_ref[...]) * _dot(ret.astype(BF16), wr_ref[...])
              + _sigmoid(gt1_ref[...]) * _dot(att_ref[...].astype(BF16), wa_ref[...])
              + _sigmoid(gt2_ref[...]) * _dot(rw.astype(BF16), ww_ref[...]))
    out = _dot(merged.astype(BF16), wo_ref[...])
    o_ref[...] = x_ref[...] + mod_ref[0, 5:6, :] * out


def _merge_call(x, mod, proj, oret, att, y, bonus, g, bo, gn, wr, wa, ww, wo, *, ct, tm, tile0, n_tiles):
    tok = lambda w, col=0: pl.BlockSpec((tm, w), lambda i: (i + tile0, col))
    tok2 = lambda w: pl.BlockSpec((2, tm, w), lambda i: (0, i + tile0, 0))
    const = lambda i: (0, 0)
    return pl.pallas_call(
        _merge_kernel,
        grid=(n_tiles,),
        in_specs=[tok(D),
                  pl.BlockSpec((1, N_ADA, D), lambda i: (_mod_row((i + tile0) * tm, ct), 0, 0)),
                  tok2(1024), tok(1024, C_RG // 1024), tok(512), tok2(512), tok(512), tok(512),
                  tok(D, C_GT // D), tok(D, C_GT // D + 1), tok(D, C_GT // D + 2),
                  pl.BlockSpec((512, 512), const),
                  pl.BlockSpec((2, 512), const),
                  pl.BlockSpec((1024, D), const),
                  pl.BlockSpec((512, D), const),
                  pl.BlockSpec((512, D), const),
                  pl.BlockSpec((D, D), const)],
        out_specs=pl.BlockSpec((tm, D), lambda i: (i, 0)),
        out_shape=jax.ShapeDtypeStruct((n_tiles * tm, D), F32),
        compiler_params=_params(("parallel",)),
        name="merge",
    )(x, mod, oret, proj, att, y, bonus, g, proj, proj, proj, bo, gn, wr, wa, ww, wo)


def _rope_1d(pos, dim, base):
    n_freq = dim // 2
    inv = np.power(np.float32(base), -(np.arange(n_freq, dtype=np.float32) / np.float32(n_freq))).astype(np.float32)
    return pos.astype(np.float32)[:, None] * inv[None, :]


def _tables():
    ang = _rope_1d(np.arange(N_CTX + N_LAT), HEAD64, 10000.0)
    rcos = np.tile(np.concatenate([np.cos(ang), np.cos(ang)], -1), (1, 8)).astype(np.float32)
    rsin = np.tile(np.concatenate([np.sin(ang), np.sin(ang)], -1), (1, 8)).astype(np.float32)
    rows = N_LAT // GRID_W
    row = np.repeat(np.arange(rows), GRID_W)
    col = np.arange(rows * GRID_W) % GRID_W
    aang = np.concatenate([_rope_1d(row, HEAD64 // 2, 10000.0), _rope_1d(col, HEAD64 // 2, 10000.0)], -1)
    ac = np.concatenate([np.cos(aang), np.cos(aang)], -1).astype(np.float32)
    asn = np.concatenate([np.sin(aang), np.sin(aang)], -1).astype(np.float32)
    qcos = np.concatenate([np.ones((N_CTX, 256), np.float32), np.tile(ac, (1, 4))], 0)
    qsin = np.concatenate([np.zeros((N_CTX, 256), np.float32), np.tile(asn, (1, 4))], 0)
    kcos = np.tile(ac, (1, 2))
    ksin = np.tile(asn, (1, 2))
    c = RET_CHUNK
    pos = np.arange(c, dtype=np.float32)
    diff = pos[:, None] - pos[None, :]
    dist = np.stack([np.where(diff >= 0, diff, -1.0), np.where(diff <= 0, -diff, -1.0)]).astype(np.float32)
    ze = np.stack([c - 1.0 - pos, pos]).astype(np.float32)
    xe = np.stack([pos + 1.0, c - pos]).astype(np.float32)
    ze = np.broadcast_to(ze[:, :, None], (2, c, LANES)).copy()
    xe = np.broadcast_to(xe[:, :, None], (2, c, LANES)).copy()
    t = RW_CHUNK
    ti = np.arange(t)
    low_incl = (ti[None, :] <= ti[:, None]).astype(np.float32)
    low_strict = (ti[None, :] < ti[:, None]).astype(np.float32)
    lcum = np.stack([low_incl, low_incl.T])
    eye2 = np.eye(2, dtype=np.float32)
    mbig = []
    for strict, incl in ((low_strict, low_incl), (low_strict.T, low_incl.T)):
        s2 = np.kron(eye2, strict)
        i2 = np.kron(eye2, incl)
        mbig.append(np.block([[s2, s2], [i2, i2]]))
    mbig = np.stack(mbig).astype(np.float32)
    eye = np.eye(LANES, dtype=np.float32)
    bo64 = np.kron(np.eye(8, dtype=np.float32), np.ones((HEAD64, HEAD64), np.float32))
    return dict(rcos=rcos, rsin=rsin, qcos=qcos, qsin=qsin, kcos=kcos, ksin=ksin, dist=dist, ze=ze, xe=xe,
                lcum=lcum, mbig=mbig, eye=eye, bo64=bo64)


def _rot_cols(n_heads):
    half = HEAD64 // 2
    idx, sgn = [], []
    for h in range(n_heads):
        base = h * HEAD64
        idx += list(range(base + half, base + HEAD64)) + list(range(base, base + half))
        sgn += [-1.0] * half + [1.0] * half
    return np.array(idx), np.array(sgn, np.float32)


def _proj_weight(w_in, v_down, dtype=BF16):
    o = np.cumsum([0, 512, 512, 1024, 1024, 512, 128, 128, 1920, 3072])
    rq, rk, rv, rg, aq, ak, av, rw, gt = (w_in[:, o[i]:o[i + 1]] for i in range(9))
    i8, s8 = _rot_cols(8)
    i2, s2 = _rot_cols(2)
    dup = lambda m: jnp.concatenate([m[:, 0:64], m[:, 0:64], m[:, 64:128], m[:, 64:128]], -1)
    z = lambda n: jnp.zeros((D, n), w_in.dtype)
    vd = z(32) if v_down is None else v_down
    cols = [rw, vd, z(96),
            rq, rq[:, i8] * s8, rk, rk[:, i8] * s8,
            rv, rg,
            aq, aq[:, i8] * s8,
            dup(ak), dup(ak[:, i2] * s2),
            dup(av), z(256),
            gt]
    w = jnp.concatenate(cols, -1)
    assert w.shape == (D, N_PROJ), w.shape
    return w.astype(dtype)


def kernel(x, c, ctx, c_ctx, ada_w, ada_b, norm_w, ffn1_w_in, ffn1_w_out, ffn2_w_in, ffn2_w_out, mix_w_in, ret_decay_logit, att_q_norm, att_k_norm, att_sink, rwkv_shift, rwkv_w0, rwkv_w_up, rwkv_a0, rwkv_a_up, rwkv_g_up, rwkv_k_k, rwkv_k_a, rwkv_r_k, rwkv_v0, rwkv_v_down, rwkv_v_up, rwkv_gn_w, rwkv_gn_b, w_branch_ret, w_branch_att, w_branch_rwkv, w_out):
    bsz = x.shape[0]
    depth = ada_w.shape[0]
    ct = bsz * N_CTX
    nt = ct + bsz * N_LAT
    tm = math.gcd(1024, ct)
    tb = {k: jnp.asarray(v) for k, v in _tables().items()}
    bo = tb["bo64"].astype(BF16)
    stack_split = lambda w: jnp.stack(_split(w))

    xs = jnp.concatenate([ctx.reshape(ct, D), x.reshape(bsz * N_LAT, D)], 0)
    rows = 8 * ((bsz + 1 + 7) // 8)
    cond = jnp.zeros((rows, D), F32).at[0].set(c_ctx).at[1:bsz + 1].set(c)
    i8, _ = _rot_cols(8)
    i2, _ = _rot_cols(2)
    vfirst = None
    for l in range(depth):
        last = l == depth - 1
        mod = _ada_call(cond, ada_w[l], ada_b[l]).reshape(rows, N_ADA, D)
        f1_in, f1_out = ffn1_w_in[l].astype(BF16), ffn1_w_out[l].astype(BF16)
        f2_in, f2_out = ffn2_w_in[l].astype(BF16), ffn2_w_out[l].astype(BF16)
        xs = _ffn_call(xs, mod, norm_w[l, 0], f1_in, f1_out, base=0, ct=ct, tm=tm, tile0=0, n_tiles=nt // tm)

        proj = _proj_call(xs, mod, norm_w[l, 1], _proj_weight(mix_w_in[l], rwkv_v_down[l - 1] if l > 0 else None),
                          ct=ct, tm=tm)

        log_g = jax.nn.log_sigmoid(ret_decay_logit[l].astype(F32))
        oret = _ret_call(proj, log_g, tb["rcos"], tb["rsin"], tb["dist"], tb["ze"], tb["xe"], bsz=bsz)

        qw = jnp.tile(att_q_norm[l], 4).reshape(1, 256)
        qwr = jnp.tile(att_q_norm[l][i2[:64]], 4).reshape(1, 256)
        kw = jnp.tile(att_k_norm[l], 2).reshape(1, 128)
        kwr = jnp.tile(att_k_norm[l][i2[:64]], 2).reshape(1, 128)
        att = _att_call(proj, att_sink[l].astype(F32), tb["qcos"], tb["qsin"], tb["kcos"], tb["ksin"],
                        qw, qwr, kw, kwr, bo[:256, :256], bsz=bsz)

        shift = jnp.concatenate([rwkv_shift[l], jnp.tile(jnp.array([[0.0], [1.0], [0.0]], F32), (1, 128))], -1)
        wlo = jnp.zeros((512, 2048), F32)
        for d in range(2):
            wlo = wlo.at[d * 64:(d + 1) * 64, d * 512:(d + 1) * 512].set(rwkv_w_up[l, d])
            wlo = wlo.at[256 + 128 + d * 64:256 + 128 + (d + 1) * 64, (2 + d) * 512:(3 + d) * 512].set(rwkv_a_up[l, d])
        if l > 0:
            vup = jnp.zeros((128, 512), F32).at[0:32].set(rwkv_v_up[l - 1])
            v0 = rwkv_v0[l - 1]
        else:
            vup = jnp.zeros((128, 512), F32)
            v0 = jnp.zeros((512,), F32)
        vec = jnp.stack([rwkv_k_k[l], rwkv_k_a[l], rwkv_r_k[l].reshape(512), v0,
                         rwkv_w0[l, 0], rwkv_w0[l, 1], rwkv_a0[l, 0], rwkv_a0[l, 1]])
        vf_in = vfirst if l > 0 else jnp.zeros((nt, 512), F32)
        r, v, kk, g, bonus, lw, kd, bv = _rw_prep_call(proj, shift, stack_split(wlo), stack_split(rwkv_g_up[l]),
                                                        stack_split(vup), bo, vec, vf_in, ct=ct, mix_v=l > 0)
        if l == 0:
            vfirst = v
        y = _rw_scan_call(r, v, kk, lw, kd, bv, tb["lcum"], tb["mbig"], tb["eye"], bsz=bsz)

        gn = jnp.stack([rwkv_gn_w[l], rwkv_gn_b[l]])
        tile0 = ct // MERGE_TM if last else 0
        xs = _merge_call(xs, mod, proj, oret, att, y, bonus, g, bo, gn,
                         w_branch_ret[l].astype(BF16), w_branch_att[l].astype(BF16),
                         w_branch_rwkv[l].astype(BF16), w_out[l].astype(BF16),
                         ct=ct, tm=MERGE_TM, tile0=tile0, n_tiles=nt // MERGE_TM - tile0)
        xs = _ffn_call(xs, mod, norm_w[l, 2], f2_in, f2_out, base=6, ct=0 if last else ct, tm=tm, tile0=0,
                       n_tiles=xs.shape[0] // tm)
    return xs.reshape(bsz, N_LAT, D)
```

```python
import functools
import math

import numpy as np
import jax
import jax.numpy as jnp
from jax import lax
from jax.experimental import pallas as pl
from jax.experimental.pallas import tpu as pltpu

F32 = jnp.float32
BF16 = jnp.bfloat16
HI = lax.Precision.HIGHEST

D = 1024
N_LAT = 2048
N_CTX = 256
GRID_W = 64
N_ADA = 9
D_FF = 2816
NORM_EPS = 1e-6
RET_CHUNK = 128
ATT_BLOCK = 128
WINDOW = 128
RW_CHUNK = 64
RWKV_GN_EPS = 64e-5
RWKV_DECAY_SCALE = 0.6065306597126334
HEAD64 = 64
LANES = 128

N_RW = 2048
C_RQ = 0
C_RV = 2048
C_RG = 3072
C_AQ = 4096
C_AK = 5120
C_AV = 5632
C_GT = 6144
N_REST = 9216

VMEM_LIMIT = 56 * 1024 * 1024


def _dot(a, b, prec=None):
    return jnp.dot(a, b, preferred_element_type=F32, precision=prec)


def _dot_nt(a, b, prec=None):
    return lax.dot_general(a, b, (((1,), (1,)), ((), ())), preferred_element_type=F32, precision=prec)


def _dot_tn(a, b, prec=None):
    return lax.dot_general(a, b, (((0,), (0,)), ((), ())), preferred_element_type=F32, precision=prec)


NN = (((1,), (0,)), ((), ()))
NT = (((1,), (1,)), ((), ()))


def _split(x, terms=2):
    out = []
    for _ in range(terms):
        piece = x.astype(BF16)
        out.append(piece)
        x = x - piece.astype(F32)
    return tuple(out)


def _mm(a, b, dims=NN):
    acc = None
    for i, ai in enumerate(a):
        for j, bj in enumerate(b):
            if i + j < max(len(a), len(b)):
                term = lax.dot_general(ai, bj, dims, preferred_element_type=F32)
                acc = term if acc is None else acc + term
    return acc


def _sigmoid(x):
    return 0.5 * jnp.tanh(0.5 * x) + 0.5


def _params(sem):
    return pltpu.CompilerParams(dimension_semantics=sem, vmem_limit_bytes=VMEM_LIMIT)


def _ada_kernel(c_ref, w_ref, b_ref, o_ref):
    c = c_ref[...]
    o_ref[...] = _dot(c * _sigmoid(c), w_ref[...], HI) + b_ref[...]


def _ada_call(cond, w, b):
    rows = cond.shape[0]
    tn = 1024
    n = w.shape[1]
    return pl.pallas_call(
        _ada_kernel,
        grid=(n // tn,),
        in_specs=[pl.BlockSpec((rows, D), lambda j: (0, 0)),
                  pl.BlockSpec((D, tn), lambda j: (0, j)),
                  pl.BlockSpec((1, tn), lambda j: (0, j))],
        out_specs=pl.BlockSpec((rows, tn), lambda j: (0, j)),
        out_shape=jax.ShapeDtypeStruct((rows, n), F32),
        compiler_params=_params(("arbitrary",)),
        name="ada",
    )(cond, w, b.reshape(1, n))


def _mod_row(tok0, ct):
    return jnp.where(tok0 < ct, 0, 1 + (tok0 - ct) // N_LAT)


def _norm_mod(x, nw, shift, scale):
    y = x * lax.rsqrt(jnp.mean(x * x, -1, keepdims=True) + NORM_EPS) * nw
    return y * (1.0 + scale) + shift


def _ffn_kernel(x_ref, mod_ref, nw_ref, wg_ref, wu_ref, wo_ref, o_ref, h_sc, acc_sc, *, base, n_ff):
    j = pl.program_id(1)

    @pl.when(j == 0)
    def _():
        h = _norm_mod(x_ref[...], nw_ref[...], mod_ref[0, base:base + 1, :], mod_ref[0, base + 1:base + 2, :])
        h_sc[...] = h.astype(BF16)
        acc_sc[...] = jnp.zeros_like(acc_sc)

    h = h_sc[...]
    g = _dot(h, wg_ref[...])
    u = _dot(h, wu_ref[...])
    act = g * _sigmoid(g) * u
    acc_sc[...] += _dot(act.astype(BF16), wo_ref[...])

    @pl.when(j == n_ff - 1)
    def _():
        o_ref[...] = x_ref[...] + 0.5 * mod_ref[0, base + 2:base + 3, :] * acc_sc[...]


def _ffn_call(x, mod, nw, w_in, w_out, *, base, ct, tm, tile0, n_tiles):
    tf = 256
    n_ff = D_FF // tf
    nt = x.shape[0]
    kern = functools.partial(_ffn_kernel, base=base, n_ff=n_ff)
    return pl.pallas_call(
        kern,
        grid=(n_tiles, n_ff),
        in_specs=[pl.BlockSpec((tm, D), lambda i, j: (i + tile0, 0)),
                  pl.BlockSpec((1, N_ADA, D), lambda i, j: (_mod_row((i + tile0) * tm, ct), 0, 0)),
                  pl.BlockSpec((1, D), lambda i, j: (0, 0)),
                  pl.BlockSpec((D, tf), lambda i, j: (0, j)),
                  pl.BlockSpec((D, tf), lambda i, j: (0, j + n_ff)),
                  pl.BlockSpec((tf, D), lambda i, j: (j, 0))],
        out_specs=pl.BlockSpec((tm, D), lambda i, j: (i, 0)),
        out_shape=jax.ShapeDtypeStruct((n_tiles * tm, D), F32),
        scratch_shapes=[pltpu.VMEM((tm, D), BF16), pltpu.VMEM((tm, D), F32)],
        compiler_params=_params(("parallel", "arbitrary")),
        name="ffn",
    )(x, mod, nw.reshape(1, D), w_in, w_in, w_out)


def _proj_kernel(x_ref, mod_ref, nw_ref, w_ref, o_ref, h_sc):
    @pl.when(pl.program_id(1) == 0)
    def _():
        h = _norm_mod(x_ref[...], nw_ref[...], mod_ref[0, 3:4, :], mod_ref[0, 4:5, :])
        h_sc[...] = h.astype(BF16)

    o_ref[...] = _dot(h_sc[...], w_ref[...]).astype(o_ref.dtype)


def _proj_call(x, mod, nw, w, *, ct, tm, out_dtype):
    tn = 512
    nt = x.shape[0]
    n_cols = w.shape[1]
    return pl.pallas_call(
        _proj_kernel,
        grid=(nt // tm, n_cols // tn),
        in_specs=[pl.BlockSpec((tm, D), lambda i, j: (i, 0)),
                  pl.BlockSpec((1, N_ADA, D), lambda i, j: (_mod_row(i * tm, ct), 0, 0)),
                  pl.BlockSpec((1, D), lambda i, j: (0, 0)),
                  pl.BlockSpec((D, tn), lambda i, j: (0, j))],
        out_specs=pl.BlockSpec((tm, tn), lambda i, j: (i, j)),
        out_shape=jax.ShapeDtypeStruct((nt, n_cols), out_dtype),
        scratch_shapes=[pltpu.VMEM((tm, D), BF16)],
        compiler_params=_params(("parallel", "arbitrary")),
        name="proj",
    )(x, mod, nw.reshape(1, D), w)


def _scan_blocks(b, d, s, *, nc_ctx, nc_lat, bsz):
    in_ctx = s < nc_ctx
    cs = jnp.where(d == 0, s, nc_ctx - 1 - s)
    ls = jnp.where(d == 0, s - nc_ctx, nc_lat - 1 - (s - nc_ctx))
    tok = jnp.where(in_ctx, b * nc_ctx + cs, bsz * nc_ctx + b * nc_lat + ls)
    pos = jnp.where(in_ctx, cs, nc_ctx + ls)
    return tok, pos


def _ret_kernel(lg_ref, q_ref, qr_ref, k_ref, kr_ref, v_ref, cos_ref, sin_ref, dist_ref, ze_ref, xe_ref,
                o_ref, st_sc, dec_sc, xi_sc, zeta_sc):
    d = pl.program_id(1)
    s = pl.program_id(2)

    heads = range(8)

    @pl.when(s == 0)
    def _():
        st_sc[...] = jnp.zeros_like(st_sc)
        dist = dist_ref[...]
        for h in heads:
            lg = lg_ref[d, h]
            dec_sc[h] = jnp.where(dist >= 0.0, jnp.exp(lg * dist), 0.0)
            xi_sc[h] = jnp.exp(lg * xe_ref[...])
            zeta_sc[h] = jnp.exp(lg * ze_ref[...])

    cos = cos_ref[...]
    sin = sin_ref[...]
    q = q_ref[...] * cos + qr_ref[...] * sin
    k = (k_ref[...] * cos + kr_ref[...] * sin) * (HEAD64 ** -0.5)
    lane = lax.broadcasted_iota(jnp.int32, (1, LANES), 1)
    hmask = [((lane // HEAD64) == e).astype(F32) for e in range(2)]
    kps = [k[:, p * LANES:(p + 1) * LANES].astype(BF16) for p in range(4)]
    qms = [(q[:, (h // 2) * LANES:(h // 2 + 1) * LANES] * hmask[h % 2]).astype(BF16) for h in heads]
    vhs = [v_ref[:, h * LANES:(h + 1) * LANES] for h in heads]
    sts = [st_sc[h] for h in heads]
    scs = [(_dot_nt(qms[h], kps[h // 2]) * dec_sc[h]).astype(BF16) for h in heads]
    inter = [_dot(qms[h], sts[h].astype(BF16)) * xi_sc[h] for h in heads]
    for h in heads:
        o_ref[:, h * LANES:(h + 1) * LANES] = _dot(scs[h], vhs[h].astype(BF16)) + inter[h]
    us = [_dot_tn(kps[h // 2], (vhs[h] * zeta_sc[h]).astype(BF16)) for h in heads]
    chunk_len = jnp.full((1, LANES), float(RET_CHUNK), F32)
    for h in heads:
        st_sc[h] = sts[h] * jnp.exp(lg_ref[d, h] * chunk_len) + us[h]


def _ret_call(proj, log_g, cos, sin, dist, ze, xe, *, bsz):
    nt = proj.shape[0]
    c = RET_CHUNK
    nc_ctx, nc_lat = N_CTX // c, N_LAT // c
    steps = nc_ctx + nc_lat
    blocks = functools.partial(_scan_blocks, nc_ctx=nc_ctx, nc_lat=nc_lat, bsz=bsz)

    def tok_map(col):
        return lambda b, d, s, lg: (blocks(b, d, s)[0], col)

    pos_map = lambda b, d, s, lg: (blocks(b, d, s)[1], 0)
    dir_map = lambda b, d, s, lg: (d, 0, 0)
    grid_spec = pltpu.PrefetchScalarGridSpec(
        num_scalar_prefetch=1,
        grid=(bsz, 2, steps),
        in_specs=[pl.BlockSpec((c, 512), tok_map(C_RQ // 512)),
                  pl.BlockSpec((c, 512), tok_map(C_RQ // 512 + 1)),
                  pl.BlockSpec((c, 512), tok_map(C_RQ // 512 + 2)),
                  pl.BlockSpec((c, 512), tok_map(C_RQ // 512 + 3)),
                  pl.BlockSpec((c, 1024), tok_map(C_RV // 1024)),
                  pl.BlockSpec((c, 512), pos_map),
                  pl.BlockSpec((c, 512), pos_map),
                  pl.BlockSpec((None, c, c), dir_map),
                  pl.BlockSpec((None, c, LANES), dir_map),
                  pl.BlockSpec((None, c, LANES), dir_map)],
        out_specs=pl.BlockSpec((None, c, 1024), lambda b, d, s, lg: (d, blocks(b, d, s)[0], 0)),
        scratch_shapes=[pltpu.VMEM((8, LANES, LANES), F32), pltpu.VMEM((8, c, c), F32),
                        pltpu.VMEM((8, c, LANES), F32), pltpu.VMEM((8, c, LANES), F32)],
    )
    return pl.pallas_call(
        _ret_kernel,
        grid_spec=grid_spec,
        out_shape=jax.ShapeDtypeStruct((2, nt, 1024), F32),
        compiler_params=_params(("parallel", "parallel", "arbitrary")),
        name="retention",
    )(log_g, proj, proj, proj, proj, proj, cos, sin, dist, ze, xe)


NEG = -1e30


def _att_kernel(sink_ref, q_ref, qr_ref, qcos_ref, qsin_ref, qw_ref, qwr_ref, bo_ref,
                kl_ref, klr_ref, vl_ref, kcos_ref, ksin_ref, kc_ref, vc_ref, kw_ref, kwr_ref,
                o_ref, kn_sc, kcn_sc, *, n_ctx_blk):
    hk = pl.program_id(1)
    jb = pl.program_id(2)
    blk = ATT_BLOCK

    @pl.when(jb == 0)
    def _():
        kl = kl_ref[...].astype(F32)
        rs = lax.rsqrt(jnp.sum(kl * kl, -1, keepdims=True) * (0.5 / HEAD64) + NORM_EPS)
        kn = ((kl * kw_ref[...]) * kcos_ref[...] + (klr_ref[...].astype(F32) * kwr_ref[...]) * ksin_ref[...]) * rs
        kn_sc[...] = kn.astype(BF16)
        kc = kc_ref[...].astype(F32)
        rc = lax.rsqrt(jnp.sum(kc * kc, -1, keepdims=True) * (0.5 / HEAD64) + NORM_EPS)
        kcn_sc[...] = (kc * kw_ref[...] * rc).astype(BF16)

    q = q_ref[...].astype(F32)
    mean_sq = _mm(_split(q * q), (bo_ref[...],)) * (1.0 / HEAD64)
    rs = lax.rsqrt(mean_sq + NORM_EPS)
    qn = ((q * qw_ref[...]) * qcos_ref[...]
          + (qr_ref[...].astype(F32) * qwr_ref[...]) * qsin_ref[...]) * rs * (HEAD64 ** -0.5)

    reach = jnp.where(jb >= n_ctx_blk, WINDOW, -1)
    lb = jnp.maximum(jb - n_ctx_blk, 0)
    n_win = 3 * blk
    start = jnp.clip((lb - 1) * blk, 0, N_LAT - n_win)
    start = pl.multiple_of(start, blk)
    n_keys = n_win + N_CTX
    kall = jnp.concatenate([kn_sc[pl.ds(start, n_win), :], kcn_sc[...]], axis=0)
    vall = jnp.concatenate([vl_ref[pl.ds(start, n_win), :], vc_ref[...]], axis=0)
    qpos = lb * blk + lax.broadcasted_iota(jnp.int32, (blk, n_keys), 0)
    col = lax.broadcasted_iota(jnp.int32, (blk, n_keys), 1)
    valid = (col >= n_win) | (jnp.abs(start + col - qpos) <= reach)
    lane = lax.broadcasted_iota(jnp.int32, (1, LANES), 1)
    hmask = [((lane // HEAD64) == e).astype(F32) for e in range(2)]
    groups = range(4)
    sinks = [sink_ref[hk * 4 + g] for g in groups]
    qgs = [(qn[:, (g // 2) * LANES:(g // 2 + 1) * LANES] * hmask[g % 2]).astype(BF16) for g in groups]
    ss = [jnp.where(valid, _dot_nt(qgs[g], kall), NEG) for g in groups]
    ms = [jnp.maximum(jnp.max(ss[g], -1, keepdims=True), sinks[g]) for g in groups]
    ps = [jnp.exp(ss[g] - ms[g]) for g in groups]
    dens = [jnp.sum(ps[g], -1, keepdims=True) + jnp.exp(sinks[g] - ms[g]) for g in groups]
    ogs = [_dot(ps[g].astype(BF16), vall) / dens[g] * hmask[g % 2] for g in groups]
    o_ref[...] = jnp.concatenate([ogs[0] + ogs[1], ogs[2] + ogs[3]], axis=-1)


def _att_call(proj, sink, qcos, qsin, kcos, ksin, qw, qwr, kw, kwr, bo, *, bsz):
    nt = proj.shape[0]
    blk = ATT_BLOCK
    n_ctx_blk, n_lat_blk = N_CTX // blk, N_LAT // blk
    steps = n_ctx_blk + n_lat_blk

    def qtok(b, jb):
        return jnp.where(jb < n_ctx_blk, b * n_ctx_blk + jb, bsz * n_ctx_blk + b * n_lat_blk + (jb - n_ctx_blk))

    def qmap(col0):
        return lambda b, hk, jb, sk: (qtok(b, jb), col0 + hk)

    qpos_map = lambda b, hk, jb, sk: (jb, 0)
    const = lambda b, hk, jb, sk: (0, 0)
    ct_rows = bsz * N_CTX

    def lat_map(col0):
        return lambda b, hk, jb, sk: (ct_rows // N_LAT + b, col0 + hk)

    def ctx_map(col0):
        return lambda b, hk, jb, sk: (b, col0 + hk)

    grid_spec = pltpu.PrefetchScalarGridSpec(
        num_scalar_prefetch=1,
        grid=(bsz, 2, steps),
        in_specs=[pl.BlockSpec((blk, 256), qmap(C_AQ // 256)),
                  pl.BlockSpec((blk, 256), qmap(C_AQ // 256 + 2)),
                  pl.BlockSpec((blk, 256), qpos_map),
                  pl.BlockSpec((blk, 256), qpos_map),
                  pl.BlockSpec((1, 256), const),
                  pl.BlockSpec((1, 256), const),
                  pl.BlockSpec((256, 256), const),
                  pl.BlockSpec((N_LAT, LANES), lat_map(C_AK // LANES)),
                  pl.BlockSpec((N_LAT, LANES), lat_map(C_AK // LANES + 2)),
                  pl.BlockSpec((N_LAT, LANES), lat_map(C_AV // LANES)),
                  pl.BlockSpec((N_LAT, LANES), const),
                  pl.BlockSpec((N_LAT, LANES), const),
                  pl.BlockSpec((N_CTX, LANES), ctx_map(C_AK // LANES)),
                  pl.BlockSpec((N_CTX, LANES), ctx_map(C_AV // LANES)),
                  pl.BlockSpec((1, LANES), const),
                  pl.BlockSpec((1, LANES), const)],
        out_specs=pl.BlockSpec((blk, 256), lambda b, hk, jb, sk: (qtok(b, jb), hk)),
        scratch_shapes=[pltpu.VMEM((N_LAT, LANES), BF16), pltpu.VMEM((N_CTX, LANES), BF16)],
    )
    kern = functools.partial(_att_kernel, n_ctx_blk=n_ctx_blk)
    return pl.pallas_call(
        kern,
        grid_spec=grid_spec,
        out_shape=jax.ShapeDtypeStruct((nt, 512), F32),
        compiler_params=_params(("parallel", "parallel", "arbitrary")),
        name="attention",
    )(sink, proj, proj, qcos, qsin, qw, qwr, bo, proj, proj, proj, kcos, ksin, proj, proj, kw, kwr)


RW_TM = 256


def _rw_prep_kernel(cur_ref, prev_ref, next_ref, shift_ref, wlo_ref, gup_ref, vup_ref, bo_ref, vec_ref, vf_ref,
                    r_o, v_o, kk_o, g_o, bonus_o, lw_o, kd_o, bv_o, *, ct, mix_v):
    i = pl.program_id(0)
    tm = RW_TM
    tok0 = i * tm
    lat_off = tok0 - ct
    is_start = jnp.where(tok0 < ct, tok0 % N_CTX == 0, lat_off % N_LAT == 0)
    is_end = jnp.where(tok0 < ct, (tok0 + tm) % N_CTX == 0, (lat_off + tm) % N_LAT == 0)
    cur = cur_ref[...]
    row = lax.broadcasted_iota(jnp.int32, (tm, 1), 0)
    prev_row = jnp.where(is_start, 0.0, prev_ref[7:8, :])
    next_row = jnp.where(is_end, 0.0, next_ref[0:1, :])
    prev = jnp.where(row == 0, prev_row, pltpu.roll(cur, 1, 0))
    nxt = jnp.where(row == tm - 1, next_row, pltpu.roll(cur, tm - 1, 0))
    c = prev * shift_ref[0:1, :] + cur * shift_ref[1:2, :] + nxt * shift_ref[2:3, :]

    r = c[:, 0:512]
    k = c[:, 512:1024]
    v = c[:, 1024:1536]
    lo = c[:, 1536:1792]
    gd = c[:, 1792:1920]
    vl = c[:, 1920:2048]

    vec = vec_ref[...]
    k_k, k_a, r_k, v0 = vec[0:1], vec[1:2], vec[2:3], vec[3:4]
    bo = (bo_ref[...],)
    if mix_v:
        v = v + (vf_ref[...] - v) * _sigmoid(v0 + _mm(_split(vl), (vup_ref[0], vup_ref[1])))
    g = _mm(_split(_sigmoid(gd)), (gup_ref[0], gup_ref[1]))
    kk = k * k_k
    ss = _mm(_split(kk * kk), bo)
    kk = kk / jnp.maximum(jnp.sqrt(ss), 1e-12)
    z_lora = _mm(_split(jnp.tanh(lo[:, 0:LANES])), (wlo_ref[0, 0], wlo_ref[1, 0]))
    a_lora = _mm(_split(lo[:, LANES:2 * LANES]), (wlo_ref[0, 1], wlo_ref[1, 1]))
    ksum = jnp.zeros_like(k)
    for d in range(2):
        z = vec[4 + d:5 + d] + z_lora[:, d * 512:(d + 1) * 512]
        a = _sigmoid(vec[6 + d:7 + d] + a_lora[:, d * 512:(d + 1) * 512])
        kd = k * (1.0 + (a - 1.0) * k_a)
        lw_o[d] = -RWKV_DECAY_SCALE * _sigmoid(z)
        kd_o[d] = kd
        bv_o[d] = a * kk
        ksum = ksum + kd
    bonus_o[...] = _mm(_split(r * ksum * r_k), bo) * v
    r_o[...] = r
    v_o[...] = v
    kk_o[...] = kk
    g_o[...] = g


def _rw_prep_call(proj, shift, wlo, gup, vup, bo, vec, vfirst, *, ct, mix_v):
    nt = proj.shape[0]
    tm = RW_TM
    nblk8 = nt // 8
    tok = pl.BlockSpec((tm, 512), lambda i: (i, 0))
    dirs = pl.BlockSpec((2, tm, 512), lambda i: (0, i, 0))
    const = lambda i: (0, 0)
    kern = functools.partial(_rw_prep_kernel, ct=ct, mix_v=mix_v)
    one = jax.ShapeDtypeStruct((nt, 512), F32)
    two = jax.ShapeDtypeStruct((2, nt, 512), F32)
    return pl.pallas_call(
        kern,
        grid=(nt // tm,),
        in_specs=[pl.BlockSpec((tm, N_RW), lambda i: (i, 0)),
                  pl.BlockSpec((8, N_RW), lambda i: (jnp.maximum(i * (tm // 8) - 1, 0), 0)),
                  pl.BlockSpec((8, N_RW), lambda i: (jnp.minimum((i + 1) * (tm // 8), nblk8 - 1), 0)),
                  pl.BlockSpec((3, N_RW), const),
                  pl.BlockSpec((2, 2, LANES, 1024), lambda i: (0, 0, 0, 0)),
                  pl.BlockSpec((2, 128, 512), lambda i: (0, 0, 0)),
                  pl.BlockSpec((2, 128, 512), lambda i: (0, 0, 0)),
                  pl.BlockSpec((512, 512), const),
                  pl.BlockSpec((8, 512), const),
                  tok],
        out_specs=[tok, tok, tok, tok, tok, dirs, dirs, dirs],
        out_shape=[one, one, one, one, one, two, two, two],
        compiler_params=_params(("parallel",)),
        name="rwkv_prep",
    )(proj, proj, proj, shift, wlo, gup, vup, bo, vec, vfirst)


SCAN_TERMS = dict(v=1, x=1, y=1, ak=1, ao=1, m=1, rhs=1, pq=1, rb=1, rk=1, bp=1, kp=1, st=1, yq=1, g=1)
RW_STEP_CHUNKS = 2


def _rw_scan_kernel(r_ref, v_ref, kk_ref, lw_ref, kd_ref, bv_ref, lcum_ref, mbig_ref, lvl_ref, eye_ref,
                    y_ref, st_sc):
    d = pl.program_id(1)
    s = pl.program_id(2)
    t = RW_CHUNK
    nt_ = SCAN_TERMS

    @pl.when(s == 0)
    def _():
        st_sc[...] = jnp.zeros_like(st_sc)

    lcum = (lcum_ref[...].astype(BF16),)
    mbig = mbig_ref[...]
    eye = eye_ref[...]
    lane = lax.broadcasted_iota(jnp.int32, (1, LANES), 1)
    m_e = (lane < HEAD64).astype(F32)
    m_o = 1.0 - m_e

    def stack2(x):
        return jnp.concatenate([x * m_e, x * m_o], axis=0)

    pairs = range(4)
    sls = [slice(p * LANES, (p + 1) * LANES) for p in pairs]
    starts = [pl.multiple_of(jnp.where(d == 0, c, RW_STEP_CHUNKS - 1 - c) * t, t) for c in range(RW_STEP_CHUNKS)]
    aw2, rt2, bi2, ki2, bp2t, kp2t, v2s, w_tot = [], [], [], [], [], [], [], []
    for st0 in starts:
        rows = pl.ds(st0, t)
        lw = lw_ref[rows, :]
        cum = _mm(lcum, _split(lw, 3))
        tot = jnp.sum(lw, axis=0, keepdims=True)
        kk, kd, bv = kk_ref[rows, :], kd_ref[rows, :], bv_ref[rows, :]
        aw = -kk * jnp.exp(cum - lw)
        rt = r_ref[rows, :] * jnp.exp(cum)
        e_inv = jnp.exp(-cum)
        e_rem = jnp.exp(tot - cum)
        v = v_ref[rows, :]
        aw2 += [stack2(aw[:, sl]) for sl in sls]
        rt2 += [stack2(rt[:, sl]) for sl in sls]
        bi2 += [stack2((bv * e_inv)[:, sl]) for sl in sls]
        ki2 += [stack2((kd * e_inv)[:, sl]) for sl in sls]
        bp2t += [stack2((bv * e_rem)[:, sl]).T for sl in sls]
        kp2t += [stack2((kd * e_rem)[:, sl]).T for sl in sls]
        v2s += [_split(stack2(v[:, sl]), nt_["v"]) for sl in sls]
        w_tot += [jnp.exp(tot)[:, sl] for sl in sls]

    items = range(RW_STEP_CHUNKS * 4)
    big = [_mm(_split(jnp.concatenate([aw2[i], rt2[i]], axis=0), nt_["x"]),
               _split(jnp.concatenate([bi2[i], ki2[i]], axis=0), nt_["y"]), NT) * mbig for i in items]
    a_ab = [b[0:2 * t, 0:2 * t] for b in big]
    a_ak = [b[0:2 * t, 2 * t:4 * t] for b in big]
    a_rb = [b[2 * t:4 * t, 0:2 * t] for b in big]
    a_rk = [b[2 * t:4 * t, 2 * t:4 * t] for b in big]
    akv = [_mm(_split(a_ak[i], nt_["ak"]), v2s[i]) for i in items]
    inv = [eye + a * lvl_ref[0] for a in a_ab]
    for j in range(1, int(math.log2(t))):
        ms = [_split(m, nt_["m"]) for m in inv]
        low = [_split(_mm(_split(a_ab[i] * lvl_ref[j], nt_["ao"]), ms[i]), nt_["m"]) for i in items]
        inv = [inv[i] + _mm(ms[i], low[i]) for i in items]
    pqs = [_split(_mm(_split(inv[i], nt_["m"]),
                      _split(jnp.concatenate([aw2[i], akv[i]], axis=1), nt_["rhs"])), nt_["pq"]) for i in items]
    tmp = [_mm(_split(a_rb[i], nt_["rb"]), pqs[i]) for i in items]
    ark_v = [_mm(_split(a_rk[i], nt_["rk"]), v2s[i]) for i in items]
    gu = [_mm(_split(bp2t[i], nt_["bp"]), pqs[i]) for i in items]
    kpv = [_mm(_split(kp2t[i], nt_["kp"]), v2s[i]) for i in items]
    yq = [_split(rt2[i] + tmp[i][:, 0:LANES], nt_["yq"]) for i in items]
    y0 = [tmp[i][:, LANES:2 * LANES] + ark_v[i] for i in items]
    g_mat = [_split(gu[i][:, 0:LANES] + eye * w_tot[i], nt_["g"]) for i in items]
    u_mat = [gu[i][:, LANES:2 * LANES] + kpv[i] for i in items]

    st = [st_sc[p] for p in pairs]
    for c in range(RW_STEP_CHUNKS):
        sts = [_split(st[p], nt_["st"]) for p in pairs]
        y2 = [_mm(yq[4 * c + p], sts[p]) + y0[4 * c + p] for p in pairs]
        st = [_mm(g_mat[4 * c + p], sts[p]) + u_mat[4 * c + p] for p in pairs]
        for p in pairs:
            y_ref[pl.ds(starts[c], t), sls[p]] = y2[p][0:t] + y2[p][t:2 * t]
    for p in pairs:
        st_sc[p] = st[p]


def _rw_scan_call(r, v, kk, lw, kd, bv, lcum, mbig, lvl, eye, *, bsz):
    nt = r.shape[0]
    t = RW_CHUNK
    rows = RW_STEP_CHUNKS * t
    nc_ctx, nc_lat = N_CTX // rows, N_LAT // rows
    steps = nc_ctx + nc_lat
    blocks = functools.partial(_scan_blocks, nc_ctx=nc_ctx, nc_lat=nc_lat, bsz=bsz)
    tok = pl.BlockSpec((rows, 512), lambda b, d, s: (blocks(b, d, s)[0], 0))
    tokd = pl.BlockSpec((None, rows, 512), lambda b, d, s: (d, blocks(b, d, s)[0], 0))
    n_lvl = lvl.shape[1]
    return pl.pallas_call(
        _rw_scan_kernel,
        grid=(bsz, 2, steps),
        in_specs=[tok, tok, tok, tokd, tokd, tokd,
                  pl.BlockSpec((None, t, t), lambda b, d, s: (d, 0, 0)),
                  pl.BlockSpec((None, 4 * t, 4 * t), lambda b, d, s: (d, 0, 0)),
                  pl.BlockSpec((None, n_lvl, 2 * t, 2 * t), lambda b, d, s: (d, 0, 0, 0)),
                  pl.BlockSpec((LANES, LANES), lambda b, d, s: (0, 0))],
        out_specs=tokd,
        out_shape=jax.ShapeDtypeStruct((2, nt, 512), F32),
        scratch_shapes=[pltpu.VMEM((4, LANES, LANES), F32)],
        compiler_params=_params(("parallel", "parallel", "arbitrary")),
        name="rwkv_scan",
    )(r, v, kk, lw, kd, bv, lcum, mbig, lvl, eye)


MERGE_TM = 256


def _merge_kernel(x_ref, mod_ref, oret_ref, rg_ref, att_ref, y_ref, bonus_ref, g_ref, gt0_ref, gt1_ref, gt2_ref,
                  bo_ref, gn_ref, wr_ref, wa_ref, ww_ref, wo_ref, o_ref):
    o = oret_ref[0] + oret_ref[1]
    parts = []
    for h in range(8):
        oh = o[:, h * LANES:(h + 1) * LANES]
        mu = jnp.mean(oh, -1, keepdims=True)
        dv = oh - mu
        var = jnp.mean(dv * dv, -1, keepdims=True)
        parts.append(dv * lax.rsqrt(var + NORM_EPS))
    rg = rg_ref[...].astype(F32)
    ret = (rg * _sigmoid(rg)) * jnp.concatenate(parts, axis=-1)

    y = y_ref[0] + y_ref[1]
    bo = (bo_ref[...],)
    mu = _mm(_split(y), bo) * (1.0 / HEAD64)
    dy = y - mu
    var = _mm(_split(dy * dy), bo) * (1.0 / HEAD64)
    yn = dy * lax.rsqrt(var + RWKV_GN_EPS) * gn_ref[0:1, :] + gn_ref[1:2, :]
    rw = (yn + bonus_ref[...]) * g_ref[...]

    merged = (_sigmoid(gt0_ref[...].astype(F32)) * _dot(ret.astype(BF16), wr_ref[...])
              + _sigmoid(gt1_ref[...].astype(F32)) * _dot(att_ref[...].astype(BF16), wa_ref[...])
              + _sigmoid(gt2_ref[...].astype(F32)) * _dot(rw.astype(BF16), ww_ref[...]))
    out = _dot(merged.astype(BF16), wo_ref[...])
    o_ref[...] = x_ref[...] + mod_ref[0, 5:6, :] * out


def _merge_call(x, mod, proj, oret, att, y, bonus, g, bo, gn, wr, wa, ww, wo, *, ct, tm, tile0, n_tiles):
    tok = lambda w, col=0: pl.BlockSpec((tm, w), lambda i: (i + tile0, col))
    tok2 = lambda w: pl.BlockSpec((2, tm, w), lambda i: (0, i + tile0, 0))
    const = lambda i: (0, 0)
    return pl.pallas_call(
        _merge_kernel,
        grid=(n_tiles,),
        in_specs=[tok(D),
                  pl.BlockSpec((1, N_ADA, D), lambda i: (_mod_row((i + tile0) * tm, ct), 0, 0)),
                  tok2(1024), tok(1024, C_RG // 1024), tok(512), tok2(512), tok(512), tok(512),
                  tok(D, C_GT // D), tok(D, C_GT // D + 1), tok(D, C_GT // D + 2),
                  pl.BlockSpec((512, 512), const),
                  pl.BlockSpec((2, 512), const),
                  pl.BlockSpec((1024, D), const),
                  pl.BlockSpec((512, D), const),
                  pl.BlockSpec((512, D), const),
                  pl.BlockSpec((D, D), const)],
        out_specs=pl.BlockSpec((tm, D), lambda i: (i, 0)),
        out_shape=jax.ShapeDtypeStruct((n_tiles * tm, D), F32),
        compiler_params=_params(("parallel",)),
        name="merge",
    )(x, mod, oret, proj, att, y, bonus, g, proj, proj, proj, bo, gn, wr, wa, ww, wo)


def _rope_1d(pos, dim, base):
    n_freq = dim // 2
    inv = np.power(np.float32(base), -(np.arange(n_freq, dtype=np.float32) / np.float32(n_freq))).astype(np.float32)
    return pos.astype(np.float32)[:, None] * inv[None, :]


def _tables():
    ang = _rope_1d(np.arange(N_CTX + N_LAT), HEAD64, 10000.0)
    rcos = np.tile(np.concatenate([np.cos(ang), np.cos(ang)], -1), (1, 8)).astype(np.float32)
    rsin = np.tile(np.concatenate([np.sin(ang), np.sin(ang)], -1), (1, 8)).astype(np.float32)
    rows = N_LAT // GRID_W
    row = np.repeat(np.arange(rows), GRID_W)
    col = np.arange(rows * GRID_W) % GRID_W
    aang = np.concatenate([_rope_1d(row, HEAD64 // 2, 10000.0), _rope_1d(col, HEAD64 // 2, 10000.0)], -1)
    ac = np.concatenate([np.cos(aang), np.cos(aang)], -1).astype(np.float32)
    asn = np.concatenate([np.sin(aang), np.sin(aang)], -1).astype(np.float32)
    qcos = np.concatenate([np.ones((N_CTX, 256), np.float32), np.tile(ac, (1, 4))], 0)
    qsin = np.concatenate([np.zeros((N_CTX, 256), np.float32), np.tile(asn, (1, 4))], 0)
    kcos = np.tile(ac, (1, 2))
    ksin = np.tile(asn, (1, 2))
    c = RET_CHUNK
    pos = np.arange(c, dtype=np.float32)
    diff = pos[:, None] - pos[None, :]
    dist = np.stack([np.where(diff >= 0, diff, -1.0), np.where(diff <= 0, -diff, -1.0)]).astype(np.float32)
    ze = np.stack([c - 1.0 - pos, pos]).astype(np.float32)
    xe = np.stack([pos + 1.0, c - pos]).astype(np.float32)
    ze = np.broadcast_to(ze[:, :, None], (2, c, LANES)).copy()
    xe = np.broadcast_to(xe[:, :, None], (2, c, LANES)).copy()
    t = RW_CHUNK
    ti = np.arange(t)
    low_incl = (ti[None, :] <= ti[:, None]).astype(np.float32)
    low_strict = (ti[None, :] < ti[:, None]).astype(np.float32)
    lcum = np.stack([low_incl, low_incl.T])
    eye2 = np.eye(2, dtype=np.float32)
    mbig = []
    for strict, incl in ((low_strict, low_incl), (low_strict.T, low_incl.T)):
        s2 = np.kron(eye2, strict)
        i2 = np.kron(eye2, incl)
        mbig.append(np.block([[s2, s2], [i2, i2]]))
    mbig = np.stack(mbig).astype(np.float32)
    lvl = []
    for j in range(int(math.log2(t))):
        sz = 2 ** j
        same = (ti[:, None] // (2 * sz)) == (ti[None, :] // (2 * sz))
        m = (same & ((ti[:, None] % (2 * sz)) >= sz) & ((ti[None, :] % (2 * sz)) < sz)).astype(np.float32)
        lvl.append(np.kron(eye2, m))
    lvl = np.stack(lvl)
    lvl = np.stack([lvl, lvl.transpose(0, 2, 1)]).astype(np.float32)
    eye = np.eye(LANES, dtype=np.float32)
    bo64 = np.kron(np.eye(8, dtype=np.float32), np.ones((HEAD64, HEAD64), np.float32))
    return dict(rcos=rcos, rsin=rsin, qcos=qcos, qsin=qsin, kcos=kcos, ksin=ksin, dist=dist, ze=ze, xe=xe,
                lcum=lcum, mbig=mbig, lvl=lvl, eye=eye, bo64=bo64)


def _rot_cols(n_heads):
    half = HEAD64 // 2
    idx, sgn = [], []
    for h in range(n_heads):
        base = h * HEAD64
        idx += list(range(base + half, base + HEAD64)) + list(range(base, base + half))
        sgn += [-1.0] * half + [1.0] * half
    return np.array(idx), np.array(sgn, np.float32)


def _proj_weight(w_in, v_down, dtype=BF16):
    o = np.cumsum([0, 512, 512, 1024, 1024, 512, 128, 128, 1920, 3072])
    rq, rk, rv, rg, aq, ak, av, rw, gt = (w_in[:, o[i]:o[i + 1]] for i in range(9))
    i8, s8 = _rot_cols(8)
    i2, s2 = _rot_cols(2)
    dup = lambda m: jnp.concatenate([m[:, 0:64], m[:, 0:64], m[:, 64:128], m[:, 64:128]], -1)
    z = lambda n: jnp.zeros((D, n), w_in.dtype)
    vd = z(32) if v_down is None else v_down
    w_rw = jnp.concatenate([rw, vd, z(96)], -1)
    w_rest = jnp.concatenate([rq, rq[:, i8] * s8, rk, rk[:, i8] * s8,
                              rv, rg,
                              aq, aq[:, i8] * s8,
                              dup(ak), dup(ak[:, i2] * s2),
                              dup(av), z(256),
                              gt], -1)
    assert w_rw.shape == (D, N_RW) and w_rest.shape == (D, N_REST), (w_rw.shape, w_rest.shape)
    return w_rw.astype(dtype), w_rest.astype(dtype)


def kernel(x, c, ctx, c_ctx, ada_w, ada_b, norm_w, ffn1_w_in, ffn1_w_out, ffn2_w_in, ffn2_w_out, mix_w_in, ret_decay_logit, att_q_norm, att_k_norm, att_sink, rwkv_shift, rwkv_w0, rwkv_w_up, rwkv_a0, rwkv_a_up, rwkv_g_up, rwkv_k_k, rwkv_k_a, rwkv_r_k, rwkv_v0, rwkv_v_down, rwkv_v_up, rwkv_gn_w, rwkv_gn_b, w_branch_ret, w_branch_att, w_branch_rwkv, w_out):
    bsz = x.shape[0]
    depth = ada_w.shape[0]
    ct = bsz * N_CTX
    nt = ct + bsz * N_LAT
    tm = math.gcd(1024, ct)
    tb = {k: jnp.asarray(v) for k, v in _tables().items()}
    bo = tb["bo64"].astype(BF16)
    stack_split = lambda w: jnp.stack(_split(w))

    xs = jnp.concatenate([ctx.reshape(ct, D), x.reshape(bsz * N_LAT, D)], 0)
    rows = 8 * ((bsz + 1 + 7) // 8)
    cond = jnp.zeros((rows, D), F32).at[0].set(c_ctx).at[1:bsz + 1].set(c)
    i8, _ = _rot_cols(8)
    i2, _ = _rot_cols(2)
    vfirst = None
    for l in range(depth):
        last = l == depth - 1
        mod = _ada_call(cond, ada_w[l], ada_b[l]).reshape(rows, N_ADA, D)
        f1_in, f1_out = ffn1_w_in[l].astype(BF16), ffn1_w_out[l].astype(BF16)
        f2_in, f2_out = ffn2_w_in[l].astype(BF16), ffn2_w_out[l].astype(BF16)
        xs = _ffn_call(xs, mod, norm_w[l, 0], f1_in, f1_out, base=0, ct=ct, tm=tm, tile0=0, n_tiles=nt // tm)

        w_rw, w_rest = _proj_weight(mix_w_in[l], rwkv_v_down[l - 1] if l > 0 else None)
        proj_rw = _proj_call(xs, mod, norm_w[l, 1], w_rw, ct=ct, tm=tm, out_dtype=F32)
        proj = _proj_call(xs, mod, norm_w[l, 1], w_rest, ct=ct, tm=tm, out_dtype=BF16)

        log_g = jax.nn.log_sigmoid(ret_decay_logit[l].astype(F32))
        oret = _ret_call(proj, log_g, tb["rcos"], tb["rsin"], tb["dist"], tb["ze"], tb["xe"], bsz=bsz)

        qw = jnp.tile(att_q_norm[l], 4).reshape(1, 256)
        qwr = jnp.tile(att_q_norm[l][i2[:64]], 4).reshape(1, 256)
        kw = jnp.tile(att_k_norm[l], 2).reshape(1, 128)
        kwr = jnp.tile(att_k_norm[l][i2[:64]], 2).reshape(1, 128)
        att = _att_call(proj, att_sink[l].astype(F32), tb["qcos"], tb["qsin"], tb["kcos"], tb["ksin"],
                        qw, qwr, kw, kwr, bo[:256, :256], bsz=bsz)

        shift = jnp.concatenate([rwkv_shift[l], jnp.tile(jnp.array([[0.0], [1.0], [0.0]], F32), (1, 128))], -1)
        wlo = jnp.zeros((2, LANES, 1024), F32)
        for d in range(2):
            wlo = wlo.at[0, d * 64:(d + 1) * 64, d * 512:(d + 1) * 512].set(rwkv_w_up[l, d])
            wlo = wlo.at[1, d * 64:(d + 1) * 64, d * 512:(d + 1) * 512].set(rwkv_a_up[l, d])
        if l > 0:
            vup = jnp.zeros((128, 512), F32).at[0:32].set(rwkv_v_up[l - 1])
            v0 = rwkv_v0[l - 1]
        else:
            vup = jnp.zeros((128, 512), F32)
            v0 = jnp.zeros((512,), F32)
        vec = jnp.stack([rwkv_k_k[l], rwkv_k_a[l], rwkv_r_k[l].reshape(512), v0,
                         rwkv_w0[l, 0], rwkv_w0[l, 1], rwkv_a0[l, 0], rwkv_a0[l, 1]])
        vf_in = vfirst if l > 0 else jnp.zeros((nt, 512), F32)
        r, v, kk, g, bonus, lw, kd, bv = _rw_prep_call(proj_rw, shift, stack_split(wlo), stack_split(rwkv_g_up[l]),
                                                        stack_split(vup), bo, vec, vf_in, ct=ct, mix_v=l > 0)
        if l == 0:
            vfirst = v
        y = _rw_scan_call(r, v, kk, lw, kd, bv, tb["lcum"], tb["mbig"], tb["lvl"], tb["eye"], bsz=bsz)

        gn = jnp.stack([rwkv_gn_w[l], rwkv_gn_b[l]])
        tile0 = ct // MERGE_TM if last else 0
        xs = _merge_call(xs, mod, proj, oret, att, y, bonus, g, bo, gn,
                         w_branch_ret[l].astype(BF16), w_branch_att[l].astype(BF16),
                         w_branch_rwkv[l].astype(BF16), w_out[l].astype(BF16),
                         ct=ct, tm=MERGE_TM, tile0=tile0, n_tiles=nt // MERGE_TM - tile0)
        xs = _ffn_call(xs, mod, norm_w[l, 2], f2_in, f2_out, base=6, ct=0 if last else ct, tm=tm, tile0=0,
                       n_tiles=xs.shape[0] // tm)
    return xs.reshape(bsz, N_LAT, D)
```

```python
import functools
import math

import numpy as np
import jax
import jax.numpy as jnp
from jax import lax
from jax.experimental import pallas as pl
from jax.experimental.pallas import tpu as pltpu

F32 = jnp.float32
BF16 = jnp.bfloat16
HI = lax.Precision.HIGHEST

D = 1024
N_LAT = 2048
N_CTX = 256
GRID_W = 64
N_ADA = 9
D_FF = 2816
NORM_EPS = 1e-6
RET_CHUNK = 128
ATT_BLOCK = 128
WINDOW = 128
RW_CHUNK = 64
RWKV_GN_EPS = 64e-5
RWKV_DECAY_SCALE = 0.6065306597126334
HEAD64 = 64
LANES = 128

N_RW = 2048
C_RQ = 0
C_RV = 2048
C_RG = 3072
C_AQ = 4096
C_AK = 5120
C_AV = 5632
C_GT = 6144
N_REST = 9216

VMEM_LIMIT = 56 * 1024 * 1024


def _dot(a, b, prec=None):
    return jnp.dot(a, b, preferred_element_type=F32, precision=prec)


def _dot_nt(a, b, prec=None):
    return lax.dot_general(a, b, (((1,), (1,)), ((), ())), preferred_element_type=F32, precision=prec)


def _dot_tn(a, b, prec=None):
    return lax.dot_general(a, b, (((0,), (0,)), ((), ())), preferred_element_type=F32, precision=prec)


NN = (((1,), (0,)), ((), ()))
NT = (((1,), (1,)), ((), ()))


def _split(x, terms=2):
    out = []
    for _ in range(terms):
        piece = x.astype(BF16)
        out.append(piece)
        x = x - piece.astype(F32)
    return tuple(out)


def _mm(a, b, dims=NN):
    acc = None
    for i, ai in enumerate(a):
        for j, bj in enumerate(b):
            if i + j < max(len(a), len(b)):
                term = lax.dot_general(ai, bj, dims, preferred_element_type=F32)
                acc = term if acc is None else acc + term
    return acc


def _sigmoid(x):
    return 0.5 * jnp.tanh(0.5 * x) + 0.5


def _params(sem):
    return pltpu.CompilerParams(dimension_semantics=sem, vmem_limit_bytes=VMEM_LIMIT)


def _ada_kernel(c_ref, w_ref, b_ref, o_ref):
    c = c_ref[...]
    o_ref[...] = _dot(c * _sigmoid(c), w_ref[...], HI) + b_ref[...]


def _ada_call(cond, w, b, layer):
    rows = cond.shape[0]
    tn = 1024
    n = w.shape[2]
    return pl.pallas_call(
        _ada_kernel,
        grid=(n // tn,),
        in_specs=[pl.BlockSpec((rows, D), lambda j: (0, 0)),
                  pl.BlockSpec((None, D, tn), lambda j: (layer, 0, j)),
                  pl.BlockSpec((1, tn), lambda j: (0, j))],
        out_specs=pl.BlockSpec((rows, tn), lambda j: (0, j)),
        out_shape=jax.ShapeDtypeStruct((rows, n), F32),
        compiler_params=_params(("arbitrary",)),
        name="ada",
    )(cond, w, b.reshape(1, n))


def _mod_row(tok0, ct):
    return jnp.where(tok0 < ct, 0, 1 + (tok0 - ct) // N_LAT)


def _norm_mod(x, nw, shift, scale):
    y = x * lax.rsqrt(jnp.mean(x * x, -1, keepdims=True) + NORM_EPS) * nw
    return y * (1.0 + scale) + shift


def _ffn_kernel(x_ref, mod_ref, nw_ref, wg_ref, wu_ref, wo_ref, o_ref, h_sc, acc_sc, *, base, n_ff):
    j = pl.program_id(1)

    @pl.when(j == 0)
    def _():
        h = _norm_mod(x_ref[...], nw_ref[...], mod_ref[0, base:base + 1, :], mod_ref[0, base + 1:base + 2, :])
        h_sc[...] = h.astype(BF16)
        acc_sc[...] = jnp.zeros_like(acc_sc)

    h = h_sc[...]
    g = _dot(h, wg_ref[...])
    u = _dot(h, wu_ref[...])
    act = g * _sigmoid(g) * u
    acc_sc[...] += _dot(act.astype(BF16), wo_ref[...])

    @pl.when(j == n_ff - 1)
    def _():
        o_ref[...] = x_ref[...] + 0.5 * mod_ref[0, base + 2:base + 3, :] * acc_sc[...]


def _ffn_call(x, mod, nw, w_in, w_out, *, base, ct, tm, tile0, n_tiles):
    tf = 256
    n_ff = D_FF // tf
    nt = x.shape[0]
    kern = functools.partial(_ffn_kernel, base=base, n_ff=n_ff)
    return pl.pallas_call(
        kern,
        grid=(n_tiles, n_ff),
        in_specs=[pl.BlockSpec((tm, D), lambda i, j: (i + tile0, 0)),
                  pl.BlockSpec((1, N_ADA, D), lambda i, j: (_mod_row((i + tile0) * tm, ct), 0, 0)),
                  pl.BlockSpec((1, D), lambda i, j: (0, 0)),
                  pl.BlockSpec((D, tf), lambda i, j: (0, j)),
                  pl.BlockSpec((D, tf), lambda i, j: (0, j + n_ff)),
                  pl.BlockSpec((tf, D), lambda i, j: (j, 0))],
        out_specs=pl.BlockSpec((tm, D), lambda i, j: (i, 0)),
        out_shape=jax.ShapeDtypeStruct((n_tiles * tm, D), F32),
        scratch_shapes=[pltpu.VMEM((tm, D), BF16), pltpu.VMEM((tm, D), F32)],
        compiler_params=_params(("parallel", "arbitrary")),
        name="ffn",
    )(x, mod, nw.reshape(1, D), w_in, w_in, w_out)


PROJ_TN = 512


def _proj_kernel(x_ref, mod_ref, nw_ref, w_ref, o_ref):
    h = _norm_mod(x_ref[...], nw_ref[...], mod_ref[0, 3:4, :], mod_ref[0, 4:5, :]).astype(BF16)
    for c in range(w_ref.shape[1] // PROJ_TN):
        cols = slice(c * PROJ_TN, (c + 1) * PROJ_TN)
        o_ref[:, cols] = _dot(h, w_ref[:, cols]).astype(o_ref.dtype)


def _proj_call(x, mod, nw, w, *, ct, tm, out_dtype):
    nt = x.shape[0]
    n_cols = w.shape[1]
    return pl.pallas_call(
        _proj_kernel,
        grid=(nt // tm,),
        in_specs=[pl.BlockSpec((tm, D), lambda i: (i, 0)),
                  pl.BlockSpec((1, N_ADA, D), lambda i: (_mod_row(i * tm, ct), 0, 0)),
                  pl.BlockSpec((1, D), lambda i: (0, 0)),
                  pl.BlockSpec((D, n_cols), lambda i: (0, 0), pipeline_mode=pl.Buffered(1))],
        out_specs=pl.BlockSpec((tm, n_cols), lambda i: (i, 0)),
        out_shape=jax.ShapeDtypeStruct((nt, n_cols), out_dtype),
        compiler_params=_params(("parallel",)),
        name="proj",
    )(x, mod, nw.reshape(1, D), w)


def _scan_blocks(b, d, s, *, nc_ctx, nc_lat, bsz):
    in_ctx = s < nc_ctx
    cs = jnp.where(d == 0, s, nc_ctx - 1 - s)
    ls = jnp.where(d == 0, s - nc_ctx, nc_lat - 1 - (s - nc_ctx))
    tok = jnp.where(in_ctx, b * nc_ctx + cs, bsz * nc_ctx + b * nc_lat + ls)
    pos = jnp.where(in_ctx, cs, nc_ctx + ls)
    return tok, pos


def _ret_kernel(lg_ref, q_ref, qr_ref, k_ref, kr_ref, v_ref, cos_ref, sin_ref, dist_ref, ze_ref, xe_ref,
                o_ref, st_sc, dec_sc, xi_sc, zeta_sc):
    d = pl.program_id(1)
    s = pl.program_id(2)

    heads = range(8)

    @pl.when(s == 0)
    def _():
        st_sc[...] = jnp.zeros_like(st_sc)
        dist = dist_ref[...]
        for h in heads:
            lg = lg_ref[d, h]
            dec_sc[h] = jnp.where(dist >= 0.0, jnp.exp(lg * dist), 0.0)
            xi_sc[h] = jnp.exp(lg * xe_ref[...])
            zeta_sc[h] = jnp.exp(lg * ze_ref[...])

    cos = cos_ref[...]
    sin = sin_ref[...]
    q = q_ref[...] * cos + qr_ref[...] * sin
    k = (k_ref[...] * cos + kr_ref[...] * sin) * (HEAD64 ** -0.5)
    lane = lax.broadcasted_iota(jnp.int32, (1, LANES), 1)
    hmask = [((lane // HEAD64) == e).astype(F32) for e in range(2)]
    kps = [k[:, p * LANES:(p + 1) * LANES].astype(BF16) for p in range(4)]
    qms = [(q[:, (h // 2) * LANES:(h // 2 + 1) * LANES] * hmask[h % 2]).astype(BF16) for h in heads]
    vhs = [v_ref[:, h * LANES:(h + 1) * LANES] for h in heads]
    sts = [st_sc[h] for h in heads]
    scs = [(_dot_nt(qms[h], kps[h // 2]) * dec_sc[h]).astype(BF16) for h in heads]
    inter = [_dot(qms[h], sts[h].astype(BF16)) * xi_sc[h] for h in heads]
    for h in heads:
        o_ref[:, h * LANES:(h + 1) * LANES] = (_dot(scs[h], vhs[h].astype(BF16)) + inter[h]).astype(o_ref.dtype)
    us = [_dot_tn(kps[h // 2], (vhs[h] * zeta_sc[h]).astype(BF16)) for h in heads]
    chunk_len = jnp.full((1, LANES), float(RET_CHUNK), F32)
    for h in heads:
        st_sc[h] = sts[h] * jnp.exp(lg_ref[d, h] * chunk_len) + us[h]


def _ret_call(proj, log_g, cos, sin, dist, ze, xe, *, bsz):
    nt = proj.shape[0]
    c = RET_CHUNK
    nc_ctx, nc_lat = N_CTX // c, N_LAT // c
    steps = nc_ctx + nc_lat
    blocks = functools.partial(_scan_blocks, nc_ctx=nc_ctx, nc_lat=nc_lat, bsz=bsz)

    def tok_map(col):
        return lambda b, d, s, lg: (blocks(b, d, s)[0], col)

    pos_map = lambda b, d, s, lg: (blocks(b, d, s)[1], 0)
    dir_map = lambda b, d, s, lg: (d, 0, 0)
    grid_spec = pltpu.PrefetchScalarGridSpec(
        num_scalar_prefetch=1,
        grid=(bsz, 2, steps),
        in_specs=[pl.BlockSpec((c, 512), tok_map(C_RQ // 512)),
                  pl.BlockSpec((c, 512), tok_map(C_RQ // 512 + 1)),
                  pl.BlockSpec((c, 512), tok_map(C_RQ // 512 + 2)),
                  pl.BlockSpec((c, 512), tok_map(C_RQ // 512 + 3)),
                  pl.BlockSpec((c, 1024), tok_map(C_RV // 1024)),
                  pl.BlockSpec((c, 512), pos_map),
                  pl.BlockSpec((c, 512), pos_map),
                  pl.BlockSpec((None, c, c), dir_map),
                  pl.BlockSpec((None, c, LANES), dir_map),
                  pl.BlockSpec((None, c, LANES), dir_map)],
        out_specs=pl.BlockSpec((None, c, 1024), lambda b, d, s, lg: (d, blocks(b, d, s)[0], 0)),
        scratch_shapes=[pltpu.VMEM((8, LANES, LANES), F32), pltpu.VMEM((8, c, c), F32),
                        pltpu.VMEM((8, c, LANES), F32), pltpu.VMEM((8, c, LANES), F32)],
    )
    return pl.pallas_call(
        _ret_kernel,
        grid_spec=grid_spec,
        out_shape=jax.ShapeDtypeStruct((2, nt, 1024), BF16),
        compiler_params=_params(("parallel", "parallel", "arbitrary")),
        name="retention",
    )(log_g, proj, proj, proj, proj, proj, cos, sin, dist, ze, xe)


NEG = -1e30


def _att_kernel(sink_ref, q_ref, qr_ref, qcos_ref, qsin_ref, qw_ref, qwr_ref, bo_ref,
                kl_ref, klr_ref, vl_ref, kcos_ref, ksin_ref, kc_ref, vc_ref, kw_ref, kwr_ref,
                o_ref, kn_sc, kcn_sc, *, n_ctx_blk):
    hk = pl.program_id(1)
    jb = pl.program_id(2)
    blk = ATT_BLOCK

    @pl.when(jb == 0)
    def _():
        kl = kl_ref[...].astype(F32)
        rs = lax.rsqrt(jnp.sum(kl * kl, -1, keepdims=True) * (0.5 / HEAD64) + NORM_EPS)
        kn = ((kl * kw_ref[...]) * kcos_ref[...] + (klr_ref[...].astype(F32) * kwr_ref[...]) * ksin_ref[...]) * rs
        kn_sc[...] = kn.astype(BF16)
        kc = kc_ref[...].astype(F32)
        rc = lax.rsqrt(jnp.sum(kc * kc, -1, keepdims=True) * (0.5 / HEAD64) + NORM_EPS)
        kcn_sc[...] = (kc * kw_ref[...] * rc).astype(BF16)

    q = q_ref[...].astype(F32)
    mean_sq = _mm(_split(q * q), (bo_ref[...],)) * (1.0 / HEAD64)
    rs = lax.rsqrt(mean_sq + NORM_EPS)
    qn = ((q * qw_ref[...]) * qcos_ref[...]
          + (qr_ref[...].astype(F32) * qwr_ref[...]) * qsin_ref[...]) * rs * (HEAD64 ** -0.5)

    reach = jnp.where(jb >= n_ctx_blk, WINDOW, -1)
    lb = jnp.maximum(jb - n_ctx_blk, 0)
    n_win = 3 * blk
    start = jnp.clip((lb - 1) * blk, 0, N_LAT - n_win)
    start = pl.multiple_of(start, blk)
    n_keys = n_win + N_CTX
    kall = jnp.concatenate([kn_sc[pl.ds(start, n_win), :], kcn_sc[...]], axis=0)
    vall = jnp.concatenate([vl_ref[pl.ds(start, n_win), :], vc_ref[...]], axis=0)
    qpos = lb * blk + lax.broadcasted_iota(jnp.int32, (blk, n_keys), 0)
    col = lax.broadcasted_iota(jnp.int32, (blk, n_keys), 1)
    valid = (col >= n_win) | (jnp.abs(start + col - qpos) <= reach)
    lane = lax.broadcasted_iota(jnp.int32, (1, LANES), 1)
    hmask = [((lane // HEAD64) == e).astype(F32) for e in range(2)]
    groups = range(4)
    sinks = [sink_ref[hk * 4 + g] for g in groups]
    qgs = [(qn[:, (g // 2) * LANES:(g // 2 + 1) * LANES] * hmask[g % 2]).astype(BF16) for g in groups]
    ss = [jnp.where(valid, _dot_nt(qgs[g], kall), NEG) for g in groups]
    ms = [jnp.maximum(jnp.max(ss[g], -1, keepdims=True), sinks[g]) for g in groups]
    ps = [jnp.exp(ss[g] - ms[g]) for g in groups]
    dens = [jnp.sum(ps[g], -1, keepdims=True) + jnp.exp(sinks[g] - ms[g]) for g in groups]
    ogs = [_dot(ps[g].astype(BF16), vall) / dens[g] * hmask[g % 2] for g in groups]
    o_ref[...] = jnp.concatenate([ogs[0] + ogs[1], ogs[2] + ogs[3]], axis=-1).astype(o_ref.dtype)


def _att_call(proj, sink, qcos, qsin, kcos, ksin, qw, qwr, kw, kwr, bo, *, bsz):
    nt = proj.shape[0]
    blk = ATT_BLOCK
    n_ctx_blk, n_lat_blk = N_CTX // blk, N_LAT // blk
    steps = n_ctx_blk + n_lat_blk

    def qtok(b, jb):
        return jnp.where(jb < n_ctx_blk, b * n_ctx_blk + jb, bsz * n_ctx_blk + b * n_lat_blk + (jb - n_ctx_blk))

    def qmap(col0):
        return lambda b, hk, jb, sk: (qtok(b, jb), col0 + hk)

    qpos_map = lambda b, hk, jb, sk: (jb, 0)
    const = lambda b, hk, jb, sk: (0, 0)
    ct_rows = bsz * N_CTX

    def lat_map(col0):
        return lambda b, hk, jb, sk: (ct_rows // N_LAT + b, col0 + hk)

    def ctx_map(col0):
        return lambda b, hk, jb, sk: (b, col0 + hk)

    grid_spec = pltpu.PrefetchScalarGridSpec(
        num_scalar_prefetch=1,
        grid=(bsz, 2, steps),
        in_specs=[pl.BlockSpec((blk, 256), qmap(C_AQ // 256)),
                  pl.BlockSpec((blk, 256), qmap(C_AQ // 256 + 2)),
                  pl.BlockSpec((blk, 256), qpos_map),
                  pl.BlockSpec((blk, 256), qpos_map),
                  pl.BlockSpec((1, 256), const),
                  pl.BlockSpec((1, 256), const),
                  pl.BlockSpec((256, 256), const),
                  pl.BlockSpec((N_LAT, LANES), lat_map(C_AK // LANES)),
                  pl.BlockSpec((N_LAT, LANES), lat_map(C_AK // LANES + 2)),
                  pl.BlockSpec((N_LAT, LANES), lat_map(C_AV // LANES)),
                  pl.BlockSpec((N_LAT, LANES), const),
                  pl.BlockSpec((N_LAT, LANES), const),
                  pl.BlockSpec((N_CTX, LANES), ctx_map(C_AK // LANES)),
                  pl.BlockSpec((N_CTX, LANES), ctx_map(C_AV // LANES)),
                  pl.BlockSpec((1, LANES), const),
                  pl.BlockSpec((1, LANES), const)],
        out_specs=pl.BlockSpec((blk, 256), lambda b, hk, jb, sk: (qtok(b, jb), hk)),
        scratch_shapes=[pltpu.VMEM((N_LAT, LANES), BF16), pltpu.VMEM((N_CTX, LANES), BF16)],
    )
    kern = functools.partial(_att_kernel, n_ctx_blk=n_ctx_blk)
    return pl.pallas_call(
        kern,
        grid_spec=grid_spec,
        out_shape=jax.ShapeDtypeStruct((nt, 512), BF16),
        compiler_params=_params(("parallel", "parallel", "arbitrary")),
        name="attention",
    )(sink, proj, proj, qcos, qsin, qw, qwr, bo, proj, proj, proj, kcos, ksin, proj, proj, kw, kwr)


RW_TM = 256


def _rw_prep_kernel(cur_ref, prev_ref, next_ref, shift_ref, wlo_ref, gup_ref, vup_ref, bo_ref, vec_ref, vf_ref,
                    r_o, v_o, kk_o, g_o, bonus_o, lw_o, kd_o, bv_o, *, ct, mix_v):
    i = pl.program_id(0)
    tm = RW_TM
    tok0 = i * tm
    lat_off = tok0 - ct
    is_start = jnp.where(tok0 < ct, tok0 % N_CTX == 0, lat_off % N_LAT == 0)
    is_end = jnp.where(tok0 < ct, (tok0 + tm) % N_CTX == 0, (lat_off + tm) % N_LAT == 0)
    cur = cur_ref[...]
    row = lax.broadcasted_iota(jnp.int32, (tm, 1), 0)
    prev_row = jnp.where(is_start, 0.0, prev_ref[7:8, :])
    next_row = jnp.where(is_end, 0.0, next_ref[0:1, :])
    prev = jnp.where(row == 0, prev_row, pltpu.roll(cur, 1, 0))
    nxt = jnp.where(row == tm - 1, next_row, pltpu.roll(cur, tm - 1, 0))
    c = prev * shift_ref[0:1, :] + cur * shift_ref[1:2, :] + nxt * shift_ref[2:3, :]

    r = c[:, 0:512]
    k = c[:, 512:1024]
    v = c[:, 1024:1536]
    lo = c[:, 1536:1792]
    gd = c[:, 1792:1920]
    vl = c[:, 1920:2048]

    vec = vec_ref[...]
    k_k, k_a, r_k, v0 = vec[0:1], vec[1:2], vec[2:3], vec[3:4]
    bo = (bo_ref[...],)
    if mix_v:
        v = v + (vf_ref[...] - v) * _sigmoid(v0 + _mm(_split(vl), (vup_ref[0], vup_ref[1])))
    g = _mm(_split(_sigmoid(gd)), (gup_ref[0], gup_ref[1]))
    kk = k * k_k
    ss = _mm(_split(kk * kk), bo)
    kk = kk / jnp.maximum(jnp.sqrt(ss), 1e-12)
    z_lora = _mm(_split(jnp.tanh(lo[:, 0:LANES])), (wlo_ref[0, 0], wlo_ref[1, 0]))
    a_lora = _mm(_split(lo[:, LANES:2 * LANES]), (wlo_ref[0, 1], wlo_ref[1, 1]))
    ksum = jnp.zeros_like(k)
    for d in range(2):
        z = vec[4 + d:5 + d] + z_lora[:, d * 512:(d + 1) * 512]
        a = _sigmoid(vec[6 + d:7 + d] + a_lora[:, d * 512:(d + 1) * 512])
        kd = k * (1.0 + (a - 1.0) * k_a)
        lw_o[d] = -RWKV_DECAY_SCALE * _sigmoid(z)
        kd_o[d] = kd
        bv_o[d] = a * kk
        ksum = ksum + kd
    bonus_o[...] = _mm(_split(r * ksum * r_k), bo) * v
    r_o[...] = r
    v_o[...] = v
    kk_o[...] = kk
    g_o[...] = g


def _rw_prep_call(proj, shift, wlo, gup, vup, bo, vec, vfirst, *, ct, mix_v):
    nt = proj.shape[0]
    tm = RW_TM
    nblk8 = nt // 8
    tok = pl.BlockSpec((tm, 512), lambda i: (i, 0))
    dirs = pl.BlockSpec((2, tm, 512), lambda i: (0, i, 0))
    const = lambda i: (0, 0)
    kern = functools.partial(_rw_prep_kernel, ct=ct, mix_v=mix_v)
    one = jax.ShapeDtypeStruct((nt, 512), F32)
    two = jax.ShapeDtypeStruct((2, nt, 512), F32)
    return pl.pallas_call(
        kern,
        grid=(nt // tm,),
        in_specs=[pl.BlockSpec((tm, N_RW), lambda i: (i, 0)),
                  pl.BlockSpec((8, N_RW), lambda i: (jnp.maximum(i * (tm // 8) - 1, 0), 0)),
                  pl.BlockSpec((8, N_RW), lambda i: (jnp.minimum((i + 1) * (tm // 8), nblk8 - 1), 0)),
                  pl.BlockSpec((3, N_RW), const),
                  pl.BlockSpec((2, 2, LANES, 1024), lambda i: (0, 0, 0, 0)),
                  pl.BlockSpec((2, 128, 512), lambda i: (0, 0, 0)),
                  pl.BlockSpec((2, 128, 512), lambda i: (0, 0, 0)),
                  pl.BlockSpec((512, 512), const),
                  pl.BlockSpec((8, 512), const),
                  tok],
        out_specs=[tok, tok, tok, tok, tok, dirs, dirs, dirs],
        out_shape=[one, one, one, one, one, two, two, two],
        compiler_params=_params(("parallel",)),
        name="rwkv_prep",
    )(proj, proj, proj, shift, wlo, gup, vup, bo, vec, vfirst)


SCAN_TERMS = dict(v=1, x=1, y=1, ak=1, ao=1, m=1, rhs=1, pq=1, rb=1, rk=1, bp=1, kp=1, st=1, yq=1, g=1)
RW_STEP_CHUNKS = 4


def _rw_scan_kernel(r_ref, v_ref, kk_ref, lw_ref, kd_ref, bv_ref, lcum_ref, mbig_ref, lvl_ref, eye_ref,
                    y_ref, st_sc):
    d = pl.program_id(1)
    s = pl.program_id(2)
    t = RW_CHUNK
    nt_ = SCAN_TERMS

    @pl.when(s == 0)
    def _():
        st_sc[...] = jnp.zeros_like(st_sc)

    lcum = (lcum_ref[...].astype(BF16),)
    mbig = mbig_ref[...]
    eye = eye_ref[...]
    lane = lax.broadcasted_iota(jnp.int32, (1, LANES), 1)
    m_e = (lane < HEAD64).astype(F32)
    m_o = 1.0 - m_e

    def stack2(x):
        return jnp.concatenate([x * m_e, x * m_o], axis=0)

    pairs = range(4)
    sls = [slice(p * LANES, (p + 1) * LANES) for p in pairs]
    starts = [pl.multiple_of(jnp.where(d == 0, c, RW_STEP_CHUNKS - 1 - c) * t, t) for c in range(RW_STEP_CHUNKS)]
    aw2, rt2, bi2, ki2, bp2t, kp2t, v2s, w_tot = [], [], [], [], [], [], [], []
    for st0 in starts:
        rows = pl.ds(st0, t)
        lw = lw_ref[rows, :]
        cum = _mm(lcum, _split(lw, 3))
        tot = jnp.sum(lw, axis=0, keepdims=True)
        kk, kd, bv = kk_ref[rows, :], kd_ref[rows, :], bv_ref[rows, :]
        aw = -kk * jnp.exp(cum - lw)
        rt = r_ref[rows, :] * jnp.exp(cum)
        e_inv = jnp.exp(-cum)
        e_rem = jnp.exp(tot - cum)
        v = v_ref[rows, :]
        aw2 += [stack2(aw[:, sl]) for sl in sls]
        rt2 += [stack2(rt[:, sl]) for sl in sls]
        bi2 += [stack2((bv * e_inv)[:, sl]) for sl in sls]
        ki2 += [stack2((kd * e_inv)[:, sl]) for sl in sls]
        bp2t += [stack2((bv * e_rem)[:, sl]).T for sl in sls]
        kp2t += [stack2((kd * e_rem)[:, sl]).T for sl in sls]
        v2s += [_split(stack2(v[:, sl]), nt_["v"]) for sl in sls]
        w_tot += [jnp.exp(tot)[:, sl] for sl in sls]

    items = range(RW_STEP_CHUNKS * 4)
    big = [_mm(_split(jnp.concatenate([aw2[i], rt2[i]], axis=0), nt_["x"]),
               _split(jnp.concatenate([bi2[i], ki2[i]], axis=0), nt_["y"]), NT) * mbig for i in items]
    a_ab = [b[0:2 * t, 0:2 * t] for b in big]
    a_ak = [b[0:2 * t, 2 * t:4 * t] for b in big]
    a_rb = [b[2 * t:4 * t, 0:2 * t] for b in big]
    a_rk = [b[2 * t:4 * t, 2 * t:4 * t] for b in big]
    akv = [_mm(_split(a_ak[i], nt_["ak"]), v2s[i]) for i in items]
    inv = [eye + a * lvl_ref[0] for a in a_ab]
    for j in range(1, int(math.log2(t))):
        ms = [_split(m, nt_["m"]) for m in inv]
        low = [_split(_mm(_split(a_ab[i] * lvl_ref[j], nt_["ao"]), ms[i]), nt_["m"]) for i in items]
        inv = [inv[i] + _mm(ms[i], low[i]) for i in items]
    pqs = [_split(_mm(_split(inv[i], nt_["m"]),
                      _split(jnp.concatenate([aw2[i], akv[i]], axis=1), nt_["rhs"])), nt_["pq"]) for i in items]
    tmp = [_mm(_split(a_rb[i], nt_["rb"]), pqs[i]) for i in items]
    ark_v = [_mm(_split(a_rk[i], nt_["rk"]), v2s[i]) for i in items]
    gu = [_mm(_split(bp2t[i], nt_["bp"]), pqs[i]) for i in items]
    kpv = [_mm(_split(kp2t[i], nt_["kp"]), v2s[i]) for i in items]
    yq = [_split(rt2[i] + tmp[i][:, 0:LANES], nt_["yq"]) for i in items]
    y0 = [tmp[i][:, LANES:2 * LANES] + ark_v[i] for i in items]
    g_mat = [_split(gu[i][:, 0:LANES] + eye * w_tot[i], nt_["g"]) for i in items]
    u_mat = [gu[i][:, LANES:2 * LANES] + kpv[i] for i in items]

    st = [st_sc[p] for p in pairs]
    for c in range(RW_STEP_CHUNKS):
        sts = [_split(st[p], nt_["st"]) for p in pairs]
        y2 = [_mm(yq[4 * c + p], sts[p]) + y0[4 * c + p] for p in pairs]
        st = [_mm(g_mat[4 * c + p], sts[p]) + u_mat[4 * c + p] for p in pairs]
        for p in pairs:
            y_ref[pl.ds(starts[c], t), sls[p]] = y2[p][0:t] + y2[p][t:2 * t]
    for p in pairs:
        st_sc[p] = st[p]


def _rw_scan_call(r, v, kk, lw, kd, bv, lcum, mbig, lvl, eye, *, bsz):
    nt = r.shape[0]
    t = RW_CHUNK
    rows = RW_STEP_CHUNKS * t
    nc_ctx, nc_lat = N_CTX // rows, N_LAT // rows
    steps = nc_ctx + nc_lat
    blocks = functools.partial(_scan_blocks, nc_ctx=nc_ctx, nc_lat=nc_lat, bsz=bsz)
    tok = pl.BlockSpec((rows, 512), lambda b, d, s: (blocks(b, d, s)[0], 0))
    tokd = pl.BlockSpec((None, rows, 512), lambda b, d, s: (d, blocks(b, d, s)[0], 0))
    n_lvl = lvl.shape[1]
    return pl.pallas_call(
        _rw_scan_kernel,
        grid=(bsz, 2, steps),
        in_specs=[tok, tok, tok, tokd, tokd, tokd,
                  pl.BlockSpec((None, t, t), lambda b, d, s: (d, 0, 0)),
                  pl.BlockSpec((None, 4 * t, 4 * t), lambda b, d, s: (d, 0, 0)),
                  pl.BlockSpec((None, n_lvl, 2 * t, 2 * t), lambda b, d, s: (d, 0, 0, 0)),
                  pl.BlockSpec((LANES, LANES), lambda b, d, s: (0, 0))],
        out_specs=tokd,
        out_shape=jax.ShapeDtypeStruct((2, nt, 512), F32),
        scratch_shapes=[pltpu.VMEM((4, LANES, LANES), F32)],
        compiler_params=_params(("parallel", "parallel", "arbitrary")),
        name="rwkv_scan",
    )(r, v, kk, lw, kd, bv, lcum, mbig, lvl, eye)


MERGE_TM = 256


def _merge_kernel(x_ref, mod_ref, oret_ref, rg_ref, att_ref, y_ref, bonus_ref, g_ref, gt0_ref, gt1_ref, gt2_ref,
                  bo_ref, gn_ref, wr_ref, wa_ref, ww_ref, wo_ref, o_ref):
    o = oret_ref[0].astype(F32) + oret_ref[1].astype(F32)
    parts = []
    for h in range(8):
        oh = o[:, h * LANES:(h + 1) * LANES]
        mu = jnp.mean(oh, -1, keepdims=True)
        dv = oh - mu
        var = jnp.mean(dv * dv, -1, keepdims=True)
        parts.append(dv * lax.rsqrt(var + NORM_EPS))
    rg = rg_ref[...].astype(F32)
    ret = (rg * _sigmoid(rg)) * jnp.concatenate(parts, axis=-1)

    y = y_ref[0] + y_ref[1]
    bo = (bo_ref[...],)
    mu = _mm(_split(y), bo) * (1.0 / HEAD64)
    dy = y - mu
    var = _mm(_split(dy * dy), bo) * (1.0 / HEAD64)
    yn = dy * lax.rsqrt(var + RWKV_GN_EPS) * gn_ref[0:1, :] + gn_ref[1:2, :]
    rw = (yn + bonus_ref[...]) * g_ref[...]

    merged = (_sigmoid(gt0_ref[...].astype(F32)) * _dot(ret.astype(BF16), wr_ref[...])
              + _sigmoid(gt1_ref[...].astype(F32)) * _dot(att_ref[...].astype(BF16), wa_ref[...])
              + _sigmoid(gt2_ref[...].astype(F32)) * _dot(rw.astype(BF16), ww_ref[...]))
    out = _dot(merged.astype(BF16), wo_ref[...])
    o_ref[...] = x_ref[...] + mod_ref[0, 5:6, :] * out


def _merge_call(x, mod, proj, oret, att, y, bonus, g, bo, gn, wr, wa, ww, wo, *, ct, tm, tile0, n_tiles):
    tok = lambda w, col=0: pl.BlockSpec((tm, w), lambda i: (i + tile0, col))
    tok2 = lambda w: pl.BlockSpec((2, tm, w), lambda i: (0, i + tile0, 0))
    const = lambda i: (0, 0)
    return pl.pallas_call(
        _merge_kernel,
        grid=(n_tiles,),
        in_specs=[tok(D),
                  pl.BlockSpec((1, N_ADA, D), lambda i: (_mod_row((i + tile0) * tm, ct), 0, 0)),
                  tok2(1024), tok(1024, C_RG // 1024), tok(512), tok2(512), tok(512), tok(512),
                  tok(D, C_GT // D), tok(D, C_GT // D + 1), tok(D, C_GT // D + 2),
                  pl.BlockSpec((512, 512), const),
                  pl.BlockSpec((2, 512), const),
                  pl.BlockSpec((1024, D), const),
                  pl.BlockSpec((512, D), const),
                  pl.BlockSpec((512, D), const),
                  pl.BlockSpec((D, D), const)],
        out_specs=pl.BlockSpec((tm, D), lambda i: (i, 0)),
        out_shape=jax.ShapeDtypeStruct((n_tiles * tm, D), F32),
        compiler_params=_params(("parallel",)),
        name="merge",
    )(x, mod, oret, proj, att, y, bonus, g, proj, proj, proj, bo, gn, wr, wa, ww, wo)


def _rope_1d(pos, dim, base):
    n_freq = dim // 2
    inv = np.power(np.float32(base), -(np.arange(n_freq, dtype=np.float32) / np.float32(n_freq))).astype(np.float32)
    return pos.astype(np.float32)[:, None] * inv[None, :]


def _tables():
    ang = _rope_1d(np.arange(N_CTX + N_LAT), HEAD64, 10000.0)
    rcos = np.tile(np.concatenate([np.cos(ang), np.cos(ang)], -1), (1, 8)).astype(np.float32)
    rsin = np.tile(np.concatenate([np.sin(ang), np.sin(ang)], -1), (1, 8)).astype(np.float32)
    rows = N_LAT // GRID_W
    row = np.repeat(np.arange(rows), GRID_W)
    col = np.arange(rows * GRID_W) % GRID_W
    aang = np.concatenate([_rope_1d(row, HEAD64 // 2, 10000.0), _rope_1d(col, HEAD64 // 2, 10000.0)], -1)
    ac = np.concatenate([np.cos(aang), np.cos(aang)], -1).astype(np.float32)
    asn = np.concatenate([np.sin(aang), np.sin(aang)], -1).astype(np.float32)
    qcos = np.concatenate([np.ones((N_CTX, 256), np.float32), np.tile(ac, (1, 4))], 0)
    qsin = np.concatenate([np.zeros((N_CTX, 256), np.float32), np.tile(asn, (1, 4))], 0)
    kcos = np.tile(ac, (1, 2))
    ksin = np.tile(asn, (1, 2))
    c = RET_CHUNK
    pos = np.arange(c, dtype=np.float32)
    diff = pos[:, None] - pos[None, :]
    dist = np.stack([np.where(diff >= 0, diff, -1.0), np.where(diff <= 0, -diff, -1.0)]).astype(np.float32)
    ze = np.stack([c - 1.0 - pos, pos]).astype(np.float32)
    xe = np.stack([pos + 1.0, c - pos]).astype(np.float32)
    ze = np.broadcast_to(ze[:, :, None], (2, c, LANES)).copy()
    xe = np.broadcast_to(xe[:, :, None], (2, c, LANES)).copy()
    t = RW_CHUNK
    ti = np.arange(t)
    low_incl = (ti[None, :] <= ti[:, None]).astype(np.float32)
    low_strict = (ti[None, :] < ti[:, None]).astype(np.float32)
    lcum = np.stack([low_incl, low_incl.T])
    eye2 = np.eye(2, dtype=np.float32)
    mbig = []
    for strict, incl in ((low_strict, low_incl), (low_strict.T, low_incl.T)):
        s2 = np.kron(eye2, strict)
        i2 = np.kron(eye2, incl)
        mbig.append(np.block([[s2, s2], [i2, i2]]))
    mbig = np.stack(mbig).astype(np.float32)
    lvl = []
    for j in range(int(math.log2(t))):
        sz = 2 ** j
        same = (ti[:, None] // (2 * sz)) == (ti[None, :] // (2 * sz))
        m = (same & ((ti[:, None] % (2 * sz)) >= sz) & ((ti[None, :] % (2 * sz)) < sz)).astype(np.float32)
        lvl.append(np.kron(eye2, m))
    lvl = np.stack(lvl)
    lvl = np.stack([lvl, lvl.transpose(0, 2, 1)]).astype(np.float32)
    eye = np.eye(LANES, dtype=np.float32)
    bo64 = np.kron(np.eye(8, dtype=np.float32), np.ones((HEAD64, HEAD64), np.float32))
    return dict(rcos=rcos, rsin=rsin, qcos=qcos, qsin=qsin, kcos=kcos, ksin=ksin, dist=dist, ze=ze, xe=xe,
                lcum=lcum, mbig=mbig, lvl=lvl, eye=eye, bo64=bo64)


def _rot_cols(n_heads):
    half = HEAD64 // 2
    idx, sgn = [], []
    for h in range(n_heads):
        base = h * HEAD64
        idx += list(range(base + half, base + HEAD64)) + list(range(base, base + half))
        sgn += [-1.0] * half + [1.0] * half
    return np.array(idx), np.array(sgn, np.float32)


def _proj_weight(w_in, v_down, dtype=BF16):
    o = np.cumsum([0, 512, 512, 1024, 1024, 512, 128, 128, 1920, 3072])
    rq, rk, rv, rg, aq, ak, av, rw, gt = (w_in[:, o[i]:o[i + 1]] for i in range(9))
    i8, s8 = _rot_cols(8)
    i2, s2 = _rot_cols(2)
    dup = lambda m: jnp.concatenate([m[:, 0:64], m[:, 0:64], m[:, 64:128], m[:, 64:128]], -1)
    z = lambda n: jnp.zeros((D, n), w_in.dtype)
    vd = z(32) if v_down is None else v_down
    w_rw = jnp.concatenate([rw, vd, z(96)], -1)
    w_rest = jnp.concatenate([rq, rq[:, i8] * s8, rk, rk[:, i8] * s8,
                              rv, rg,
                              aq, aq[:, i8] * s8,
                              dup(ak), dup(ak[:, i2] * s2),
                              dup(av), z(256),
                              gt], -1)
    assert w_rw.shape == (D, N_RW) and w_rest.shape == (D, N_REST), (w_rw.shape, w_rest.shape)
    return w_rw.astype(dtype), w_rest.astype(dtype)


def kernel(x, c, ctx, c_ctx, ada_w, ada_b, norm_w, ffn1_w_in, ffn1_w_out, ffn2_w_in, ffn2_w_out, mix_w_in, ret_decay_logit, att_q_norm, att_k_norm, att_sink, rwkv_shift, rwkv_w0, rwkv_w_up, rwkv_a0, rwkv_a_up, rwkv_g_up, rwkv_k_k, rwkv_k_a, rwkv_r_k, rwkv_v0, rwkv_v_down, rwkv_v_up, rwkv_gn_w, rwkv_gn_b, w_branch_ret, w_branch_att, w_branch_rwkv, w_out):
    bsz = x.shape[0]
    depth = ada_w.shape[0]
    ct = bsz * N_CTX
    nt = ct + bsz * N_LAT
    tm = math.gcd(1024, ct)
    ffn_tm = math.gcd(2048, ct)
    tb = {k: jnp.asarray(v) for k, v in _tables().items()}
    bo = tb["bo64"].astype(BF16)
    stack_split = lambda w: jnp.stack(_split(w))

    xs = jnp.concatenate([ctx.reshape(ct, D), x.reshape(bsz * N_LAT, D)], 0)
    rows = 8 * ((bsz + 1 + 7) // 8)
    cond = jnp.zeros((rows, D), F32).at[0].set(c_ctx).at[1:bsz + 1].set(c)
    i8, _ = _rot_cols(8)
    i2, _ = _rot_cols(2)
    vfirst = None
    for l in range(depth):
        last = l == depth - 1
        mod = _ada_call(cond, ada_w, ada_b[l], l).reshape(rows, N_ADA, D)
        f1_in, f1_out = ffn1_w_in[l].astype(BF16), ffn1_w_out[l].astype(BF16)
        f2_in, f2_out = ffn2_w_in[l].astype(BF16), ffn2_w_out[l].astype(BF16)
        xs = _ffn_call(xs, mod, norm_w[l, 0], f1_in, f1_out, base=0, ct=ct, tm=ffn_tm, tile0=0, n_tiles=nt // ffn_tm)

        w_rw, w_rest = _proj_weight(mix_w_in[l], rwkv_v_down[l - 1] if l > 0 else None)
        proj_rw = _proj_call(xs, mod, norm_w[l, 1], w_rw, ct=ct, tm=tm, out_dtype=F32)
        proj = _proj_call(xs, mod, norm_w[l, 1], w_rest, ct=ct, tm=tm // 2, out_dtype=BF16)

        log_g = jax.nn.log_sigmoid(ret_decay_logit[l].astype(F32))
        oret = _ret_call(proj, log_g, tb["rcos"], tb["rsin"], tb["dist"], tb["ze"], tb["xe"], bsz=bsz)

        qw = jnp.tile(att_q_norm[l], 4).reshape(1, 256)
        qwr = jnp.tile(att_q_norm[l][i2[:64]], 4).reshape(1, 256)
        kw = jnp.tile(att_k_norm[l], 2).reshape(1, 128)
        kwr = jnp.tile(att_k_norm[l][i2[:64]], 2).reshape(1, 128)
        att = _att_call(proj, att_sink[l].astype(F32), tb["qcos"], tb["qsin"], tb["kcos"], tb["ksin"],
                        qw, qwr, kw, kwr, bo[:256, :256], bsz=bsz)

        shift = jnp.concatenate([rwkv_shift[l], jnp.tile(jnp.array([[0.0], [1.0], [0.0]], F32), (1, 128))], -1)
        wlo = jnp.zeros((2, LANES, 1024), F32)
        for d in range(2):
            wlo = wlo.at[0, d * 64:(d + 1) * 64, d * 512:(d + 1) * 512].set(rwkv_w_up[l, d])
            wlo = wlo.at[1, d * 64:(d + 1) * 64, d * 512:(d + 1) * 512].set(rwkv_a_up[l, d])
        if l > 0:
            vup = jnp.zeros((128, 512), F32).at[0:32].set(rwkv_v_up[l - 1])
            v0 = rwkv_v0[l - 1]
        else:
            vup = jnp.zeros((128, 512), F32)
            v0 = jnp.zeros((512,), F32)
        vec = jnp.stack([rwkv_k_k[l], rwkv_k_a[l], rwkv_r_k[l].reshape(512), v0,
                         rwkv_w0[l, 0], rwkv_w0[l, 1], rwkv_a0[l, 0], rwkv_a0[l, 1]])
        vf_in = vfirst if l > 0 else jnp.zeros((nt, 512), F32)
        r, v, kk, g, bonus, lw, kd, bv = _rw_prep_call(proj_rw, shift, stack_split(wlo), stack_split(rwkv_g_up[l]),
                                                        stack_split(vup), bo, vec, vf_in, ct=ct, mix_v=l > 0)
        if l == 0:
            vfirst = v
        y = _rw_scan_call(r, v, kk, lw, kd, bv, tb["lcum"], tb["mbig"], tb["lvl"], tb["eye"], bsz=bsz)

        gn = jnp.stack([rwkv_gn_w[l], rwkv_gn_b[l]])
        tile0 = ct // MERGE_TM if last else 0
        xs = _merge_call(xs, mod, proj, oret, att, y, bonus, g, bo, gn,
                         w_branch_ret[l].astype(BF16), w_branch_att[l].astype(BF16),
                         w_branch_rwkv[l].astype(BF16), w_out[l].astype(BF16),
                         ct=ct, tm=MERGE_TM, tile0=tile0, n_tiles=nt // MERGE_TM - tile0)
        xs = _ffn_call(xs, mod, norm_w[l, 2], f2_in, f2_out, base=6, ct=0 if last else ct, tm=ffn_tm, tile0=0,
                       n_tiles=xs.shape[0] // ffn_tm)
    return xs.reshape(bsz, N_LAT, D)
```

```python
import functools
import math

import numpy as np
import jax
import jax.numpy as jnp
from jax import lax
from jax.experimental import pallas as pl
from jax.experimental.pallas import tpu as pltpu

F32 = jnp.float32
BF16 = jnp.bfloat16
HI = lax.Precision.HIGHEST

D = 1024
N_LAT = 2048
N_CTX = 256
GRID_W = 64
N_ADA = 9
D_FF = 2816
NORM_EPS = 1e-6
RET_CHUNK = 128
ATT_BLOCK = 128
WINDOW = 128
RW_CHUNK = 64
RWKV_GN_EPS = 64e-5
RWKV_DECAY_SCALE = 0.6065306597126334
HEAD64 = 64
LANES = 128

N_RW = 2048
C_RQ = 0
C_RV = 2048
C_RG = 3072
C_AQ = 4096
C_AK = 5120
C_AV = 5632
C_GT = 6144
N_REST = 9216

VMEM_LIMIT = 56 * 1024 * 1024


def _dot(a, b, prec=None):
    return jnp.dot(a, b, preferred_element_type=F32, precision=prec)


def _dot_nt(a, b, prec=None):
    return lax.dot_general(a, b, (((1,), (1,)), ((), ())), preferred_element_type=F32, precision=prec)


def _dot_tn(a, b, prec=None):
    return lax.dot_general(a, b, (((0,), (0,)), ((), ())), preferred_element_type=F32, precision=prec)


NN = (((1,), (0,)), ((), ()))
NT = (((1,), (1,)), ((), ()))


def _split(x, terms=2):
    out = []
    for _ in range(terms):
        piece = x.astype(BF16)
        out.append(piece)
        x = x - piece.astype(F32)
    return tuple(out)


def _mm(a, b, dims=NN):
    acc = None
    for i, ai in enumerate(a):
        for j, bj in enumerate(b):
            if i + j < max(len(a), len(b)):
                term = lax.dot_general(ai, bj, dims, preferred_element_type=F32)
                acc = term if acc is None else acc + term
    return acc


def _sigmoid(x):
    return 0.5 * jnp.tanh(0.5 * x) + 0.5


def _params(sem):
    return pltpu.CompilerParams(dimension_semantics=sem, vmem_limit_bytes=VMEM_LIMIT)


def _ada_kernel(c_ref, w_ref, b_ref, o_ref):
    c = c_ref[...]
    o_ref[...] = _dot(c * _sigmoid(c), w_ref[...], HI) + b_ref[...]


def _ada_call(cond, w, b, layer):
    rows = cond.shape[0]
    tn = 1024
    n = w.shape[2]
    return pl.pallas_call(
        _ada_kernel,
        grid=(n // tn,),
        in_specs=[pl.BlockSpec((rows, D), lambda j: (0, 0)),
                  pl.BlockSpec((None, D, tn), lambda j: (layer, 0, j)),
                  pl.BlockSpec((1, tn), lambda j: (0, j))],
        out_specs=pl.BlockSpec((rows, tn), lambda j: (0, j)),
        out_shape=jax.ShapeDtypeStruct((rows, n), F32),
        compiler_params=_params(("arbitrary",)),
        name="ada",
    )(cond, w, b.reshape(1, n))


def _mod_row(tok0, ct):
    return jnp.where(tok0 < ct, 0, 1 + (tok0 - ct) // N_LAT)


def _norm_mod(x, nw, shift, scale):
    y = x * lax.rsqrt(jnp.mean(x * x, -1, keepdims=True) + NORM_EPS) * nw
    return y * (1.0 + scale) + shift


def _ffn_kernel(x_ref, mod_ref, nw_ref, wg_ref, wu_ref, wo_ref, o_ref, h_sc, acc_sc, *, base, n_ff):
    j = pl.program_id(1)

    @pl.when(j == 0)
    def _():
        h = _norm_mod(x_ref[...], nw_ref[...], mod_ref[0, base:base + 1, :], mod_ref[0, base + 1:base + 2, :])
        h_sc[...] = h.astype(BF16)
        acc_sc[...] = jnp.zeros_like(acc_sc)

    h = h_sc[...]
    g = _dot(h, wg_ref[...])
    u = _dot(h, wu_ref[...])
    act = g * _sigmoid(g) * u
    acc_sc[...] += _dot(act.astype(BF16), wo_ref[...])

    @pl.when(j == n_ff - 1)
    def _():
        o_ref[...] = x_ref[...] + 0.5 * mod_ref[0, base + 2:base + 3, :] * acc_sc[...]


def _ffn_call(x, mod, nw, w_in, w_out, layer, *, base, ct, tm, tile0, n_tiles):
    tf = 256
    n_ff = D_FF // tf
    nt = x.shape[0]
    kern = functools.partial(_ffn_kernel, base=base, n_ff=n_ff)
    return pl.pallas_call(
        kern,
        grid=(n_tiles, n_ff),
        in_specs=[pl.BlockSpec((tm, D), lambda i, j: (i + tile0, 0)),
                  pl.BlockSpec((1, N_ADA, D), lambda i, j: (_mod_row((i + tile0) * tm, ct), 0, 0)),
                  pl.BlockSpec((1, D), lambda i, j: (0, 0)),
                  pl.BlockSpec((None, D, tf), lambda i, j: (layer, 0, j)),
                  pl.BlockSpec((None, D, tf), lambda i, j: (layer, 0, j + n_ff)),
                  pl.BlockSpec((None, tf, D), lambda i, j: (layer, j, 0))],
        out_specs=pl.BlockSpec((tm, D), lambda i, j: (i, 0)),
        out_shape=jax.ShapeDtypeStruct((n_tiles * tm, D), F32),
        scratch_shapes=[pltpu.VMEM((tm, D), BF16), pltpu.VMEM((tm, D), F32)],
        compiler_params=_params(("parallel", "arbitrary")),
        name="ffn",
    )(x, mod, nw.reshape(1, D), w_in, w_in, w_out)


PROJ_TN = 512


def _proj_kernel(x_ref, mod_ref, nw_ref, w_ref, o_ref):
    h = _norm_mod(x_ref[...], nw_ref[...], mod_ref[0, 3:4, :], mod_ref[0, 4:5, :]).astype(BF16)
    for c in range(w_ref.shape[1] // PROJ_TN):
        cols = slice(c * PROJ_TN, (c + 1) * PROJ_TN)
        o_ref[:, cols] = _dot(h, w_ref[:, cols]).astype(o_ref.dtype)


def _proj_call(x, mod, nw, w, layer, *, ct, tm, out_dtype):
    nt = x.shape[0]
    n_cols = w.shape[2]
    return pl.pallas_call(
        _proj_kernel,
        grid=(nt // tm,),
        in_specs=[pl.BlockSpec((tm, D), lambda i: (i, 0)),
                  pl.BlockSpec((1, N_ADA, D), lambda i: (_mod_row(i * tm, ct), 0, 0)),
                  pl.BlockSpec((1, D), lambda i: (0, 0)),
                  pl.BlockSpec((None, D, n_cols), lambda i: (layer, 0, 0), pipeline_mode=pl.Buffered(1))],
        out_specs=pl.BlockSpec((tm, n_cols), lambda i: (i, 0)),
        out_shape=jax.ShapeDtypeStruct((nt, n_cols), out_dtype),
        compiler_params=_params(("parallel",)),
        name="proj",
    )(x, mod, nw.reshape(1, D), w)


def _scan_blocks(b, d, s, *, nc_ctx, nc_lat, bsz):
    in_ctx = s < nc_ctx
    cs = jnp.where(d == 0, s, nc_ctx - 1 - s)
    ls = jnp.where(d == 0, s - nc_ctx, nc_lat - 1 - (s - nc_ctx))
    tok = jnp.where(in_ctx, b * nc_ctx + cs, bsz * nc_ctx + b * nc_lat + ls)
    pos = jnp.where(in_ctx, cs, nc_ctx + ls)
    return tok, pos


def _ret_kernel(lg_ref, qkv_ref, cs_ref, dist_ref, ze_ref, xe_ref, o_ref, st_sc, dec_sc, xi_sc, zeta_sc):
    d = pl.program_id(1)
    s = pl.program_id(2)

    heads = range(8)

    @pl.when(s == 0)
    def _():
        st_sc[...] = jnp.zeros_like(st_sc)
        dist = dist_ref[...]
        for h in heads:
            lg = lg_ref[d, h]
            dec_sc[h] = jnp.where(dist >= 0.0, jnp.exp(lg * dist), 0.0)
            xi_sc[h] = jnp.exp(lg * xe_ref[...])
            zeta_sc[h] = jnp.exp(lg * ze_ref[...])

    cos = cs_ref[:, 0:512]
    sin = cs_ref[:, 512:1024]
    q = qkv_ref[:, 0:512] * cos + qkv_ref[:, 512:1024] * sin
    k = (qkv_ref[:, 1024:1536] * cos + qkv_ref[:, 1536:2048] * sin) * (HEAD64 ** -0.5)
    lane = lax.broadcasted_iota(jnp.int32, (1, LANES), 1)
    hmask = [((lane // HEAD64) == e).astype(F32) for e in range(2)]
    kps = [k[:, p * LANES:(p + 1) * LANES].astype(BF16) for p in range(4)]
    qms = [(q[:, (h // 2) * LANES:(h // 2 + 1) * LANES] * hmask[h % 2]).astype(BF16) for h in heads]
    vhs = [qkv_ref[:, C_RV + h * LANES:C_RV + (h + 1) * LANES] for h in heads]
    sts = [st_sc[h] for h in heads]
    scs = [(_dot_nt(qms[h], kps[h // 2]) * dec_sc[h]).astype(BF16) for h in heads]
    inter = [_dot(qms[h], sts[h].astype(BF16)) * xi_sc[h] for h in heads]
    for h in heads:
        o_ref[:, h * LANES:(h + 1) * LANES] = (_dot(scs[h], vhs[h].astype(BF16)) + inter[h]).astype(o_ref.dtype)
    us = [_dot_tn(kps[h // 2], (vhs[h] * zeta_sc[h]).astype(BF16)) for h in heads]
    chunk_len = jnp.full((1, LANES), float(RET_CHUNK), F32)
    for h in heads:
        st_sc[h] = sts[h] * jnp.exp(lg_ref[d, h] * chunk_len) + us[h]


def _ret_call(proj, log_g, cos_sin, dist, ze, xe, *, bsz):
    nt = proj.shape[0]
    c = RET_CHUNK
    nc_ctx, nc_lat = N_CTX // c, N_LAT // c
    steps = nc_ctx + nc_lat
    blocks = functools.partial(_scan_blocks, nc_ctx=nc_ctx, nc_lat=nc_lat, bsz=bsz)
    assert C_RQ == 0 and C_RV == 2048 and C_RG == 3072
    pos_map = lambda b, d, s, lg: (blocks(b, d, s)[1], 0)
    dir_map = lambda b, d, s, lg: (d, 0, 0)
    grid_spec = pltpu.PrefetchScalarGridSpec(
        num_scalar_prefetch=1,
        grid=(bsz, 2, steps),
        in_specs=[pl.BlockSpec((c, C_RG), lambda b, d, s, lg: (blocks(b, d, s)[0], 0)),
                  pl.BlockSpec((c, 1024), pos_map),
                  pl.BlockSpec((None, c, c), dir_map),
                  pl.BlockSpec((None, c, LANES), dir_map),
                  pl.BlockSpec((None, c, LANES), dir_map)],
        out_specs=pl.BlockSpec((None, c, 1024), lambda b, d, s, lg: (d, blocks(b, d, s)[0], 0)),
        scratch_shapes=[pltpu.VMEM((8, LANES, LANES), F32), pltpu.VMEM((8, c, c), F32),
                        pltpu.VMEM((8, c, LANES), F32), pltpu.VMEM((8, c, LANES), F32)],
    )
    return pl.pallas_call(
        _ret_kernel,
        grid_spec=grid_spec,
        out_shape=jax.ShapeDtypeStruct((2, nt, 1024), BF16),
        compiler_params=_params(("parallel", "parallel", "arbitrary")),
        name="retention",
    )(log_g, proj, cos_sin, dist, ze, xe)


NEG = -1e30


def _att_kernel(sink_ref, q_ref, qr_ref, qcos_ref, qsin_ref, qw_ref, qwr_ref, bo_ref,
                kl_ref, klr_ref, vl_ref, kcos_ref, ksin_ref, kc_ref, vc_ref, kw_ref, kwr_ref,
                o_ref, kn_sc, kcn_sc, *, n_ctx_blk):
    hk = pl.program_id(1)
    jb = pl.program_id(2)
    blk = ATT_BLOCK

    @pl.when(jb == 0)
    def _():
        kl = kl_ref[...].astype(F32)
        rs = lax.rsqrt(jnp.sum(kl * kl, -1, keepdims=True) * (0.5 / HEAD64) + NORM_EPS)
        kn = ((kl * kw_ref[...]) * kcos_ref[...] + (klr_ref[...].astype(F32) * kwr_ref[...]) * ksin_ref[...]) * rs
        kn_sc[...] = kn.astype(BF16)
        kc = kc_ref[...].astype(F32)
        rc = lax.rsqrt(jnp.sum(kc * kc, -1, keepdims=True) * (0.5 / HEAD64) + NORM_EPS)
        kcn_sc[...] = (kc * kw_ref[...] * rc).astype(BF16)

    q = q_ref[...].astype(F32)
    mean_sq = _mm(_split(q * q), (bo_ref[...],)) * (1.0 / HEAD64)
    rs = lax.rsqrt(mean_sq + NORM_EPS)
    qn = ((q * qw_ref[...]) * qcos_ref[...]
          + (qr_ref[...].astype(F32) * qwr_ref[...]) * qsin_ref[...]) * rs * (HEAD64 ** -0.5)

    reach = jnp.where(jb >= n_ctx_blk, WINDOW, -1)
    lb = jnp.maximum(jb - n_ctx_blk, 0)
    n_win = 3 * blk
    start = jnp.clip((lb - 1) * blk, 0, N_LAT - n_win)
    start = pl.multiple_of(start, blk)
    n_keys = n_win + N_CTX
    kall = jnp.concatenate([kn_sc[pl.ds(start, n_win), :], kcn_sc[...]], axis=0)
    vall = jnp.concatenate([vl_ref[pl.ds(start, n_win), :], vc_ref[...]], axis=0)
    qpos = lb * blk + lax.broadcasted_iota(jnp.int32, (blk, n_keys), 0)
    col = lax.broadcasted_iota(jnp.int32, (blk, n_keys), 1)
    valid = (col >= n_win) | (jnp.abs(start + col - qpos) <= reach)
    lane = lax.broadcasted_iota(jnp.int32, (1, LANES), 1)
    hmask = [((lane // HEAD64) == e).astype(F32) for e in range(2)]
    groups = range(4)
    sinks = [sink_ref[hk * 4 + g] for g in groups]
    qgs = [(qn[:, (g // 2) * LANES:(g // 2 + 1) * LANES] * hmask[g % 2]).astype(BF16) for g in groups]
    ss = [jnp.where(valid, _dot_nt(qgs[g], kall), NEG) for g in groups]
    ms = [jnp.maximum(jnp.max(ss[g], -1, keepdims=True), sinks[g]) for g in groups]
    ps = [jnp.exp(ss[g] - ms[g]) for g in groups]
    dens = [jnp.sum(ps[g], -1, keepdims=True) + jnp.exp(sinks[g] - ms[g]) for g in groups]
    ogs = [_dot(ps[g].astype(BF16), vall) / dens[g] * hmask[g % 2] for g in groups]
    o_ref[...] = jnp.concatenate([ogs[0] + ogs[1], ogs[2] + ogs[3]], axis=-1).astype(o_ref.dtype)


def _att_call(proj, sink, qcos, qsin, kcos, ksin, qw, qwr, kw, kwr, bo, *, bsz):
    nt = proj.shape[0]
    blk = ATT_BLOCK
    n_ctx_blk, n_lat_blk = N_CTX // blk, N_LAT // blk
    steps = n_ctx_blk + n_lat_blk

    def qtok(b, jb):
        return jnp.where(jb < n_ctx_blk, b * n_ctx_blk + jb, bsz * n_ctx_blk + b * n_lat_blk + (jb - n_ctx_blk))

    def qmap(col0):
        return lambda b, hk, jb, sk: (qtok(b, jb), col0 + hk)

    qpos_map = lambda b, hk, jb, sk: (jb, 0)
    const = lambda b, hk, jb, sk: (0, 0)
    ct_rows = bsz * N_CTX

    def lat_map(col0):
        return lambda b, hk, jb, sk: (ct_rows // N_LAT + b, col0 + hk)

    def ctx_map(col0):
        return lambda b, hk, jb, sk: (b, col0 + hk)

    grid_spec = pltpu.PrefetchScalarGridSpec(
        num_scalar_prefetch=1,
        grid=(bsz, 2, steps),
        in_specs=[pl.BlockSpec((blk, 256), qmap(C_AQ // 256)),
                  pl.BlockSpec((blk, 256), qmap(C_AQ // 256 + 2)),
                  pl.BlockSpec((blk, 256), qpos_map),
                  pl.BlockSpec((blk, 256), qpos_map),
                  pl.BlockSpec((1, 256), const),
                  pl.BlockSpec((1, 256), const),
                  pl.BlockSpec((256, 256), const),
                  pl.BlockSpec((N_LAT, LANES), lat_map(C_AK // LANES)),
                  pl.BlockSpec((N_LAT, LANES), lat_map(C_AK // LANES + 2)),
                  pl.BlockSpec((N_LAT, LANES), lat_map(C_AV // LANES)),
                  pl.BlockSpec((N_LAT, LANES), const),
                  pl.BlockSpec((N_LAT, LANES), const),
                  pl.BlockSpec((N_CTX, LANES), ctx_map(C_AK // LANES)),
                  pl.BlockSpec((N_CTX, LANES), ctx_map(C_AV // LANES)),
                  pl.BlockSpec((1, LANES), const),
                  pl.BlockSpec((1, LANES), const)],
        out_specs=pl.BlockSpec((blk, 256), lambda b, hk, jb, sk: (qtok(b, jb), hk)),
        scratch_shapes=[pltpu.VMEM((N_LAT, LANES), BF16), pltpu.VMEM((N_CTX, LANES), BF16)],
    )
    kern = functools.partial(_att_kernel, n_ctx_blk=n_ctx_blk)
    return pl.pallas_call(
        kern,
        grid_spec=grid_spec,
        out_shape=jax.ShapeDtypeStruct((nt, 512), BF16),
        compiler_params=_params(("parallel", "parallel", "arbitrary")),
        name="attention",
    )(sink, proj, proj, qcos, qsin, qw, qwr, bo, proj, proj, proj, kcos, ksin, proj, proj, kw, kwr)


RW_TM = 256


def _rw_prep_kernel(cur_ref, prev_ref, next_ref, shift_ref, wlo_ref, gup_ref, vup_ref, bo_ref, vec_ref, vf_ref,
                    r_o, v_o, kk_o, g_o, bonus_o, lw_o, kd_o, bv_o, *, ct, mix_v):
    i = pl.program_id(0)
    tm = RW_TM
    tok0 = i * tm
    lat_off = tok0 - ct
    is_start = jnp.where(tok0 < ct, tok0 % N_CTX == 0, lat_off % N_LAT == 0)
    is_end = jnp.where(tok0 < ct, (tok0 + tm) % N_CTX == 0, (lat_off + tm) % N_LAT == 0)
    cur = cur_ref[...]
    row = lax.broadcasted_iota(jnp.int32, (tm, 1), 0)
    prev_row = jnp.where(is_start, 0.0, prev_ref[7:8, :])
    next_row = jnp.where(is_end, 0.0, next_ref[0:1, :])
    prev = jnp.where(row == 0, prev_row, pltpu.roll(cur, 1, 0))
    nxt = jnp.where(row == tm - 1, next_row, pltpu.roll(cur, tm - 1, 0))
    c = prev * shift_ref[0:1, :] + cur * shift_ref[1:2, :] + nxt * shift_ref[2:3, :]

    r = c[:, 0:512]
    k = c[:, 512:1024]
    v = c[:, 1024:1536]
    lo = c[:, 1536:1792]
    gd = c[:, 1792:1920]
    vl = c[:, 1920:2048]

    vec = vec_ref[...]
    k_k, k_a, r_k, v0 = vec[0:1], vec[1:2], vec[2:3], vec[3:4]
    bo = (bo_ref[...],)
    if mix_v:
        v = v + (vf_ref[...] - v) * _sigmoid(v0 + _mm(_split(vl), (vup_ref[0], vup_ref[1])))
    g = _mm(_split(_sigmoid(gd)), (gup_ref[0], gup_ref[1]))
    kk = k * k_k
    ss = _mm(_split(kk * kk), bo)
    kk = kk / jnp.maximum(jnp.sqrt(ss), 1e-12)
    z_lora = _mm(_split(jnp.tanh(lo[:, 0:LANES])), (wlo_ref[0, 0], wlo_ref[1, 0]))
    a_lora = _mm(_split(lo[:, LANES:2 * LANES]), (wlo_ref[0, 1], wlo_ref[1, 1]))
    ksum = jnp.zeros_like(k)
    for d in range(2):
        z = vec[4 + d:5 + d] + z_lora[:, d * 512:(d + 1) * 512]
        a = _sigmoid(vec[6 + d:7 + d] + a_lora[:, d * 512:(d + 1) * 512])
        kd = k * (1.0 + (a - 1.0) * k_a)
        lw_o[d] = -RWKV_DECAY_SCALE * _sigmoid(z)
        kd_o[d] = kd
        bv_o[d] = a * kk
        ksum = ksum + kd
    bonus_o[...] = _mm(_split(r * ksum * r_k), bo) * v
    r_o[...] = r
    v_o[...] = v
    kk_o[...] = kk
    g_o[...] = g


def _rw_prep_call(proj, shift, wlo, gup, vup, bo, vec, vfirst, *, ct, mix_v):
    nt = proj.shape[0]
    tm = RW_TM
    nblk8 = nt // 8
    tok = pl.BlockSpec((tm, 512), lambda i: (i, 0))
    dirs = pl.BlockSpec((2, tm, 512), lambda i: (0, i, 0))
    const = lambda i: (0, 0)
    kern = functools.partial(_rw_prep_kernel, ct=ct, mix_v=mix_v)
    one = jax.ShapeDtypeStruct((nt, 512), F32)
    two = jax.ShapeDtypeStruct((2, nt, 512), F32)
    return pl.pallas_call(
        kern,
        grid=(nt // tm,),
        in_specs=[pl.BlockSpec((tm, N_RW), lambda i: (i, 0)),
                  pl.BlockSpec((8, N_RW), lambda i: (jnp.maximum(i * (tm // 8) - 1, 0), 0)),
                  pl.BlockSpec((8, N_RW), lambda i: (jnp.minimum((i + 1) * (tm // 8), nblk8 - 1), 0)),
                  pl.BlockSpec((3, N_RW), const),
                  pl.BlockSpec((2, 2, LANES, 1024), lambda i: (0, 0, 0, 0)),
                  pl.BlockSpec((2, 128, 512), lambda i: (0, 0, 0)),
                  pl.BlockSpec((2, 128, 512), lambda i: (0, 0, 0)),
                  pl.BlockSpec((512, 512), const),
                  pl.BlockSpec((8, 512), const),
                  tok],
        out_specs=[tok, tok, tok, tok, tok, dirs, dirs, dirs],
        out_shape=[one, one, one, one, one, two, two, two],
        compiler_params=_params(("parallel",)),
        name="rwkv_prep",
    )(proj, proj, proj, shift, wlo, gup, vup, bo, vec, vfirst)


SCAN_TERMS = dict(v=1, x=1, y=1, ak=1, ao=1, m=1, rhs=1, pq=1, rb=1, rk=1, bp=1, kp=1, st=1, yq=1, g=1)
RW_STEP_CHUNKS = 4


def _rw_scan_kernel(r_ref, v_ref, kk_ref, lw_ref, kd_ref, bv_ref, lcum_ref, mbig_ref, lvl_ref, eye_ref,
                    y_ref, st_sc):
    d = pl.program_id(1)
    s = pl.program_id(2)
    t = RW_CHUNK
    nt_ = SCAN_TERMS

    @pl.when(s == 0)
    def _():
        st_sc[...] = jnp.zeros_like(st_sc)

    lcum = (lcum_ref[...].astype(BF16),)
    mbig = mbig_ref[...]
    eye = eye_ref[...]
    lane = lax.broadcasted_iota(jnp.int32, (1, LANES), 1)
    m_e = (lane < HEAD64).astype(F32)
    m_o = 1.0 - m_e

    def stack2(x):
        return jnp.concatenate([x * m_e, x * m_o], axis=0)

    pairs = range(4)
    sls = [slice(p * LANES, (p + 1) * LANES) for p in pairs]
    starts = [pl.multiple_of(jnp.where(d == 0, c, RW_STEP_CHUNKS - 1 - c) * t, t) for c in range(RW_STEP_CHUNKS)]
    aw2, rt2, bi2, ki2, bp2t, kp2t, v2s, w_tot = [], [], [], [], [], [], [], []
    for st0 in starts:
        rows = pl.ds(st0, t)
        lw = lw_ref[rows, :]
        cum = _mm(lcum, _split(lw, 3))
        tot = jnp.sum(lw, axis=0, keepdims=True)
        kk, kd, bv = kk_ref[rows, :], kd_ref[rows, :], bv_ref[rows, :]
        aw = -kk * jnp.exp(cum - lw)
        rt = r_ref[rows, :] * jnp.exp(cum)
        e_inv = jnp.exp(-cum)
        e_rem = jnp.exp(tot - cum)
        v = v_ref[rows, :]
        aw2 += [stack2(aw[:, sl]) for sl in sls]
        rt2 += [stack2(rt[:, sl]) for sl in sls]
        bi2 += [stack2((bv * e_inv)[:, sl]) for sl in sls]
        ki2 += [stack2((kd * e_inv)[:, sl]) for sl in sls]
        bp2t += [stack2((bv * e_rem)[:, sl]).T for sl in sls]
        kp2t += [stack2((kd * e_rem)[:, sl]).T for sl in sls]
        v2s += [_split(stack2(v[:, sl]), nt_["v"]) for sl in sls]
        w_tot += [jnp.exp(tot)[:, sl] for sl in sls]

    items = range(RW_STEP_CHUNKS * 4)
    big = [_mm(_split(jnp.concatenate([aw2[i], rt2[i]], axis=0), nt_["x"]),
               _split(jnp.concatenate([bi2[i], ki2[i]], axis=0), nt_["y"]), NT) * mbig for i in items]
    a_ab = [b[0:2 * t, 0:2 * t] for b in big]
    a_ak = [b[0:2 * t, 2 * t:4 * t] for b in big]
    a_rb = [b[2 * t:4 * t, 0:2 * t] for b in big]
    a_rk = [b[2 * t:4 * t, 2 * t:4 * t] for b in big]
    akv = [_mm(_split(a_ak[i], nt_["ak"]), v2s[i]) for i in items]
    inv = [eye + a * lvl_ref[0] for a in a_ab]
    for j in range(1, int(math.log2(t))):
        ms = [_split(m, nt_["m"]) for m in inv]
        low = [_split(_mm(_split(a_ab[i] * lvl_ref[j], nt_["ao"]), ms[i]), nt_["m"]) for i in items]
        inv = [inv[i] + _mm(ms[i], low[i]) for i in items]
    pqs = [_split(_mm(_split(inv[i], nt_["m"]),
                      _split(jnp.concatenate([aw2[i], akv[i]], axis=1), nt_["rhs"])), nt_["pq"]) for i in items]
    tmp = [_mm(_split(a_rb[i], nt_["rb"]), pqs[i]) for i in items]
    ark_v = [_mm(_split(a_rk[i], nt_["rk"]), v2s[i]) for i in items]
    gu = [_mm(_split(bp2t[i], nt_["bp"]), pqs[i]) for i in items]
    kpv = [_mm(_split(kp2t[i], nt_["kp"]), v2s[i]) for i in items]
    yq = [_split(rt2[i] + tmp[i][:, 0:LANES], nt_["yq"]) for i in items]
    y0 = [tmp[i][:, LANES:2 * LANES] + ark_v[i] for i in items]
    g_mat = [_split(gu[i][:, 0:LANES] + eye * w_tot[i], nt_["g"]) for i in items]
    u_mat = [gu[i][:, LANES:2 * LANES] + kpv[i] for i in items]

    st = [st_sc[p] for p in pairs]
    for c in range(RW_STEP_CHUNKS):
        sts = [_split(st[p], nt_["st"]) for p in pairs]
        y2 = [_mm(yq[4 * c + p], sts[p]) + y0[4 * c + p] for p in pairs]
        st = [_mm(g_mat[4 * c + p], sts[p]) + u_mat[4 * c + p] for p in pairs]
        for p in pairs:
            y_ref[pl.ds(starts[c], t), sls[p]] = y2[p][0:t] + y2[p][t:2 * t]
    for p in pairs:
        st_sc[p] = st[p]


def _rw_scan_call(r, v, kk, lw, kd, bv, lcum, mbig, lvl, eye, *, bsz):
    nt = r.shape[0]
    t = RW_CHUNK
    rows = RW_STEP_CHUNKS * t
    nc_ctx, nc_lat = N_CTX // rows, N_LAT // rows
    steps = nc_ctx + nc_lat
    blocks = functools.partial(_scan_blocks, nc_ctx=nc_ctx, nc_lat=nc_lat, bsz=bsz)
    tok = pl.BlockSpec((rows, 512), lambda b, d, s: (blocks(b, d, s)[0], 0))
    tokd = pl.BlockSpec((None, rows, 512), lambda b, d, s: (d, blocks(b, d, s)[0], 0))
    n_lvl = lvl.shape[1]
    return pl.pallas_call(
        _rw_scan_kernel,
        grid=(bsz, 2, steps),
        in_specs=[tok, tok, tok, tokd, tokd, tokd,
                  pl.BlockSpec((None, t, t), lambda b, d, s: (d, 0, 0)),
                  pl.BlockSpec((None, 4 * t, 4 * t), lambda b, d, s: (d, 0, 0)),
                  pl.BlockSpec((None, n_lvl, 2 * t, 2 * t), lambda b, d, s: (d, 0, 0, 0)),
                  pl.BlockSpec((LANES, LANES), lambda b, d, s: (0, 0))],
        out_specs=tokd,
        out_shape=jax.ShapeDtypeStruct((2, nt, 512), F32),
        scratch_shapes=[pltpu.VMEM((4, LANES, LANES), F32)],
        compiler_params=_params(("parallel", "parallel", "arbitrary")),
        name="rwkv_scan",
    )(r, v, kk, lw, kd, bv, lcum, mbig, lvl, eye)


MERGE_TM = 512


def _merge_kernel(x_ref, mod_ref, oret_ref, rg_ref, att_ref, y_ref, bonus_ref, g_ref, gt_ref,
                  bo_ref, gn_ref, wr_ref, wa_ref, ww_ref, wo_ref, o_ref):
    o = oret_ref[0].astype(F32) + oret_ref[1].astype(F32)
    parts = []
    for h in range(8):
        oh = o[:, h * LANES:(h + 1) * LANES]
        mu = jnp.mean(oh, -1, keepdims=True)
        dv = oh - mu
        var = jnp.mean(dv * dv, -1, keepdims=True)
        parts.append(dv * lax.rsqrt(var + NORM_EPS))
    rg = rg_ref[...].astype(F32)
    ret = (rg * _sigmoid(rg)) * jnp.concatenate(parts, axis=-1)

    y = y_ref[0] + y_ref[1]
    bo = (bo_ref[...],)
    mu = _mm(_split(y), bo) * (1.0 / HEAD64)
    dy = y - mu
    var = _mm(_split(dy * dy), bo) * (1.0 / HEAD64)
    yn = dy * lax.rsqrt(var + RWKV_GN_EPS) * gn_ref[0:1, :] + gn_ref[1:2, :]
    rw = (yn + bonus_ref[...]) * g_ref[...]

    merged = (_sigmoid(gt_ref[:, 0:D].astype(F32)) * _dot(ret.astype(BF16), wr_ref[...])
              + _sigmoid(gt_ref[:, D:2 * D].astype(F32)) * _dot(att_ref[...].astype(BF16), wa_ref[...])
              + _sigmoid(gt_ref[:, 2 * D:3 * D].astype(F32)) * _dot(rw.astype(BF16), ww_ref[...]))
    out = _dot(merged.astype(BF16), wo_ref[...])
    o_ref[...] = x_ref[...] + mod_ref[0, 5:6, :] * out


def _merge_call(x, mod, proj, oret, att, y, bonus, g, bo, gn, wr, wa, ww, wo, layer, *, ct, tm, tile0, n_tiles):
    tok = lambda w, col=0: pl.BlockSpec((tm, w), lambda i: (i + tile0, col))
    tok2 = lambda w: pl.BlockSpec((2, tm, w), lambda i: (0, i + tile0, 0))
    const = lambda i: (0, 0)
    wspec = lambda rows: pl.BlockSpec((None, rows, D), lambda i: (layer, 0, 0), pipeline_mode=pl.Buffered(1))
    assert C_GT % (3 * D) == 0
    return pl.pallas_call(
        _merge_kernel,
        grid=(n_tiles,),
        in_specs=[tok(D),
                  pl.BlockSpec((1, N_ADA, D), lambda i: (_mod_row((i + tile0) * tm, ct), 0, 0)),
                  tok2(1024), tok(1024, C_RG // 1024), tok(512), tok2(512), tok(512), tok(512),
                  tok(3 * D, C_GT // (3 * D)),
                  pl.BlockSpec((512, 512), const),
                  pl.BlockSpec((2, 512), const),
                  wspec(1024), wspec(512), wspec(512), wspec(D)],
        out_specs=pl.BlockSpec((tm, D), lambda i: (i, 0)),
        out_shape=jax.ShapeDtypeStruct((n_tiles * tm, D), F32),
        compiler_params=_params(("parallel",)),
        name="merge",
    )(x, mod, oret, proj, att, y, bonus, g, proj, bo, gn, wr, wa, ww, wo)


def _rope_1d(pos, dim, base):
    n_freq = dim // 2
    inv = np.power(np.float32(base), -(np.arange(n_freq, dtype=np.float32) / np.float32(n_freq))).astype(np.float32)
    return pos.astype(np.float32)[:, None] * inv[None, :]


def _tables():
    ang = _rope_1d(np.arange(N_CTX + N_LAT), HEAD64, 10000.0)
    rcos = np.tile(np.concatenate([np.cos(ang), np.cos(ang)], -1), (1, 8)).astype(np.float32)
    rsin = np.tile(np.concatenate([np.sin(ang), np.sin(ang)], -1), (1, 8)).astype(np.float32)
    rows = N_LAT // GRID_W
    row = np.repeat(np.arange(rows), GRID_W)
    col = np.arange(rows * GRID_W) % GRID_W
    aang = np.concatenate([_rope_1d(row, HEAD64 // 2, 10000.0), _rope_1d(col, HEAD64 // 2, 10000.0)], -1)
    ac = np.concatenate([np.cos(aang), np.cos(aang)], -1).astype(np.float32)
    asn = np.concatenate([np.sin(aang), np.sin(aang)], -1).astype(np.float32)
    qcos = np.concatenate([np.ones((N_CTX, 256), np.float32), np.tile(ac, (1, 4))], 0)
    qsin = np.concatenate([np.zeros((N_CTX, 256), np.float32), np.tile(asn, (1, 4))], 0)
    kcos = np.tile(ac, (1, 2))
    ksin = np.tile(asn, (1, 2))
    c = RET_CHUNK
    pos = np.arange(c, dtype=np.float32)
    diff = pos[:, None] - pos[None, :]
    dist = np.stack([np.where(diff >= 0, diff, -1.0), np.where(diff <= 0, -diff, -1.0)]).astype(np.float32)
    ze = np.stack([c - 1.0 - pos, pos]).astype(np.float32)
    xe = np.stack([pos + 1.0, c - pos]).astype(np.float32)
    ze = np.broadcast_to(ze[:, :, None], (2, c, LANES)).copy()
    xe = np.broadcast_to(xe[:, :, None], (2, c, LANES)).copy()
    t = RW_CHUNK
    ti = np.arange(t)
    low_incl = (ti[None, :] <= ti[:, None]).astype(np.float32)
    low_strict = (ti[None, :] < ti[:, None]).astype(np.float32)
    lcum = np.stack([low_incl, low_incl.T])
    eye2 = np.eye(2, dtype=np.float32)
    mbig = []
    for strict, incl in ((low_strict, low_incl), (low_strict.T, low_incl.T)):
        s2 = np.kron(eye2, strict)
        i2 = np.kron(eye2, incl)
        mbig.append(np.block([[s2, s2], [i2, i2]]))
    mbig = np.stack(mbig).astype(np.float32)
    lvl = []
    for j in range(int(math.log2(t))):
        sz = 2 ** j
        same = (ti[:, None] // (2 * sz)) == (ti[None, :] // (2 * sz))
        m = (same & ((ti[:, None] % (2 * sz)) >= sz) & ((ti[None, :] % (2 * sz)) < sz)).astype(np.float32)
        lvl.append(np.kron(eye2, m))
    lvl = np.stack(lvl)
    lvl = np.stack([lvl, lvl.transpose(0, 2, 1)]).astype(np.float32)
    eye = np.eye(LANES, dtype=np.float32)
    bo64 = np.kron(np.eye(8, dtype=np.float32), np.ones((HEAD64, HEAD64), np.float32))
    return dict(rcos=rcos, rsin=rsin, qcos=qcos, qsin=qsin, kcos=kcos, ksin=ksin, dist=dist, ze=ze, xe=xe,
                lcum=lcum, mbig=mbig, lvl=lvl, eye=eye, bo64=bo64)


def _rot_cols(n_heads):
    half = HEAD64 // 2
    idx, sgn = [], []
    for h in range(n_heads):
        base = h * HEAD64
        idx += list(range(base + half, base + HEAD64)) + list(range(base, base + half))
        sgn += [-1.0] * half + [1.0] * half
    return np.array(idx), np.array(sgn, np.float32)


def _proj_weight(w_in, v_down):
    depth = w_in.shape[0]
    o = np.cumsum([0, 512, 512, 1024, 1024, 512, 128, 128, 1920, 3072])
    rq, rk, rv, rg, aq, ak, av, rw, gt = (w_in[..., o[i]:o[i + 1]] for i in range(9))
    i8, s8 = _rot_cols(8)
    i2, s2 = _rot_cols(2)
    s8, s2 = jnp.asarray(s8, w_in.dtype), jnp.asarray(s2, w_in.dtype)
    dup = lambda m: jnp.concatenate([m[..., 0:64], m[..., 0:64], m[..., 64:128], m[..., 64:128]], -1)
    z = lambda n, d=depth: jnp.zeros((d, D, n), w_in.dtype)
    vd = jnp.concatenate([z(32, 1), v_down.astype(w_in.dtype)], 0)
    w_rw = jnp.concatenate([rw, vd, z(96)], -1)
    w_rest = jnp.concatenate([rq, rq[..., i8] * s8, rk, rk[..., i8] * s8,
                              rv, rg,
                              aq, aq[..., i8] * s8,
                              dup(ak), dup(ak[..., i2] * s2),
                              dup(av), z(256),
                              gt], -1)
    assert w_rw.shape == (depth, D, N_RW) and w_rest.shape == (depth, D, N_REST), (w_rw.shape, w_rest.shape)
    return w_rw, w_rest


def kernel(x, c, ctx, c_ctx, ada_w, ada_b, norm_w, ffn1_w_in, ffn1_w_out, ffn2_w_in, ffn2_w_out, mix_w_in, ret_decay_logit, att_q_norm, att_k_norm, att_sink, rwkv_shift, rwkv_w0, rwkv_w_up, rwkv_a0, rwkv_a_up, rwkv_g_up, rwkv_k_k, rwkv_k_a, rwkv_r_k, rwkv_v0, rwkv_v_down, rwkv_v_up, rwkv_gn_w, rwkv_gn_b, w_branch_ret, w_branch_att, w_branch_rwkv, w_out):
    bsz = x.shape[0]
    depth = ada_w.shape[0]
    ct = bsz * N_CTX
    nt = ct + bsz * N_LAT
    tm = math.gcd(1024, ct)
    ffn_tm = math.gcd(2048, ct)
    tb = {k: jnp.asarray(v) for k, v in _tables().items()}
    bo = tb["bo64"].astype(BF16)
    rcs = jnp.concatenate([tb["rcos"], tb["rsin"]], -1)
    stack_split = lambda w: jnp.stack(_split(w))

    xs = jnp.concatenate([ctx.reshape(ct, D), x.reshape(bsz * N_LAT, D)], 0)
    rows = 8 * ((bsz + 1 + 7) // 8)
    cond = jnp.zeros((rows, D), F32).at[0].set(c_ctx).at[1:bsz + 1].set(c)
    i8, _ = _rot_cols(8)
    i2, _ = _rot_cols(2)
    vfirst = None
    f1_in, f1_out = ffn1_w_in.astype(BF16), ffn1_w_out.astype(BF16)
    f2_in, f2_out = ffn2_w_in.astype(BF16), ffn2_w_out.astype(BF16)
    w_rw, w_rest = _proj_weight(mix_w_in.astype(BF16), rwkv_v_down)
    wb_ret, wb_att = w_branch_ret.astype(BF16), w_branch_att.astype(BF16)
    wb_rw, wb_out = w_branch_rwkv.astype(BF16), w_out.astype(BF16)
    for l in range(depth):
        last = l == depth - 1
        mod = _ada_call(cond, ada_w, ada_b[l], l).reshape(rows, N_ADA, D)
        xs = _ffn_call(xs, mod, norm_w[l, 0], f1_in, f1_out, l, base=0, ct=ct, tm=ffn_tm, tile0=0,
                       n_tiles=nt // ffn_tm)

        proj_rw = _proj_call(xs, mod, norm_w[l, 1], w_rw, l, ct=ct, tm=tm, out_dtype=F32)
        proj = _proj_call(xs, mod, norm_w[l, 1], w_rest, l, ct=ct, tm=tm // 2, out_dtype=BF16)

        log_g = jax.nn.log_sigmoid(ret_decay_logit[l].astype(F32))
        oret = _ret_call(proj, log_g, rcs, tb["dist"], tb["ze"], tb["xe"], bsz=bsz)

        qw = jnp.tile(att_q_norm[l], 4).reshape(1, 256)
        qwr = jnp.tile(att_q_norm[l][i2[:64]], 4).reshape(1, 256)
        kw = jnp.tile(att_k_norm[l], 2).reshape(1, 128)
        kwr = jnp.tile(att_k_norm[l][i2[:64]], 2).reshape(1, 128)
        att = _att_call(proj, att_sink[l].astype(F32), tb["qcos"], tb["qsin"], tb["kcos"], tb["ksin"],
                        qw, qwr, kw, kwr, bo[:256, :256], bsz=bsz)

        shift = jnp.concatenate([rwkv_shift[l], jnp.tile(jnp.array([[0.0], [1.0], [0.0]], F32), (1, 128))], -1)
        wlo = jnp.zeros((2, LANES, 1024), F32)
        for d in range(2):
            wlo = wlo.at[0, d * 64:(d + 1) * 64, d * 512:(d + 1) * 512].set(rwkv_w_up[l, d])
            wlo = wlo.at[1, d * 64:(d + 1) * 64, d * 512:(d + 1) * 512].set(rwkv_a_up[l, d])
        if l > 0:
            vup = jnp.zeros((128, 512), F32).at[0:32].set(rwkv_v_up[l - 1])
            v0 = rwkv_v0[l - 1]
        else:
            vup = jnp.zeros((128, 512), F32)
            v0 = jnp.zeros((512,), F32)
        vec = jnp.stack([rwkv_k_k[l], rwkv_k_a[l], rwkv_r_k[l].reshape(512), v0,
                         rwkv_w0[l, 0], rwkv_w0[l, 1], rwkv_a0[l, 0], rwkv_a0[l, 1]])
        vf_in = vfirst if l > 0 else jnp.zeros((nt, 512), F32)
        r, v, kk, g, bonus, lw, kd, bv = _rw_prep_call(proj_rw, shift, stack_split(wlo), stack_split(rwkv_g_up[l]),
                                                        stack_split(vup), bo, vec, vf_in, ct=ct, mix_v=l > 0)
        if l == 0:
            vfirst = v
        y = _rw_scan_call(r, v, kk, lw, kd, bv, tb["lcum"], tb["mbig"], tb["lvl"], tb["eye"], bsz=bsz)

        gn = jnp.stack([rwkv_gn_w[l], rwkv_gn_b[l]])
        tile0 = ct // MERGE_TM if last else 0
        xs = _merge_call(xs, mod, proj, oret, att, y, bonus, g, bo, gn, wb_ret, wb_att, wb_rw, wb_out, l,
                         ct=ct, tm=MERGE_TM, tile0=tile0, n_tiles=nt // MERGE_TM - tile0)
        xs = _ffn_call(xs, mod, norm_w[l, 2], f2_in, f2_out, l, base=6, ct=0 if last else ct, tm=ffn_tm, tile0=0,
                       n_tiles=xs.shape[0] // ffn_tm)
    return xs.reshape(bsz, N_LAT, D)
```

```python
import functools
import math

import numpy as np
import jax
import jax.numpy as jnp
from jax import lax
from jax.experimental import pallas as pl
from jax.experimental.pallas import tpu as pltpu

F32 = jnp.float32
BF16 = jnp.bfloat16
HI = lax.Precision.HIGHEST

D = 1024
N_LAT = 2048
N_CTX = 256
GRID_W = 64
N_ADA = 9
D_FF = 2816
NORM_EPS = 1e-6
RET_CHUNK = 128
ATT_BLOCK = 128
WINDOW = 128
RW_CHUNK = 64
RWKV_GN_EPS = 64e-5
RWKV_DECAY_SCALE = 0.6065306597126334
HEAD64 = 64
LANES = 128

N_RW = 2048
C_RQ = 0
C_RV = 2048
C_RG = 3072
C_AQ = 4096
C_AK = 5120
C_AV = 5632
C_GT = 6144
N_REST = 9216

VMEM_LIMIT = 56 * 1024 * 1024


def _dot(a, b, prec=None):
    return jnp.dot(a, b, preferred_element_type=F32, precision=prec)


def _dot_nt(a, b, prec=None):
    return lax.dot_general(a, b, (((1,), (1,)), ((), ())), preferred_element_type=F32, precision=prec)


def _dot_tn(a, b, prec=None):
    return lax.dot_general(a, b, (((0,), (0,)), ((), ())), preferred_element_type=F32, precision=prec)


NN = (((1,), (0,)), ((), ()))
NT = (((1,), (1,)), ((), ()))


def _split(x, terms=2):
    out = []
    for _ in range(terms):
        piece = x.astype(BF16)
        out.append(piece)
        x = x - piece.astype(F32)
    return tuple(out)


def _mm(a, b, dims=NN):
    acc = None
    for i, ai in enumerate(a):
        for j, bj in enumerate(b):
            if i + j < max(len(a), len(b)):
                term = lax.dot_general(ai, bj, dims, preferred_element_type=F32)
                acc = term if acc is None else acc + term
    return acc


def _sigmoid(x):
    return 0.5 * jnp.tanh(0.5 * x) + 0.5


def _params(sem):
    return pltpu.CompilerParams(dimension_semantics=sem, vmem_limit_bytes=VMEM_LIMIT)


def _ada_kernel(c_ref, w_ref, b_ref, o_ref):
    c = c_ref[...]
    o_ref[...] = _dot(c * _sigmoid(c), w_ref[...], HI) + b_ref[...]


def _ada_call(cond, w, b, layer):
    rows = cond.shape[0]
    tn = 1024
    n = w.shape[2]
    return pl.pallas_call(
        _ada_kernel,
        grid=(n // tn,),
        in_specs=[pl.BlockSpec((rows, D), lambda j: (0, 0)),
                  pl.BlockSpec((None, D, tn), lambda j: (layer, 0, j)),
                  pl.BlockSpec((1, tn), lambda j: (0, j))],
        out_specs=pl.BlockSpec((rows, tn), lambda j: (0, j)),
        out_shape=jax.ShapeDtypeStruct((rows, n), F32),
        compiler_params=_params(("arbitrary",)),
        name="ada",
    )(cond, w, b.reshape(1, n))


def _mod_row(tok0, ct):
    return jnp.where(tok0 < ct, 0, 1 + (tok0 - ct) // N_LAT)


def _norm_mod(x, nw, shift, scale):
    y = x * lax.rsqrt(jnp.mean(x * x, -1, keepdims=True) + NORM_EPS) * nw
    return y * (1.0 + scale) + shift


def _ffn_kernel(x_ref, mod_ref, nw_ref, wg_ref, wu_ref, wo_ref, o_ref, h_sc, acc_sc, *, base, n_ff):
    j = pl.program_id(1)

    @pl.when(j == 0)
    def _():
        h = _norm_mod(x_ref[...], nw_ref[...], mod_ref[0, base:base + 1, :], mod_ref[0, base + 1:base + 2, :])
        h_sc[...] = h.astype(BF16)
        acc_sc[...] = jnp.zeros_like(acc_sc)

    h = h_sc[...]
    g = _dot(h, wg_ref[...])
    u = _dot(h, wu_ref[...])
    act = g * _sigmoid(g) * u
    acc_sc[...] += _dot(act.astype(BF16), wo_ref[...])

    @pl.when(j == n_ff - 1)
    def _():
        o_ref[...] = x_ref[...] + 0.5 * mod_ref[0, base + 2:base + 3, :] * acc_sc[...]


def _ffn_call(x, mod, nw, w_in, w_out, layer, *, base, ct, tm, tile0, n_tiles):
    tf = 256
    n_ff = D_FF // tf
    nt = x.shape[0]
    kern = functools.partial(_ffn_kernel, base=base, n_ff=n_ff)
    return pl.pallas_call(
        kern,
        grid=(n_tiles, n_ff),
        in_specs=[pl.BlockSpec((tm, D), lambda i, j: (i + tile0, 0)),
                  pl.BlockSpec((1, N_ADA, D), lambda i, j: (_mod_row((i + tile0) * tm, ct), 0, 0)),
                  pl.BlockSpec((1, D), lambda i, j: (0, 0)),
                  pl.BlockSpec((None, D, tf), lambda i, j: (layer, 0, j)),
                  pl.BlockSpec((None, D, tf), lambda i, j: (layer, 0, j + n_ff)),
                  pl.BlockSpec((None, tf, D), lambda i, j: (layer, j, 0))],
        out_specs=pl.BlockSpec((tm, D), lambda i, j: (i, 0)),
        out_shape=jax.ShapeDtypeStruct((n_tiles * tm, D), F32),
        scratch_shapes=[pltpu.VMEM((tm, D), BF16), pltpu.VMEM((tm, D), F32)],
        compiler_params=_params(("parallel", "arbitrary")),
        name="ffn",
    )(x, mod, nw.reshape(1, D), w_in, w_in, w_out)


PROJ_TN = 512


def _proj_kernel(x_ref, mod_ref, nw_ref, w_ref, o_ref):
    h = _norm_mod(x_ref[...], nw_ref[...], mod_ref[0, 3:4, :], mod_ref[0, 4:5, :]).astype(BF16)
    for c in range(w_ref.shape[1] // PROJ_TN):
        cols = slice(c * PROJ_TN, (c + 1) * PROJ_TN)
        o_ref[:, cols] = _dot(h, w_ref[:, cols]).astype(o_ref.dtype)


def _proj_call(x, mod, nw, w, layer, *, ct, tm, out_dtype):
    nt = x.shape[0]
    n_cols = w.shape[2]
    return pl.pallas_call(
        _proj_kernel,
        grid=(nt // tm,),
        in_specs=[pl.BlockSpec((tm, D), lambda i: (i, 0)),
                  pl.BlockSpec((1, N_ADA, D), lambda i: (_mod_row(i * tm, ct), 0, 0)),
                  pl.BlockSpec((1, D), lambda i: (0, 0)),
                  pl.BlockSpec((None, D, n_cols), lambda i: (layer, 0, 0), pipeline_mode=pl.Buffered(1))],
        out_specs=pl.BlockSpec((tm, n_cols), lambda i: (i, 0)),
        out_shape=jax.ShapeDtypeStruct((nt, n_cols), out_dtype),
        compiler_params=_params(("parallel",)),
        name="proj",
    )(x, mod, nw.reshape(1, D), w)


def _scan_blocks(b, d, s, *, nc_ctx, nc_lat, bsz):
    in_ctx = s < nc_ctx
    cs = jnp.where(d == 0, s, nc_ctx - 1 - s)
    ls = jnp.where(d == 0, s - nc_ctx, nc_lat - 1 - (s - nc_ctx))
    tok = jnp.where(in_ctx, b * nc_ctx + cs, bsz * nc_ctx + b * nc_lat + ls)
    pos = jnp.where(in_ctx, cs, nc_ctx + ls)
    return tok, pos


RET_STEP_CHUNKS = 2


def _ret_kernel(lg_ref, qkv_ref, cs_ref, dist_ref, ze_ref, xe_ref, o_ref, st_sc, dec_sc, xi_sc, zeta_sc):
    d = pl.program_id(1)
    s = pl.program_id(2)

    heads = range(8)

    @pl.when(s == 0)
    def _():
        st_sc[...] = jnp.zeros_like(st_sc)
        dist = dist_ref[...]
        for h in heads:
            lg = lg_ref[d, h]
            dec_sc[h] = jnp.where(dist >= 0.0, jnp.exp(lg * dist), 0.0)
            xi_sc[h] = jnp.exp(lg * xe_ref[...])
            zeta_sc[h] = jnp.exp(lg * ze_ref[...])

    c = RET_CHUNK
    lane = lax.broadcasted_iota(jnp.int32, (1, LANES), 1)
    hmask = [((lane // HEAD64) == e).astype(F32) for e in range(2)]
    starts = [pl.multiple_of(jnp.where(d == 0, i, RET_STEP_CHUNKS - 1 - i) * c, c) for i in range(RET_STEP_CHUNKS)]
    qms, kps, vhs = [], [], []
    for st0 in starts:
        rows = pl.ds(st0, c)
        cos = cs_ref[rows, 0:512]
        sin = cs_ref[rows, 512:1024]
        q = qkv_ref[rows, 0:512] * cos + qkv_ref[rows, 512:1024] * sin
        k = (qkv_ref[rows, 1024:1536] * cos + qkv_ref[rows, 1536:2048] * sin) * (HEAD64 ** -0.5)
        kps.append([k[:, p * LANES:(p + 1) * LANES].astype(BF16) for p in range(4)])
        qms.append([(q[:, (h // 2) * LANES:(h // 2 + 1) * LANES] * hmask[h % 2]).astype(BF16) for h in heads])
        vhs.append([qkv_ref[rows, C_RV + h * LANES:C_RV + (h + 1) * LANES] for h in heads])
    chunks = range(RET_STEP_CHUNKS)
    scs = [[(_dot_nt(qms[i][h], kps[i][h // 2]) * dec_sc[h]).astype(BF16) for h in heads] for i in chunks]
    intra = [[_dot(scs[i][h], vhs[i][h].astype(BF16)) for h in heads] for i in chunks]
    us = [[_dot_tn(kps[i][h // 2], (vhs[i][h] * zeta_sc[h]).astype(BF16)) for h in heads] for i in chunks]
    chunk_len = jnp.full((1, LANES), float(c), F32)
    decay = [jnp.exp(lg_ref[d, h] * chunk_len) for h in heads]
    sts = [st_sc[h] for h in heads]
    for i in chunks:
        inter = [_dot(qms[i][h], sts[h].astype(BF16)) * xi_sc[h] for h in heads]
        for h in heads:
            o_ref[pl.ds(starts[i], c), h * LANES:(h + 1) * LANES] = (intra[i][h] + inter[h]).astype(o_ref.dtype)
        sts = [sts[h] * decay[h] + us[i][h] for h in heads]
    for h in heads:
        st_sc[h] = sts[h]


def _ret_call(proj, log_g, cos_sin, dist, ze, xe, *, bsz):
    nt = proj.shape[0]
    c = RET_CHUNK
    rows = RET_STEP_CHUNKS * c
    nc_ctx, nc_lat = N_CTX // rows, N_LAT // rows
    steps = nc_ctx + nc_lat
    blocks = functools.partial(_scan_blocks, nc_ctx=nc_ctx, nc_lat=nc_lat, bsz=bsz)
    assert C_RQ == 0 and C_RV == 2048 and C_RG == 3072
    pos_map = lambda b, d, s, lg: (blocks(b, d, s)[1], 0)
    dir_map = lambda b, d, s, lg: (d, 0, 0)
    grid_spec = pltpu.PrefetchScalarGridSpec(
        num_scalar_prefetch=1,
        grid=(bsz, 2, steps),
        in_specs=[pl.BlockSpec((rows, C_RG), lambda b, d, s, lg: (blocks(b, d, s)[0], 0)),
                  pl.BlockSpec((rows, 1024), pos_map),
                  pl.BlockSpec((None, c, c), dir_map),
                  pl.BlockSpec((None, c, LANES), dir_map),
                  pl.BlockSpec((None, c, LANES), dir_map)],
        out_specs=pl.BlockSpec((None, rows, 1024), lambda b, d, s, lg: (d, blocks(b, d, s)[0], 0)),
        scratch_shapes=[pltpu.VMEM((8, LANES, LANES), F32), pltpu.VMEM((8, c, c), F32),
                        pltpu.VMEM((8, c, LANES), F32), pltpu.VMEM((8, c, LANES), F32)],
    )
    return pl.pallas_call(
        _ret_kernel,
        grid_spec=grid_spec,
        out_shape=jax.ShapeDtypeStruct((2, nt, 1024), BF16),
        compiler_params=_params(("parallel", "parallel", "arbitrary")),
        name="retention",
    )(log_g, proj, cos_sin, dist, ze, xe)


NEG = -1e30


def _att_kernel(sink_ref, q_ref, qr_ref, qcos_ref, qsin_ref, qw_ref, qwr_ref, bo_ref,
                kl_ref, klr_ref, vl_ref, kcos_ref, ksin_ref, kc_ref, vc_ref, kw_ref, kwr_ref,
                o_ref, kn_sc, kcn_sc, *, n_ctx_blk):
    hk = pl.program_id(1)
    jb = pl.program_id(2)
    blk = ATT_BLOCK

    @pl.when(jb == 0)
    def _():
        kl = kl_ref[...].astype(F32)
        rs = lax.rsqrt(jnp.sum(kl * kl, -1, keepdims=True) * (0.5 / HEAD64) + NORM_EPS)
        kn = ((kl * kw_ref[...]) * kcos_ref[...] + (klr_ref[...].astype(F32) * kwr_ref[...]) * ksin_ref[...]) * rs
        kn_sc[...] = kn.astype(BF16)
        kc = kc_ref[...].astype(F32)
        rc = lax.rsqrt(jnp.sum(kc * kc, -1, keepdims=True) * (0.5 / HEAD64) + NORM_EPS)
        kcn_sc[...] = (kc * kw_ref[...] * rc).astype(BF16)

    q = q_ref[...].astype(F32)
    mean_sq = _mm(_split(q * q), (bo_ref[...],)) * (1.0 / HEAD64)
    rs = lax.rsqrt(mean_sq + NORM_EPS)
    qn = ((q * qw_ref[...]) * qcos_ref[...]
          + (qr_ref[...].astype(F32) * qwr_ref[...]) * qsin_ref[...]) * rs * (HEAD64 ** -0.5)

    reach = jnp.where(jb >= n_ctx_blk, WINDOW, -1)
    lb = jnp.maximum(jb - n_ctx_blk, 0)
    n_win = 3 * blk
    start = jnp.clip((lb - 1) * blk, 0, N_LAT - n_win)
    start = pl.multiple_of(start, blk)
    n_keys = n_win + N_CTX
    kall = jnp.concatenate([kn_sc[pl.ds(start, n_win), :], kcn_sc[...]], axis=0)
    vall = jnp.concatenate([vl_ref[pl.ds(start, n_win), :], vc_ref[...]], axis=0)
    qpos = lb * blk + lax.broadcasted_iota(jnp.int32, (blk, n_keys), 0)
    col = lax.broadcasted_iota(jnp.int32, (blk, n_keys), 1)
    valid = (col >= n_win) | (jnp.abs(start + col - qpos) <= reach)
    lane = lax.broadcasted_iota(jnp.int32, (1, LANES), 1)
    hmask = [((lane // HEAD64) == e).astype(F32) for e in range(2)]
    groups = range(4)
    sinks = [sink_ref[hk * 4 + g] for g in groups]
    qgs = [(qn[:, (g // 2) * LANES:(g // 2 + 1) * LANES] * hmask[g % 2]).astype(BF16) for g in groups]
    ss = [jnp.where(valid, _dot_nt(qgs[g], kall), NEG) for g in groups]
    ms = [jnp.maximum(jnp.max(ss[g], -1, keepdims=True), sinks[g]) for g in groups]
    ps = [jnp.exp(ss[g] - ms[g]) for g in groups]
    dens = [jnp.sum(ps[g], -1, keepdims=True) + jnp.exp(sinks[g] - ms[g]) for g in groups]
    ogs = [_dot(ps[g].astype(BF16), vall) / dens[g] * hmask[g % 2] for g in groups]
    o_ref[...] = jnp.concatenate([ogs[0] + ogs[1], ogs[2] + ogs[3]], axis=-1).astype(o_ref.dtype)


def _att_call(proj, sink, qcos, qsin, kcos, ksin, qw, qwr, kw, kwr, bo, *, bsz):
    nt = proj.shape[0]
    blk = ATT_BLOCK
    n_ctx_blk, n_lat_blk = N_CTX // blk, N_LAT // blk
    steps = n_ctx_blk + n_lat_blk

    def qtok(b, jb):
        return jnp.where(jb < n_ctx_blk, b * n_ctx_blk + jb, bsz * n_ctx_blk + b * n_lat_blk + (jb - n_ctx_blk))

    def qmap(col0):
        return lambda b, hk, jb, sk: (qtok(b, jb), col0 + hk)

    qpos_map = lambda b, hk, jb, sk: (jb, 0)
    const = lambda b, hk, jb, sk: (0, 0)
    ct_rows = bsz * N_CTX

    def lat_map(col0):
        return lambda b, hk, jb, sk: (ct_rows // N_LAT + b, col0 + hk)

    def ctx_map(col0):
        return lambda b, hk, jb, sk: (b, col0 + hk)

    grid_spec = pltpu.PrefetchScalarGridSpec(
        num_scalar_prefetch=1,
        grid=(bsz, 2, steps),
        in_specs=[pl.BlockSpec((blk, 256), qmap(C_AQ // 256)),
                  pl.BlockSpec((blk, 256), qmap(C_AQ // 256 + 2)),
                  pl.BlockSpec((blk, 256), qpos_map),
                  pl.BlockSpec((blk, 256), qpos_map),
                  pl.BlockSpec((1, 256), const),
                  pl.BlockSpec((1, 256), const),
                  pl.BlockSpec((256, 256), const),
                  pl.BlockSpec((N_LAT, LANES), lat_map(C_AK // LANES)),
                  pl.BlockSpec((N_LAT, LANES), lat_map(C_AK // LANES + 2)),
                  pl.BlockSpec((N_LAT, LANES), lat_map(C_AV // LANES)),
                  pl.BlockSpec((N_LAT, LANES), const),
                  pl.BlockSpec((N_LAT, LANES), const),
                  pl.BlockSpec((N_CTX, LANES), ctx_map(C_AK // LANES)),
                  pl.BlockSpec((N_CTX, LANES), ctx_map(C_AV // LANES)),
                  pl.BlockSpec((1, LANES), const),
                  pl.BlockSpec((1, LANES), const)],
        out_specs=pl.BlockSpec((blk, 256), lambda b, hk, jb, sk: (qtok(b, jb), hk)),
        scratch_shapes=[pltpu.VMEM((N_LAT, LANES), BF16), pltpu.VMEM((N_CTX, LANES), BF16)],
    )
    kern = functools.partial(_att_kernel, n_ctx_blk=n_ctx_blk)
    return pl.pallas_call(
        kern,
        grid_spec=grid_spec,
        out_shape=jax.ShapeDtypeStruct((nt, 512), BF16),
        compiler_params=_params(("parallel", "parallel", "arbitrary")),
        name="attention",
    )(sink, proj, proj, qcos, qsin, qw, qwr, bo, proj, proj, proj, kcos, ksin, proj, proj, kw, kwr)


RW_TM = 256


def _rw_prep_kernel(cur_ref, prev_ref, next_ref, shift_ref, wlo_ref, gup_ref, vup_ref, bo_ref, vec_ref, vf_ref,
                    r_o, v_o, kk_o, g_o, bonus_o, lw_o, kd_o, bv_o, *, ct, mix_v):
    i = pl.program_id(0)
    tm = RW_TM
    tok0 = i * tm
    lat_off = tok0 - ct
    is_start = jnp.where(tok0 < ct, tok0 % N_CTX == 0, lat_off % N_LAT == 0)
    is_end = jnp.where(tok0 < ct, (tok0 + tm) % N_CTX == 0, (lat_off + tm) % N_LAT == 0)
    cur = cur_ref[...]
    row = lax.broadcasted_iota(jnp.int32, (tm, 1), 0)
    prev_row = jnp.where(is_start, 0.0, prev_ref[7:8, :])
    next_row = jnp.where(is_end, 0.0, next_ref[0:1, :])
    prev = jnp.where(row == 0, prev_row, pltpu.roll(cur, 1, 0))
    nxt = jnp.where(row == tm - 1, next_row, pltpu.roll(cur, tm - 1, 0))
    c = prev * shift_ref[0:1, :] + cur * shift_ref[1:2, :] + nxt * shift_ref[2:3, :]

    r = c[:, 0:512]
    k = c[:, 512:1024]
    v = c[:, 1024:1536]
    lo = c[:, 1536:1792]
    gd = c[:, 1792:1920]
    vl = c[:, 1920:2048]

    vec = vec_ref[...]
    k_k, k_a, r_k, v0 = vec[0:1], vec[1:2], vec[2:3], vec[3:4]
    bo = (bo_ref[...],)
    if mix_v:
        v = v + (vf_ref[...] - v) * _sigmoid(v0 + _mm(_split(vl), (vup_ref[0], vup_ref[1])))
    g = _mm(_split(_sigmoid(gd)), (gup_ref[0], gup_ref[1]))
    kk = k * k_k
    ss = _mm(_split(kk * kk), bo)
    kk = kk / jnp.maximum(jnp.sqrt(ss), 1e-12)
    z_lora = _mm(_split(jnp.tanh(lo[:, 0:LANES])), (wlo_ref[0, 0], wlo_ref[1, 0]))
    a_lora = _mm(_split(lo[:, LANES:2 * LANES]), (wlo_ref[0, 1], wlo_ref[1, 1]))
    ksum = jnp.zeros_like(k)
    for d in range(2):
        z = vec[4 + d:5 + d] + z_lora[:, d * 512:(d + 1) * 512]
        a = _sigmoid(vec[6 + d:7 + d] + a_lora[:, d * 512:(d + 1) * 512])
        kd = k * (1.0 + (a - 1.0) * k_a)
        lw_o[d] = -RWKV_DECAY_SCALE * _sigmoid(z)
        kd_o[d] = kd
        bv_o[d] = a * kk
        ksum = ksum + kd
    bonus_o[...] = _mm(_split(r * ksum * r_k), bo) * v
    r_o[...] = r
    v_o[...] = v
    kk_o[...] = kk
    g_o[...] = g


def _rw_prep_call(proj, shift, wlo, gup, vup, bo, vec, vfirst, *, ct, mix_v):
    nt = proj.shape[0]
    tm = RW_TM
    nblk8 = nt // 8
    tok = pl.BlockSpec((tm, 512), lambda i: (i, 0))
    dirs = pl.BlockSpec((2, tm, 512), lambda i: (0, i, 0))
    const = lambda i: (0, 0)
    kern = functools.partial(_rw_prep_kernel, ct=ct, mix_v=mix_v)
    one = jax.ShapeDtypeStruct((nt, 512), F32)
    two = jax.ShapeDtypeStruct((2, nt, 512), F32)
    return pl.pallas_call(
        kern,
        grid=(nt // tm,),
        in_specs=[pl.BlockSpec((tm, N_RW), lambda i: (i, 0)),
                  pl.BlockSpec((8, N_RW), lambda i: (jnp.maximum(i * (tm // 8) - 1, 0), 0)),
                  pl.BlockSpec((8, N_RW), lambda i: (jnp.minimum((i + 1) * (tm // 8), nblk8 - 1), 0)),
                  pl.BlockSpec((3, N_RW), const),
                  pl.BlockSpec((2, 2, LANES, 1024), lambda i: (0, 0, 0, 0)),
                  pl.BlockSpec((2, 128, 512), lambda i: (0, 0, 0)),
                  pl.BlockSpec((2, 128, 512), lambda i: (0, 0, 0)),
                  pl.BlockSpec((512, 512), const),
                  pl.BlockSpec((8, 512), const),
                  tok],
        out_specs=[tok, tok, tok, tok, tok, dirs, dirs, dirs],
        out_shape=[one, one, one, one, one, two, two, two],
        compiler_params=_params(("parallel",)),
        name="rwkv_prep",
    )(proj, proj, proj, shift, wlo, gup, vup, bo, vec, vfirst)


SCAN_TERMS = dict(v=1, x=1, y=1, ak=1, ao=1, m=1, rhs=1, pq=1, rb=1, rk=1, bp=1, kp=1, st=1, yq=1, g=1)
RW_STEP_CHUNKS = 4


def _rw_scan_kernel(r_ref, v_ref, kk_ref, lw_ref, kd_ref, bv_ref, lcum_ref, mbig_ref, lvl_ref, eye_ref,
                    y_ref, st_sc):
    d = pl.program_id(1)
    s = pl.program_id(2)
    t = RW_CHUNK
    nt_ = SCAN_TERMS

    @pl.when(s == 0)
    def _():
        st_sc[...] = jnp.zeros_like(st_sc)

    lcum = (lcum_ref[...].astype(BF16),)
    mbig = mbig_ref[...]
    eye = eye_ref[...]
    lane = lax.broadcasted_iota(jnp.int32, (1, LANES), 1)
    m_e = (lane < HEAD64).astype(F32)
    m_o = 1.0 - m_e

    def stack2(x):
        return jnp.concatenate([x * m_e, x * m_o], axis=0)

    pairs = range(4)
    sls = [slice(p * LANES, (p + 1) * LANES) for p in pairs]
    starts = [pl.multiple_of(jnp.where(d == 0, c, RW_STEP_CHUNKS - 1 - c) * t, t) for c in range(RW_STEP_CHUNKS)]
    aw2, rt2, bi2, ki2, bp2t, kp2t, v2s, w_tot = [], [], [], [], [], [], [], []
    for st0 in starts:
        rows = pl.ds(st0, t)
        lw = lw_ref[rows, :]
        cum = _mm(lcum, _split(lw, 3))
        tot = jnp.sum(lw, axis=0, keepdims=True)
        kk, kd, bv = kk_ref[rows, :], kd_ref[rows, :], bv_ref[rows, :]
        aw = -kk * jnp.exp(cum - lw)
        rt = r_ref[rows, :] * jnp.exp(cum)
        e_inv = jnp.exp(-cum)
        e_rem = jnp.exp(tot - cum)
        v = v_ref[rows, :]
        aw2 += [stack2(aw[:, sl]) for sl in sls]
        rt2 += [stack2(rt[:, sl]) for sl in sls]
        bi2 += [stack2((bv * e_inv)[:, sl]) for sl in sls]
        ki2 += [stack2((kd * e_inv)[:, sl]) for sl in sls]
        bp2t += [stack2((bv * e_rem)[:, sl]).T for sl in sls]
        kp2t += [stack2((kd * e_rem)[:, sl]).T for sl in sls]
        v2s += [_split(stack2(v[:, sl]), nt_["v"]) for sl in sls]
        w_tot += [jnp.exp(tot)[:, sl] for sl in sls]

    items = range(RW_STEP_CHUNKS * 4)
    big = [_mm(_split(jnp.concatenate([aw2[i], rt2[i]], axis=0), nt_["x"]),
               _split(jnp.concatenate([bi2[i], ki2[i]], axis=0), nt_["y"]), NT) * mbig for i in items]
    a_ab = [b[0:2 * t, 0:2 * t] for b in big]
    a_ak = [b[0:2 * t, 2 * t:4 * t] for b in big]
    a_rb = [b[2 * t:4 * t, 0:2 * t] for b in big]
    a_rk = [b[2 * t:4 * t, 2 * t:4 * t] for b in big]
    akv = [_mm(_split(a_ak[i], nt_["ak"]), v2s[i]) for i in items]
    inv = [eye + a * lvl_ref[0] for a in a_ab]
    for j in range(1, int(math.log2(t))):
        ms = [_split(m, nt_["m"]) for m in inv]
        low = [_split(_mm(_split(a_ab[i] * lvl_ref[j], nt_["ao"]), ms[i]), nt_["m"]) for i in items]
        inv = [inv[i] + _mm(ms[i], low[i]) for i in items]
    pqs = [_split(_mm(_split(inv[i], nt_["m"]),
                      _split(jnp.concatenate([aw2[i], akv[i]], axis=1), nt_["rhs"])), nt_["pq"]) for i in items]
    tmp = [_mm(_split(a_rb[i], nt_["rb"]), pqs[i]) for i in items]
    ark_v = [_mm(_split(a_rk[i], nt_["rk"]), v2s[i]) for i in items]
    gu = [_mm(_split(bp2t[i], nt_["bp"]), pqs[i]) for i in items]
    kpv = [_mm(_split(kp2t[i], nt_["kp"]), v2s[i]) for i in items]
    yq = [_split(rt2[i] + tmp[i][:, 0:LANES], nt_["yq"]) for i in items]
    y0 = [tmp[i][:, LANES:2 * LANES] + ark_v[i] for i in items]
    g_mat = [_split(gu[i][:, 0:LANES] + eye * w_tot[i], nt_["g"]) for i in items]
    u_mat = [gu[i][:, LANES:2 * LANES] + kpv[i] for i in items]

    st = [st_sc[p] for p in pairs]
    for c in range(RW_STEP_CHUNKS):
        sts = [_split(st[p], nt_["st"]) for p in pairs]
        y2 = [_mm(yq[4 * c + p], sts[p]) + y0[4 * c + p] for p in pairs]
        st = [_mm(g_mat[4 * c + p], sts[p]) + u_mat[4 * c + p] for p in pairs]
        for p in pairs:
            y_ref[pl.ds(starts[c], t), sls[p]] = y2[p][0:t] + y2[p][t:2 * t]
    for p in pairs:
        st_sc[p] = st[p]


def _rw_scan_call(r, v, kk, lw, kd, bv, lcum, mbig, lvl, eye, *, bsz):
    nt = r.shape[0]
    t = RW_CHUNK
    rows = RW_STEP_CHUNKS * t
    nc_ctx, nc_lat = N_CTX // rows, N_LAT // rows
    steps = nc_ctx + nc_lat
    blocks = functools.partial(_scan_blocks, nc_ctx=nc_ctx, nc_lat=nc_lat, bsz=bsz)
    tok = pl.BlockSpec((rows, 512), lambda b, d, s: (blocks(b, d, s)[0], 0))
    tokd = pl.BlockSpec((None, rows, 512), lambda b, d, s: (d, blocks(b, d, s)[0], 0))
    n_lvl = lvl.shape[1]
    return pl.pallas_call(
        _rw_scan_kernel,
        grid=(bsz, 2, steps),
        in_specs=[tok, tok, tok, tokd, tokd, tokd,
                  pl.BlockSpec((None, t, t), lambda b, d, s: (d, 0, 0)),
                  pl.BlockSpec((None, 4 * t, 4 * t), lambda b, d, s: (d, 0, 0)),
                  pl.BlockSpec((None, n_lvl, 2 * t, 2 * t), lambda b, d, s: (d, 0, 0, 0)),
                  pl.BlockSpec((LANES, LANES), lambda b, d, s: (0, 0))],
        out_specs=tokd,
        out_shape=jax.ShapeDtypeStruct((2, nt, 512), F32),
        scratch_shapes=[pltpu.VMEM((4, LANES, LANES), F32)],
        compiler_params=_params(("parallel", "parallel", "arbitrary")),
        name="rwkv_scan",
    )(r, v, kk, lw, kd, bv, lcum, mbig, lvl, eye)


MERGE_TM = 512


def _merge_kernel(x_ref, mod_ref, oret_ref, rg_ref, att_ref, y_ref, bonus_ref, g_ref, gt_ref,
                  bo_ref, gn_ref, wr_ref, wa_ref, ww_ref, wo_ref, o_ref):
    o = oret_ref[0].astype(F32) + oret_ref[1].astype(F32)
    parts = []
    for h in range(8):
        oh = o[:, h * LANES:(h + 1) * LANES]
        mu = jnp.mean(oh, -1, keepdims=True)
        dv = oh - mu
        var = jnp.mean(dv * dv, -1, keepdims=True)
        parts.append(dv * lax.rsqrt(var + NORM_EPS))
    rg = rg_ref[...].astype(F32)
    ret = (rg * _sigmoid(rg)) * jnp.concatenate(parts, axis=-1)

    y = y_ref[0] + y_ref[1]
    bo = (bo_ref[...],)
    mu = _mm(_split(y), bo) * (1.0 / HEAD64)
    dy = y - mu
    var = _mm(_split(dy * dy), bo) * (1.0 / HEAD64)
    yn = dy * lax.rsqrt(var + RWKV_GN_EPS) * gn_ref[0:1, :] + gn_ref[1:2, :]
    rw = (yn + bonus_ref[...]) * g_ref[...]

    merged = (_sigmoid(gt_ref[:, 0:D].astype(F32)) * _dot(ret.astype(BF16), wr_ref[...])
              + _sigmoid(gt_ref[:, D:2 * D].astype(F32)) * _dot(att_ref[...].astype(BF16), wa_ref[...])
              + _sigmoid(gt_ref[:, 2 * D:3 * D].astype(F32)) * _dot(rw.astype(BF16), ww_ref[...]))
    out = _dot(merged.astype(BF16), wo_ref[...])
    o_ref[...] = x_ref[...] + mod_ref[0, 5:6, :] * out


def _merge_call(x, mod, proj, oret, att, y, bonus, g, bo, gn, wr, wa, ww, wo, layer, *, ct, tm, tile0, n_tiles):
    tok = lambda w, col=0: pl.BlockSpec((tm, w), lambda i: (i + tile0, col))
    tok2 = lambda w: pl.BlockSpec((2, tm, w), lambda i: (0, i + tile0, 0))
    const = lambda i: (0, 0)
    wspec = lambda rows: pl.BlockSpec((None, rows, D), lambda i: (layer, 0, 0), pipeline_mode=pl.Buffered(1))
    assert C_GT % (3 * D) == 0
    return pl.pallas_call(
        _merge_kernel,
        grid=(n_tiles,),
        in_specs=[tok(D),
                  pl.BlockSpec((1, N_ADA, D), lambda i: (_mod_row((i + tile0) * tm, ct), 0, 0)),
                  tok2(1024), tok(1024, C_RG // 1024), tok(512), tok2(512), tok(512), tok(512),
                  tok(3 * D, C_GT // (3 * D)),
                  pl.BlockSpec((512, 512), const),
                  pl.BlockSpec((2, 512), const),
                  wspec(1024), wspec(512), wspec(512), wspec(D)],
        out_specs=pl.BlockSpec((tm, D), lambda i: (i, 0)),
        out_shape=jax.ShapeDtypeStruct((n_tiles * tm, D), F32),
        compiler_params=_params(("parallel",)),
        name="merge",
    )(x, mod, oret, proj, att, y, bonus, g, proj, bo, gn, wr, wa, ww, wo)


def _rope_1d(pos, dim, base):
    n_freq = dim // 2
    inv = np.power(np.float32(base), -(np.arange(n_freq, dtype=np.float32) / np.float32(n_freq))).astype(np.float32)
    return pos.astype(np.float32)[:, None] * inv[None, :]


def _tables():
    ang = _rope_1d(np.arange(N_CTX + N_LAT), HEAD64, 10000.0)
    rcos = np.tile(np.concatenate([np.cos(ang), np.cos(ang)], -1), (1, 8)).astype(np.float32)
    rsin = np.tile(np.concatenate([np.sin(ang), np.sin(ang)], -1), (1, 8)).astype(np.float32)
    rows = N_LAT // GRID_W
    row = np.repeat(np.arange(rows), GRID_W)
    col = np.arange(rows * GRID_W) % GRID_W
    aang = np.concatenate([_rope_1d(row, HEAD64 // 2, 10000.0), _rope_1d(col, HEAD64 // 2, 10000.0)], -1)
    ac = np.concatenate([np.cos(aang), np.cos(aang)], -1).astype(np.float32)
    asn = np.concatenate([np.sin(aang), np.sin(aang)], -1).astype(np.float32)
    qcos = np.concatenate([np.ones((N_CTX, 256), np.float32), np.tile(ac, (1, 4))], 0)
    qsin = np.concatenate([np.zeros((N_CTX, 256), np.float32), np.tile(asn, (1, 4))], 0)
    kcos = np.tile(ac, (1, 2))
    ksin = np.tile(asn, (1, 2))
    c = RET_CHUNK
    pos = np.arange(c, dtype=np.float32)
    diff = pos[:, None] - pos[None, :]
    dist = np.stack([np.where(diff >= 0, diff, -1.0), np.where(diff <= 0, -diff, -1.0)]).astype(np.float32)
    ze = np.stack([c - 1.0 - pos, pos]).astype(np.float32)
    xe = np.stack([pos + 1.0, c - pos]).astype(np.float32)
    ze = np.broadcast_to(ze[:, :, None], (2, c, LANES)).copy()
    xe = np.broadcast_to(xe[:, :, None], (2, c, LANES)).copy()
    t = RW_CHUNK
    ti = np.arange(t)
    low_incl = (ti[None, :] <= ti[:, None]).astype(np.float32)
    low_strict = (ti[None, :] < ti[:, None]).astype(np.float32)
    lcum = np.stack([low_incl, low_incl.T])
    eye2 = np.eye(2, dtype=np.float32)
    mbig = []
    for strict, incl in ((low_strict, low_incl), (low_strict.T, low_incl.T)):
        s2 = np.kron(eye2, strict)
        i2 = np.kron(eye2, incl)
        mbig.append(np.block([[s2, s2], [i2, i2]]))
    mbig = np.stack(mbig).astype(np.float32)
    lvl = []
    for j in range(int(math.log2(t))):
        sz = 2 ** j
        same = (ti[:, None] // (2 * sz)) == (ti[None, :] // (2 * sz))
        m = (same & ((ti[:, None] % (2 * sz)) >= sz) & ((ti[None, :] % (2 * sz)) < sz)).astype(np.float32)
        lvl.append(np.kron(eye2, m))
    lvl = np.stack(lvl)
    lvl = np.stack([lvl, lvl.transpose(0, 2, 1)]).astype(np.float32)
    eye = np.eye(LANES, dtype=np.float32)
    bo64 = np.kron(np.eye(8, dtype=np.float32), np.ones((HEAD64, HEAD64), np.float32))
    return dict(rcos=rcos, rsin=rsin, qcos=qcos, qsin=qsin, kcos=kcos, ksin=ksin, dist=dist, ze=ze, xe=xe,
                lcum=lcum, mbig=mbig, lvl=lvl, eye=eye, bo64=bo64)


def _rot_cols(n_heads):
    half = HEAD64 // 2
    idx, sgn = [], []
    for h in range(n_heads):
        base = h * HEAD64
        idx += list(range(base + half, base + HEAD64)) + list(range(base, base + half))
        sgn += [-1.0] * half + [1.0] * half
    return np.array(idx), np.array(sgn, np.float32)


def _proj_weight(w_in, v_down):
    depth = w_in.shape[0]
    o = np.cumsum([0, 512, 512, 1024, 1024, 512, 128, 128, 1920, 3072])
    rq, rk, rv, rg, aq, ak, av, rw, gt = (w_in[..., o[i]:o[i + 1]] for i in range(9))
    half = HEAD64 // 2
    sign = jnp.asarray([-1.0, 1.0], w_in.dtype).reshape(2, 1)

    def rot(m):
        heads = m.shape[-1] // HEAD64
        halves = m.reshape(depth, D, heads, 2, half)
        return (halves[:, :, :, ::-1, :] * sign).reshape(depth, D, heads * HEAD64)

    dup = lambda m: jnp.concatenate([m[..., 0:64], m[..., 0:64], m[..., 64:128], m[..., 64:128]], -1)
    z = lambda n, d=depth: jnp.zeros((d, D, n), w_in.dtype)
    vd = jnp.concatenate([z(32, 1), v_down.astype(w_in.dtype)], 0)
    w_rw = jnp.concatenate([rw, vd, z(96)], -1)
    w_rest = jnp.concatenate([rq, rot(rq), rk, rot(rk),
                              rv, rg,
                              aq, rot(aq),
                              dup(ak), dup(rot(ak)),
                              dup(av), z(256),
                              gt], -1)
    assert w_rw.shape == (depth, D, N_RW) and w_rest.shape == (depth, D, N_REST), (w_rw.shape, w_rest.shape)
    return w_rw, w_rest


def kernel(x, c, ctx, c_ctx, ada_w, ada_b, norm_w, ffn1_w_in, ffn1_w_out, ffn2_w_in, ffn2_w_out, mix_w_in, ret_decay_logit, att_q_norm, att_k_norm, att_sink, rwkv_shift, rwkv_w0, rwkv_w_up, rwkv_a0, rwkv_a_up, rwkv_g_up, rwkv_k_k, rwkv_k_a, rwkv_r_k, rwkv_v0, rwkv_v_down, rwkv_v_up, rwkv_gn_w, rwkv_gn_b, w_branch_ret, w_branch_att, w_branch_rwkv, w_out):
    bsz = x.shape[0]
    depth = ada_w.shape[0]
    ct = bsz * N_CTX
    nt = ct + bsz * N_LAT
    tm = math.gcd(1024, ct)
    ffn_tm = math.gcd(2048, ct)
    tb = {k: jnp.asarray(v) for k, v in _tables().items()}
    bo = tb["bo64"].astype(BF16)
    rcs = jnp.concatenate([tb["rcos"], tb["rsin"]], -1)
    stack_split = lambda w: jnp.stack(_split(w))

    xs = jnp.concatenate([ctx.reshape(ct, D), x.reshape(bsz * N_LAT, D)], 0)
    rows = 8 * ((bsz + 1 + 7) // 8)
    cond = jnp.zeros((rows, D), F32).at[0].set(c_ctx).at[1:bsz + 1].set(c)
    i2, _ = _rot_cols(2)
    vfirst = None
    f1_in, f1_out = ffn1_w_in.astype(BF16), ffn1_w_out.astype(BF16)
    f2_in, f2_out = ffn2_w_in.astype(BF16), ffn2_w_out.astype(BF16)
    w_rw, w_rest = _proj_weight(mix_w_in.astype(BF16), rwkv_v_down)
    wb_ret, wb_att = w_branch_ret.astype(BF16), w_branch_att.astype(BF16)
    wb_rw, wb_out = w_branch_rwkv.astype(BF16), w_out.astype(BF16)
    for l in range(depth):
        last = l == depth - 1
        mod = _ada_call(cond, ada_w, ada_b[l], l).reshape(rows, N_ADA, D)
        xs = _ffn_call(xs, mod, norm_w[l, 0], f1_in, f1_out, l, base=0, ct=ct, tm=ffn_tm, tile0=0,
                       n_tiles=nt // ffn_tm)

        proj_rw = _proj_call(xs, mod, norm_w[l, 1], w_rw, l, ct=ct, tm=tm, out_dtype=F32)
        proj = _proj_call(xs, mod, norm_w[l, 1], w_rest, l, ct=ct, tm=tm // 2, out_dtype=BF16)

        log_g = jax.nn.log_sigmoid(ret_decay_logit[l].astype(F32))
        oret = _ret_call(proj, log_g, rcs, tb["dist"], tb["ze"], tb["xe"], bsz=bsz)

        qw = jnp.tile(att_q_norm[l], 4).reshape(1, 256)
        qwr = jnp.tile(att_q_norm[l][i2[:64]], 4).reshape(1, 256)
        kw = jnp.tile(att_k_norm[l], 2).reshape(1, 128)
        kwr = jnp.tile(att_k_norm[l][i2[:64]], 2).reshape(1, 128)
        att = _att_call(proj, att_sink[l].astype(F32), tb["qcos"], tb["qsin"], tb["kcos"], tb["ksin"],
                        qw, qwr, kw, kwr, bo[:256, :256], bsz=bsz)

        shift = jnp.concatenate([rwkv_shift[l], jnp.tile(jnp.array([[0.0], [1.0], [0.0]], F32), (1, 128))], -1)
        wlo = jnp.zeros((2, LANES, 1024), F32)
        for d in range(2):
            wlo = wlo.at[0, d * 64:(d + 1) * 64, d * 512:(d + 1) * 512].set(rwkv_w_up[l, d])
            wlo = wlo.at[1, d * 64:(d + 1) * 64, d * 512:(d + 1) * 512].set(rwkv_a_up[l, d])
        if l > 0:
            vup = jnp.zeros((128, 512), F32).at[0:32].set(rwkv_v_up[l - 1])
            v0 = rwkv_v0[l - 1]
        else:
            vup = jnp.zeros((128, 512), F32)
            v0 = jnp.zeros((512,), F32)
        vec = jnp.stack([rwkv_k_k[l], rwkv_k_a[l], rwkv_r_k[l].reshape(512), v0,
                         rwkv_w0[l, 0], rwkv_w0[l, 1], rwkv_a0[l, 0], rwkv_a0[l, 1]])
        vf_in = vfirst if l > 0 else jnp.zeros((nt, 512), F32)
        r, v, kk, g, bonus, lw, kd, bv = _rw_prep_call(proj_rw, shift, stack_split(wlo), stack_split(rwkv_g_up[l]),
                                                        stack_split(vup), bo, vec, vf_in, ct=ct, mix_v=l > 0)
        if l == 0:
            vfirst = v
        y = _rw_scan_call(r, v, kk, lw, kd, bv, tb["lcum"], tb["mbig"], tb["lvl"], tb["eye"], bsz=bsz)

        gn = jnp.stack([rwkv_gn_w[l], rwkv_gn_b[l]])
        tile0 = ct // MERGE_TM if last else 0
        xs = _merge_call(xs, mod, proj, oret, att, y, bonus, g, bo, gn, wb_ret, wb_att, wb_rw, wb_out, l,
                         ct=ct, tm=MERGE_TM, tile0=tile0, n_tiles=nt // MERGE_TM - tile0)
        xs = _ffn_call(xs, mod, norm_w[l, 2], f2_in, f2_out, l, base=6, ct=0 if last else ct, tm=ffn_tm, tile0=0,
                       n_tiles=xs.shape[0] // ffn_tm)
    return xs.reshape(bsz, N_LAT, D)
```

```python
import functools
import math

import numpy as np
import jax
import jax.numpy as jnp
from jax import lax
from jax.experimental import pallas as pl
from jax.experimental.pallas import tpu as pltpu

F32 = jnp.float32
BF16 = jnp.bfloat16
HI = lax.Precision.HIGHEST

D = 1024
N_LAT = 2048
N_CTX = 256
GRID_W = 64
N_ADA = 9
D_FF = 2816
NORM_EPS = 1e-6
RET_CHUNK = 128
ATT_BLOCK = 128
WINDOW = 128
RW_CHUNK = 64
RWKV_GN_EPS = 64e-5
RWKV_DECAY_SCALE = 0.6065306597126334
HEAD64 = 64
LANES = 128

N_RW = 2048
C_RQ = 0
C_RV = 2048
C_RG = 3072
C_AQ = 4096
C_AK = 5120
C_AV = 5632
C_GT = 6144
N_REST = 9216

VMEM_LIMIT = 56 * 1024 * 1024


def _dot(a, b, prec=None):
    return jnp.dot(a, b, preferred_element_type=F32, precision=prec)


def _dot_nt(a, b, prec=None):
    return lax.dot_general(a, b, (((1,), (1,)), ((), ())), preferred_element_type=F32, precision=prec)


def _dot_tn(a, b, prec=None):
    return lax.dot_general(a, b, (((0,), (0,)), ((), ())), preferred_element_type=F32, precision=prec)


NN = (((1,), (0,)), ((), ()))
NT = (((1,), (1,)), ((), ()))


def _split(x, terms=2):
    out = []
    for _ in range(terms):
        piece = x.astype(BF16)
        out.append(piece)
        x = x - piece.astype(F32)
    return tuple(out)


def _mm(a, b, dims=NN):
    acc = None
    for i, ai in enumerate(a):
        for j, bj in enumerate(b):
            if i + j < max(len(a), len(b)):
                term = lax.dot_general(ai, bj, dims, preferred_element_type=F32)
                acc = term if acc is None else acc + term
    return acc


def _sigmoid(x):
    return 0.5 * jnp.tanh(0.5 * x) + 0.5


def _params(sem):
    return pltpu.CompilerParams(dimension_semantics=sem, vmem_limit_bytes=VMEM_LIMIT)


def _ada_kernel(c_ref, w_ref, b_ref, o_ref):
    c = c_ref[...]
    o_ref[...] = _dot(c * _sigmoid(c), w_ref[...], HI) + b_ref[...]


def _ada_call(cond, w, b, layer):
    rows = cond.shape[0]
    tn = 1024
    n = w.shape[2]
    return pl.pallas_call(
        _ada_kernel,
        grid=(n // tn,),
        in_specs=[pl.BlockSpec((rows, D), lambda j: (0, 0)),
                  pl.BlockSpec((None, D, tn), lambda j: (layer, 0, j)),
                  pl.BlockSpec((1, tn), lambda j: (0, j))],
        out_specs=pl.BlockSpec((rows, tn), lambda j: (0, j)),
        out_shape=jax.ShapeDtypeStruct((rows, n), F32),
        compiler_params=_params(("arbitrary",)),
        name="ada",
    )(cond, w, b.reshape(1, n))


def _mod_row(tok0, ct):
    return jnp.where(tok0 < ct, 0, 1 + (tok0 - ct) // N_LAT)


def _norm_mod(x, nw, shift, scale):
    y = x * lax.rsqrt(jnp.mean(x * x, -1, keepdims=True) + NORM_EPS) * nw
    return y * (1.0 + scale) + shift


def _ffn_kernel(x_ref, mod_ref, nw_ref, wg_ref, wu_ref, wo_ref, o_ref, h_sc, acc_sc, *, base, n_ff):
    j = pl.program_id(1)

    @pl.when(j == 0)
    def _():
        h = _norm_mod(x_ref[...], nw_ref[...], mod_ref[0, base:base + 1, :], mod_ref[0, base + 1:base + 2, :])
        h_sc[...] = h.astype(BF16)
        acc_sc[...] = jnp.zeros_like(acc_sc)

    h = h_sc[...]
    g = _dot(h, wg_ref[...].astype(BF16))
    u = _dot(h, wu_ref[...].astype(BF16))
    act = g * _sigmoid(g) * u
    acc_sc[...] += _dot(act.astype(BF16), wo_ref[...].astype(BF16))

    @pl.when(j == n_ff - 1)
    def _():
        o_ref[...] = x_ref[...] + 0.5 * mod_ref[0, base + 2:base + 3, :] * acc_sc[...]


def _ffn_call(x, mod, nw, w_in, w_out, layer, *, base, ct, tm, tile0, n_tiles):
    tf = 256
    n_ff = D_FF // tf
    nt = x.shape[0]
    kern = functools.partial(_ffn_kernel, base=base, n_ff=n_ff)
    return pl.pallas_call(
        kern,
        grid=(n_tiles, n_ff),
        in_specs=[pl.BlockSpec((tm, D), lambda i, j: (i + tile0, 0)),
                  pl.BlockSpec((1, N_ADA, D), lambda i, j: (_mod_row((i + tile0) * tm, ct), 0, 0)),
                  pl.BlockSpec((1, D), lambda i, j: (0, 0)),
                  pl.BlockSpec((None, D, tf), lambda i, j: (layer, 0, j)),
                  pl.BlockSpec((None, D, tf), lambda i, j: (layer, 0, j + n_ff)),
                  pl.BlockSpec((None, tf, D), lambda i, j: (layer, j, 0))],
        out_specs=pl.BlockSpec((tm, D), lambda i, j: (i, 0)),
        out_shape=jax.ShapeDtypeStruct((n_tiles * tm, D), F32),
        scratch_shapes=[pltpu.VMEM((tm, D), BF16), pltpu.VMEM((tm, D), F32)],
        compiler_params=_params(("parallel", "arbitrary")),
        name="ffn",
    )(x, mod, nw.reshape(1, D), w_in, w_in, w_out)


PROJ_TN = 512
MIX_OFF = np.cumsum([0, 512, 512, 1024, 1024, 512, 128, 128, 1920, 3072])
M_RQ, M_RK, M_RV, M_RG, M_AQ, M_AK, M_AV, M_RW, M_GT = (int(v) for v in MIX_OFF[:9])
X_RQ, X_RK, X_AQ, X_AK, X_AKR, X_AV, N_X = 0, 512, 1024, 1536, 1792, 2048, 2304
PLAN_REST = ((0, M_RQ, 512, C_RQ), (1, X_RQ, 512, C_RQ + 512), (0, M_RK, 512, C_RQ + 1024), (1, X_RK, 512, C_RQ + 1536),
             (0, M_RV, 1024, C_RV), (0, M_RG, 1024, C_RG),
             (0, M_AQ, 512, C_AQ), (1, X_AQ, 512, C_AQ + 512),
             (1, X_AK, 256, C_AK), (1, X_AKR, 256, C_AK + 256), (1, X_AV, 256, C_AV), (None, 0, 256, C_AV + 256),
             (0, M_GT, 3072, C_GT))
RW_TAIL = 128
PLAN_RW = ((0, 0, 1920 - RW_TAIL, 0), (1, 0, 2 * LANES, 1920 - RW_TAIL))


def _proj_kernel(x_ref, mod_ref, nw_ref, wa_ref, wb_ref, o_ref, *, plan):
    h = _norm_mod(x_ref[...], nw_ref[...], mod_ref[0, 3:4, :], mod_ref[0, 4:5, :]).astype(BF16)
    for which, src, width, dst in plan:
        for c in range(0, width, PROJ_TN):
            w = min(PROJ_TN, width - c)
            if which is None:
                o_ref[:, dst + c:dst + c + w] = jnp.zeros((x_ref.shape[0], w), o_ref.dtype)
            else:
                w_ref = (wa_ref, wb_ref)[which]
                o_ref[:, dst + c:dst + c + w] = _dot(h, w_ref[:, src + c:src + c + w]).astype(o_ref.dtype)


def _proj_call(x, mod, nw, wa, wa_block, wb, layer, *, plan, n_cols, ct, tm, out_dtype):
    nt = x.shape[0]
    resident = lambda width, col: pl.BlockSpec((None, D, width), lambda i: (layer, 0, col),
                                               pipeline_mode=pl.Buffered(1))
    return pl.pallas_call(
        functools.partial(_proj_kernel, plan=plan),
        grid=(nt // tm,),
        in_specs=[pl.BlockSpec((tm, D), lambda i: (i, 0)),
                  pl.BlockSpec((1, N_ADA, D), lambda i: (_mod_row(i * tm, ct), 0, 0)),
                  pl.BlockSpec((1, D), lambda i: (0, 0)),
                  resident(*wa_block), resident(wb.shape[2], 0)],
        out_specs=pl.BlockSpec((tm, n_cols), lambda i: (i, 0)),
        out_shape=jax.ShapeDtypeStruct((nt, n_cols), out_dtype),
        compiler_params=_params(("parallel",)),
        name="proj",
    )(x, mod, nw.reshape(1, D), wa, wb)


def _scan_blocks(b, d, s, *, nc_ctx, nc_lat, bsz):
    in_ctx = s < nc_ctx
    cs = jnp.where(d == 0, s, nc_ctx - 1 - s)
    ls = jnp.where(d == 0, s - nc_ctx, nc_lat - 1 - (s - nc_ctx))
    tok = jnp.where(in_ctx, b * nc_ctx + cs, bsz * nc_ctx + b * nc_lat + ls)
    pos = jnp.where(in_ctx, cs, nc_ctx + ls)
    return tok, pos


RET_STEP_CHUNKS = 2


def _ret_kernel(lg_ref, qkv_ref, cs_ref, dist_ref, ze_ref, xe_ref, o_ref, st_sc, dec_sc, xi_sc, zeta_sc):
    d = pl.program_id(1)
    s = pl.program_id(2)

    heads = range(8)

    @pl.when(s == 0)
    def _():
        st_sc[...] = jnp.zeros_like(st_sc)
        dist = dist_ref[...]
        for h in heads:
            lg = lg_ref[d, h]
            dec_sc[h] = jnp.where(dist >= 0.0, jnp.exp(lg * dist), 0.0)
            xi_sc[h] = jnp.exp(lg * xe_ref[...])
            zeta_sc[h] = jnp.exp(lg * ze_ref[...])

    c = RET_CHUNK
    lane = lax.broadcasted_iota(jnp.int32, (1, LANES), 1)
    hmask = [((lane // HEAD64) == e).astype(F32) for e in range(2)]
    starts = [pl.multiple_of(jnp.where(d == 0, i, RET_STEP_CHUNKS - 1 - i) * c, c) for i in range(RET_STEP_CHUNKS)]
    qms, kps, vhs = [], [], []
    for st0 in starts:
        rows = pl.ds(st0, c)
        cos = cs_ref[rows, 0:512]
        sin = cs_ref[rows, 512:1024]
        q = qkv_ref[rows, 0:512] * cos + qkv_ref[rows, 512:1024] * sin
        k = (qkv_ref[rows, 1024:1536] * cos + qkv_ref[rows, 1536:2048] * sin) * (HEAD64 ** -0.5)
        kps.append([k[:, p * LANES:(p + 1) * LANES].astype(BF16) for p in range(4)])
        qms.append([(q[:, (h // 2) * LANES:(h // 2 + 1) * LANES] * hmask[h % 2]).astype(BF16) for h in heads])
        vhs.append([qkv_ref[rows, C_RV + h * LANES:C_RV + (h + 1) * LANES] for h in heads])
    chunks = range(RET_STEP_CHUNKS)
    scs = [[(_dot_nt(qms[i][h], kps[i][h // 2]) * dec_sc[h]).astype(BF16) for h in heads] for i in chunks]
    intra = [[_dot(scs[i][h], vhs[i][h].astype(BF16)) for h in heads] for i in chunks]
    us = [[_dot_tn(kps[i][h // 2], (vhs[i][h] * zeta_sc[h]).astype(BF16)) for h in heads] for i in chunks]
    chunk_len = jnp.full((1, LANES), float(c), F32)
    decay = [jnp.exp(lg_ref[d, h] * chunk_len) for h in heads]
    sts = [st_sc[h] for h in heads]
    for i in chunks:
        inter = [_dot(qms[i][h], sts[h].astype(BF16)) * xi_sc[h] for h in heads]
        for h in heads:
            o_ref[pl.ds(starts[i], c), h * LANES:(h + 1) * LANES] = (intra[i][h] + inter[h]).astype(o_ref.dtype)
        sts = [sts[h] * decay[h] + us[i][h] for h in heads]
    for h in heads:
        st_sc[h] = sts[h]


def _ret_call(proj, log_g, cos_sin, dist, ze, xe, *, bsz):
    nt = proj.shape[0]
    c = RET_CHUNK
    rows = RET_STEP_CHUNKS * c
    nc_ctx, nc_lat = N_CTX // rows, N_LAT // rows
    steps = nc_ctx + nc_lat
    blocks = functools.partial(_scan_blocks, nc_ctx=nc_ctx, nc_lat=nc_lat, bsz=bsz)
    assert C_RQ == 0 and C_RV == 2048 and C_RG == 3072
    pos_map = lambda b, d, s, lg: (blocks(b, d, s)[1], 0)
    dir_map = lambda b, d, s, lg: (d, 0, 0)
    grid_spec = pltpu.PrefetchScalarGridSpec(
        num_scalar_prefetch=1,
        grid=(bsz, 2, steps),
        in_specs=[pl.BlockSpec((rows, C_RG), lambda b, d, s, lg: (blocks(b, d, s)[0], 0)),
                  pl.BlockSpec((rows, 1024), pos_map),
                  pl.BlockSpec((None, c, c), dir_map),
                  pl.BlockSpec((None, c, LANES), dir_map),
                  pl.BlockSpec((None, c, LANES), dir_map)],
        out_specs=pl.BlockSpec((None, rows, 1024), lambda b, d, s, lg: (d, blocks(b, d, s)[0], 0)),
        scratch_shapes=[pltpu.VMEM((8, LANES, LANES), F32), pltpu.VMEM((8, c, c), F32),
                        pltpu.VMEM((8, c, LANES), F32), pltpu.VMEM((8, c, LANES), F32)],
    )
    return pl.pallas_call(
        _ret_kernel,
        grid_spec=grid_spec,
        out_shape=jax.ShapeDtypeStruct((2, nt, 1024), BF16),
        compiler_params=_params(("parallel", "parallel", "arbitrary")),
        name="retention",
    )(log_g, proj, cos_sin, dist, ze, xe)


NEG = -1e30


def _att_kernel(sink_ref, q_ref, qr_ref, qcos_ref, qsin_ref, qw_ref, qwr_ref, bo_ref,
                kl_ref, klr_ref, vl_ref, kcos_ref, ksin_ref, kc_ref, vc_ref, kw_ref, kwr_ref,
                o_ref, kn_sc, kcn_sc, *, n_ctx_blk):
    hk = pl.program_id(1)
    jb = pl.program_id(2)
    blk = ATT_BLOCK

    @pl.when(jb == 0)
    def _():
        kl = kl_ref[...].astype(F32)
        rs = lax.rsqrt(jnp.sum(kl * kl, -1, keepdims=True) * (0.5 / HEAD64) + NORM_EPS)
        kn = ((kl * kw_ref[...]) * kcos_ref[...] + (klr_ref[...].astype(F32) * kwr_ref[...]) * ksin_ref[...]) * rs
        kn_sc[...] = kn.astype(BF16)
        kc = kc_ref[...].astype(F32)
        rc = lax.rsqrt(jnp.sum(kc * kc, -1, keepdims=True) * (0.5 / HEAD64) + NORM_EPS)
        kcn_sc[...] = (kc * kw_ref[...] * rc).astype(BF16)

    q = q_ref[...].astype(F32)
    mean_sq = _mm(_split(q * q), (bo_ref[...],)) * (1.0 / HEAD64)
    rs = lax.rsqrt(mean_sq + NORM_EPS)
    qn = ((q * qw_ref[...]) * qcos_ref[...]
          + (qr_ref[...].astype(F32) * qwr_ref[...]) * qsin_ref[...]) * rs * (HEAD64 ** -0.5)

    reach = jnp.where(jb >= n_ctx_blk, WINDOW, -1)
    lb = jnp.maximum(jb - n_ctx_blk, 0)
    n_win = 3 * blk
    start = jnp.clip((lb - 1) * blk, 0, N_LAT - n_win)
    start = pl.multiple_of(start, blk)
    n_keys = n_win + N_CTX
    kall = jnp.concatenate([kn_sc[pl.ds(start, n_win), :], kcn_sc[...]], axis=0)
    vall = jnp.concatenate([vl_ref[pl.ds(start, n_win), :], vc_ref[...]], axis=0)
    qpos = lb * blk + lax.broadcasted_iota(jnp.int32, (blk, n_keys), 0)
    col = lax.broadcasted_iota(jnp.int32, (blk, n_keys), 1)
    valid = (col >= n_win) | (jnp.abs(start + col - qpos) <= reach)
    lane = lax.broadcasted_iota(jnp.int32, (1, LANES), 1)
    hmask = [((lane // HEAD64) == e).astype(F32) for e in range(2)]
    groups = range(4)
    sinks = [sink_ref[hk * 4 + g] for g in groups]
    qgs = [(qn[:, (g // 2) * LANES:(g // 2 + 1) * LANES] * hmask[g % 2]).astype(BF16) for g in groups]
    ss = [jnp.where(valid, _dot_nt(qgs[g], kall), NEG) for g in groups]
    ms = [jnp.maximum(jnp.max(ss[g], -1, keepdims=True), sinks[g]) for g in groups]
    ps = [jnp.exp(ss[g] - ms[g]) for g in groups]
    dens = [jnp.sum(ps[g], -1, keepdims=True) + jnp.exp(sinks[g] - ms[g]) for g in groups]
    ogs = [_dot(ps[g].astype(BF16), vall) / dens[g] * hmask[g % 2] for g in groups]
    o_ref[...] = jnp.concatenate([ogs[0] + ogs[1], ogs[2] + ogs[3]], axis=-1).astype(o_ref.dtype)


def _att_call(proj, sink, qcos, qsin, kcos, ksin, qw, qwr, kw, kwr, bo, *, bsz):
    nt = proj.shape[0]
    blk = ATT_BLOCK
    n_ctx_blk, n_lat_blk = N_CTX // blk, N_LAT // blk
    steps = n_ctx_blk + n_lat_blk

    def qtok(b, jb):
        return jnp.where(jb < n_ctx_blk, b * n_ctx_blk + jb, bsz * n_ctx_blk + b * n_lat_blk + (jb - n_ctx_blk))

    def qmap(col0):
        return lambda b, hk, jb, sk: (qtok(b, jb), col0 + hk)

    qpos_map = lambda b, hk, jb, sk: (jb, 0)
    const = lambda b, hk, jb, sk: (0, 0)
    ct_rows = bsz * N_CTX

    def lat_map(col0):
        return lambda b, hk, jb, sk: (ct_rows // N_LAT + b, col0 + hk)

    def ctx_map(col0):
        return lambda b, hk, jb, sk: (b, col0 + hk)

    grid_spec = pltpu.PrefetchScalarGridSpec(
        num_scalar_prefetch=1,
        grid=(bsz, 2, steps),
        in_specs=[pl.BlockSpec((blk, 256), qmap(C_AQ // 256)),
                  pl.BlockSpec((blk, 256), qmap(C_AQ // 256 + 2)),
                  pl.BlockSpec((blk, 256), qpos_map),
                  pl.BlockSpec((blk, 256), qpos_map),
                  pl.BlockSpec((1, 256), const),
                  pl.BlockSpec((1, 256), const),
                  pl.BlockSpec((256, 256), const),
                  pl.BlockSpec((N_LAT, LANES), lat_map(C_AK // LANES)),
                  pl.BlockSpec((N_LAT, LANES), lat_map(C_AK // LANES + 2)),
                  pl.BlockSpec((N_LAT, LANES), lat_map(C_AV // LANES)),
                  pl.BlockSpec((N_LAT, LANES), const),
                  pl.BlockSpec((N_LAT, LANES), const),
                  pl.BlockSpec((N_CTX, LANES), ctx_map(C_AK // LANES)),
                  pl.BlockSpec((N_CTX, LANES), ctx_map(C_AV // LANES)),
                  pl.BlockSpec((1, LANES), const),
                  pl.BlockSpec((1, LANES), const)],
        out_specs=pl.BlockSpec((blk, 256), lambda b, hk, jb, sk: (qtok(b, jb), hk)),
        scratch_shapes=[pltpu.VMEM((N_LAT, LANES), BF16), pltpu.VMEM((N_CTX, LANES), BF16)],
    )
    kern = functools.partial(_att_kernel, n_ctx_blk=n_ctx_blk)
    return pl.pallas_call(
        kern,
        grid_spec=grid_spec,
        out_shape=jax.ShapeDtypeStruct((nt, 512), BF16),
        compiler_params=_params(("parallel", "parallel", "arbitrary")),
        name="attention",
    )(sink, proj, proj, qcos, qsin, qw, qwr, bo, proj, proj, proj, kcos, ksin, proj, proj, kw, kwr)


RW_TM = 256


def _rw_prep_kernel(cur_ref, prev_ref, next_ref, shift_ref, wlo_ref, gup_ref, vup_ref, bo_ref, vec_ref, vf_ref,
                    r_o, v_o, kk_o, g_o, bonus_o, lw_o, kd_o, bv_o, *, ct, mix_v):
    i = pl.program_id(0)
    tm = RW_TM
    tok0 = i * tm
    lat_off = tok0 - ct
    is_start = jnp.where(tok0 < ct, tok0 % N_CTX == 0, lat_off % N_LAT == 0)
    is_end = jnp.where(tok0 < ct, (tok0 + tm) % N_CTX == 0, (lat_off + tm) % N_LAT == 0)
    cur = cur_ref[...]
    row = lax.broadcasted_iota(jnp.int32, (tm, 1), 0)
    prev_row = jnp.where(is_start, 0.0, prev_ref[7:8, :])
    next_row = jnp.where(is_end, 0.0, next_ref[0:1, :])
    prev = jnp.where(row == 0, prev_row, pltpu.roll(cur, 1, 0))
    nxt = jnp.where(row == tm - 1, next_row, pltpu.roll(cur, tm - 1, 0))
    c = prev * shift_ref[0:1, :] + cur * shift_ref[1:2, :] + nxt * shift_ref[2:3, :]

    r = c[:, 0:512]
    k = c[:, 512:1024]
    v = c[:, 1024:1536]
    lo = c[:, 1536:1792]
    gd = c[:, 1792:1920]
    vl = c[:, 1920:2048]

    vec = vec_ref[...]
    k_k, k_a, r_k, v0 = vec[0:1], vec[1:2], vec[2:3], vec[3:4]
    bo = (bo_ref[...],)
    if mix_v:
        v = v + (vf_ref[...] - v) * _sigmoid(v0 + _mm(_split(vl), (vup_ref[0], vup_ref[1])))
    g = _mm(_split(_sigmoid(gd)), (gup_ref[0], gup_ref[1]))
    kk = k * k_k
    ss = _mm(_split(kk * kk), bo)
    kk = kk / jnp.maximum(jnp.sqrt(ss), 1e-12)
    z_lora = _mm(_split(jnp.tanh(lo[:, 0:LANES])), (wlo_ref[0, 0], wlo_ref[1, 0]))
    a_lora = _mm(_split(lo[:, LANES:2 * LANES]), (wlo_ref[0, 1], wlo_ref[1, 1]))
    ksum = jnp.zeros_like(k)
    for d in range(2):
        z = vec[4 + d:5 + d] + z_lora[:, d * 512:(d + 1) * 512]
        a = _sigmoid(vec[6 + d:7 + d] + a_lora[:, d * 512:(d + 1) * 512])
        kd = k * (1.0 + (a - 1.0) * k_a)
        lw_o[d] = -RWKV_DECAY_SCALE * _sigmoid(z)
        kd_o[d] = kd
        bv_o[d] = a * kk
        ksum = ksum + kd
    bonus_o[...] = _mm(_split(r * ksum * r_k), bo) * v
    r_o[...] = r
    v_o[...] = v
    kk_o[...] = kk
    g_o[...] = g


def _rw_prep_call(proj, shift, wlo, gup, vup, bo, vec, vfirst, *, ct, mix_v):
    nt = proj.shape[0]
    tm = RW_TM
    nblk8 = nt // 8
    tok = pl.BlockSpec((tm, 512), lambda i: (i, 0))
    dirs = pl.BlockSpec((2, tm, 512), lambda i: (0, i, 0))
    const = lambda i: (0, 0)
    kern = functools.partial(_rw_prep_kernel, ct=ct, mix_v=mix_v)
    one = jax.ShapeDtypeStruct((nt, 512), F32)
    two = jax.ShapeDtypeStruct((2, nt, 512), F32)
    return pl.pallas_call(
        kern,
        grid=(nt // tm,),
        in_specs=[pl.BlockSpec((tm, N_RW), lambda i: (i, 0)),
                  pl.BlockSpec((8, N_RW), lambda i: (jnp.maximum(i * (tm // 8) - 1, 0), 0)),
                  pl.BlockSpec((8, N_RW), lambda i: (jnp.minimum((i + 1) * (tm // 8), nblk8 - 1), 0)),
                  pl.BlockSpec((3, N_RW), const),
                  pl.BlockSpec((2, 2, LANES, 1024), lambda i: (0, 0, 0, 0)),
                  pl.BlockSpec((2, 128, 512), lambda i: (0, 0, 0)),
                  pl.BlockSpec((2, 128, 512), lambda i: (0, 0, 0)),
                  pl.BlockSpec((512, 512), const),
                  pl.BlockSpec((8, 512), const),
                  tok],
        out_specs=[tok, tok, tok, tok, tok, dirs, dirs, dirs],
        out_shape=[one, one, one, one, one, two, two, two],
        compiler_params=_params(("parallel",)),
        name="rwkv_prep",
    )(proj, proj, proj, shift, wlo, gup, vup, bo, vec, vfirst)


SCAN_TERMS = dict(v=1, x=1, y=1, ak=1, ao=1, m=1, rhs=1, pq=1, rb=1, rk=1, bp=1, kp=1, st=1, yq=1, g=1)
RW_STEP_CHUNKS = 4


def _rw_scan_kernel(r_ref, v_ref, kk_ref, lw_ref, kd_ref, bv_ref, lcum_ref, mbig_ref, lvl_ref, eye_ref,
                    y_ref, st_sc):
    d = pl.program_id(1)
    s = pl.program_id(2)
    t = RW_CHUNK
    nt_ = SCAN_TERMS

    @pl.when(s == 0)
    def _():
        st_sc[...] = jnp.zeros_like(st_sc)

    lcum = (lcum_ref[...].astype(BF16),)
    mbig = mbig_ref[...]
    eye = eye_ref[...]
    lane = lax.broadcasted_iota(jnp.int32, (1, LANES), 1)
    m_e = (lane < HEAD64).astype(F32)
    m_o = 1.0 - m_e

    def stack2(x):
        return jnp.concatenate([x * m_e, x * m_o], axis=0)

    pairs = range(4)
    sls = [slice(p * LANES, (p + 1) * LANES) for p in pairs]
    starts = [pl.multiple_of(jnp.where(d == 0, c, RW_STEP_CHUNKS - 1 - c) * t, t) for c in range(RW_STEP_CHUNKS)]
    aw2, rt2, bi2, ki2, bp2t, kp2t, v2s, w_tot = [], [], [], [], [], [], [], []
    for st0 in starts:
        rows = pl.ds(st0, t)
        lw = lw_ref[rows, :]
        cum = _mm(lcum, _split(lw, 3))
        tot = jnp.sum(lw, axis=0, keepdims=True)
        kk, kd, bv = kk_ref[rows, :], kd_ref[rows, :], bv_ref[rows, :]
        aw = -kk * jnp.exp(cum - lw)
        rt = r_ref[rows, :] * jnp.exp(cum)
        e_inv = jnp.exp(-cum)
        e_rem = jnp.exp(tot - cum)
        v = v_ref[rows, :]
        aw2 += [stack2(aw[:, sl]) for sl in sls]
        rt2 += [stack2(rt[:, sl]) for sl in sls]
        bi2 += [stack2((bv * e_inv)[:, sl]) for sl in sls]
        ki2 += [stack2((kd * e_inv)[:, sl]) for sl in sls]
        bp2t += [stack2((bv * e_rem)[:, sl]).T for sl in sls]
        kp2t += [stack2((kd * e_rem)[:, sl]).T for sl in sls]
        v2s += [_split(stack2(v[:, sl]), nt_["v"]) for sl in sls]
        w_tot += [jnp.exp(tot)[:, sl] for sl in sls]

    items = range(RW_STEP_CHUNKS * 4)
    big = [_mm(_split(jnp.concatenate([aw2[i], rt2[i]], axis=0), nt_["x"]),
               _split(jnp.concatenate([bi2[i], ki2[i]], axis=0), nt_["y"]), NT) * mbig for i in items]
    a_ab = [b[0:2 * t, 0:2 * t] for b in big]
    a_ak = [b[0:2 * t, 2 * t:4 * t] for b in big]
    a_rb = [b[2 * t:4 * t, 0:2 * t] for b in big]
    a_rk = [b[2 * t:4 * t, 2 * t:4 * t] for b in big]
    akv = [_mm(_split(a_ak[i], nt_["ak"]), v2s[i]) for i in items]
    inv = [eye + a * lvl_ref[0] for a in a_ab]
    for j in range(1, int(math.log2(t))):
        ms = [_split(m, nt_["m"]) for m in inv]
        low = [_split(_mm(_split(a_ab[i] * lvl_ref[j], nt_["ao"]), ms[i]), nt_["m"]) for i in items]
        inv = [inv[i] + _mm(ms[i], low[i]) for i in items]
    pqs = [_split(_mm(_split(inv[i], nt_["m"]),
                      _split(jnp.concatenate([aw2[i], akv[i]], axis=1), nt_["rhs"])), nt_["pq"]) for i in items]
    tmp = [_mm(_split(a_rb[i], nt_["rb"]), pqs[i]) for i in items]
    ark_v = [_mm(_split(a_rk[i], nt_["rk"]), v2s[i]) for i in items]
    gu = [_mm(_split(bp2t[i], nt_["bp"]), pqs[i]) for i in items]
    kpv = [_mm(_split(kp2t[i], nt_["kp"]), v2s[i]) for i in items]
    yq = [_split(rt2[i] + tmp[i][:, 0:LANES], nt_["yq"]) for i in items]
    y0 = [tmp[i][:, LANES:2 * LANES] + ark_v[i] for i in items]
    g_mat = [_split(gu[i][:, 0:LANES] + eye * w_tot[i], nt_["g"]) for i in items]
    u_mat = [gu[i][:, LANES:2 * LANES] + kpv[i] for i in items]

    st = [st_sc[p] for p in pairs]
    for c in range(RW_STEP_CHUNKS):
        sts = [_split(st[p], nt_["st"]) for p in pairs]
        y2 = [_mm(yq[4 * c + p], sts[p]) + y0[4 * c + p] for p in pairs]
        st = [_mm(g_mat[4 * c + p], sts[p]) + u_mat[4 * c + p] for p in pairs]
        for p in pairs:
            y_ref[pl.ds(starts[c], t), sls[p]] = y2[p][0:t] + y2[p][t:2 * t]
    for p in pairs:
        st_sc[p] = st[p]


def _rw_scan_call(r, v, kk, lw, kd, bv, lcum, mbig, lvl, eye, *, bsz):
    nt = r.shape[0]
    t = RW_CHUNK
    rows = RW_STEP_CHUNKS * t
    nc_ctx, nc_lat = N_CTX // rows, N_LAT // rows
    steps = nc_ctx + nc_lat
    blocks = functools.partial(_scan_blocks, nc_ctx=nc_ctx, nc_lat=nc_lat, bsz=bsz)
    tok = pl.BlockSpec((rows, 512), lambda b, d, s: (blocks(b, d, s)[0], 0))
    tokd = pl.BlockSpec((None, rows, 512), lambda b, d, s: (d, blocks(b, d, s)[0], 0))
    n_lvl = lvl.shape[1]
    return pl.pallas_call(
        _rw_scan_kernel,
        grid=(bsz, 2, steps),
        in_specs=[tok, tok, tok, tokd, tokd, tokd,
                  pl.BlockSpec((None, t, t), lambda b, d, s: (d, 0, 0)),
                  pl.BlockSpec((None, 4 * t, 4 * t), lambda b, d, s: (d, 0, 0)),
                  pl.BlockSpec((None, n_lvl, 2 * t, 2 * t), lambda b, d, s: (d, 0, 0, 0)),
                  pl.BlockSpec((LANES, LANES), lambda b, d, s: (0, 0))],
        out_specs=tokd,
        out_shape=jax.ShapeDtypeStruct((2, nt, 512), F32),
        scratch_shapes=[pltpu.VMEM((4, LANES, LANES), F32)],
        compiler_params=_params(("parallel", "parallel", "arbitrary")),
        name="rwkv_scan",
    )(r, v, kk, lw, kd, bv, lcum, mbig, lvl, eye)


MERGE_TM = 512


def _merge_kernel(x_ref, mod_ref, oret_ref, rg_ref, att_ref, y_ref, bonus_ref, g_ref, gt_ref,
                  bo_ref, gn_ref, wr_ref, wa_ref, ww_ref, wo_ref, o_ref):
    o = oret_ref[0].astype(F32) + oret_ref[1].astype(F32)
    parts = []
    for h in range(8):
        oh = o[:, h * LANES:(h + 1) * LANES]
        mu = jnp.mean(oh, -1, keepdims=True)
        dv = oh - mu
        var = jnp.mean(dv * dv, -1, keepdims=True)
        parts.append(dv * lax.rsqrt(var + NORM_EPS))
    rg = rg_ref[...].astype(F32)
    ret = (rg * _sigmoid(rg)) * jnp.concatenate(parts, axis=-1)

    y = y_ref[0] + y_ref[1]
    bo = (bo_ref[...],)
    mu = _mm(_split(y), bo) * (1.0 / HEAD64)
    dy = y - mu
    var = _mm(_split(dy * dy), bo) * (1.0 / HEAD64)
    yn = dy * lax.rsqrt(var + RWKV_GN_EPS) * gn_ref[0:1, :] + gn_ref[1:2, :]
    rw = (yn + bonus_ref[...]) * g_ref[...]

    merged = (_sigmoid(gt_ref[:, 0:D].astype(F32)) * _dot(ret.astype(BF16), wr_ref[...])
              + _sigmoid(gt_ref[:, D:2 * D].astype(F32)) * _dot(att_ref[...].astype(BF16), wa_ref[...])
              + _sigmoid(gt_ref[:, 2 * D:3 * D].astype(F32)) * _dot(rw.astype(BF16), ww_ref[...]))
    out = _dot(merged.astype(BF16), wo_ref[...])
    o_ref[...] = x_ref[...] + mod_ref[0, 5:6, :] * out


def _merge_call(x, mod, proj, oret, att, y, bonus, g, bo, gn, wr, wa, ww, wo, layer, *, ct, tm, tile0, n_tiles):
    tok = lambda w, col=0: pl.BlockSpec((tm, w), lambda i: (i + tile0, col))
    tok2 = lambda w: pl.BlockSpec((2, tm, w), lambda i: (0, i + tile0, 0))
    const = lambda i: (0, 0)
    wspec = lambda rows: pl.BlockSpec((None, rows, D), lambda i: (layer, 0, 0), pipeline_mode=pl.Buffered(1))
    assert C_GT % (3 * D) == 0
    return pl.pallas_call(
        _merge_kernel,
        grid=(n_tiles,),
        in_specs=[tok(D),
                  pl.BlockSpec((1, N_ADA, D), lambda i: (_mod_row((i + tile0) * tm, ct), 0, 0)),
                  tok2(1024), tok(1024, C_RG // 1024), tok(512), tok2(512), tok(512), tok(512),
                  tok(3 * D, C_GT // (3 * D)),
                  pl.BlockSpec((512, 512), const),
                  pl.BlockSpec((2, 512), const),
                  wspec(1024), wspec(512), wspec(512), wspec(D)],
        out_specs=pl.BlockSpec((tm, D), lambda i: (i, 0)),
        out_shape=jax.ShapeDtypeStruct((n_tiles * tm, D), F32),
        compiler_params=_params(("parallel",)),
        name="merge",
    )(x, mod, oret, proj, att, y, bonus, g, proj, bo, gn, wr, wa, ww, wo)


def _rope_1d(pos, dim, base):
    n_freq = dim // 2
    inv = np.power(np.float32(base), -(np.arange(n_freq, dtype=np.float32) / np.float32(n_freq))).astype(np.float32)
    return pos.astype(np.float32)[:, None] * inv[None, :]


def _tables():
    ang = _rope_1d(np.arange(N_CTX + N_LAT), HEAD64, 10000.0)
    rcos = np.tile(np.concatenate([np.cos(ang), np.cos(ang)], -1), (1, 8)).astype(np.float32)
    rsin = np.tile(np.concatenate([np.sin(ang), np.sin(ang)], -1), (1, 8)).astype(np.float32)
    rows = N_LAT // GRID_W
    row = np.repeat(np.arange(rows), GRID_W)
    col = np.arange(rows * GRID_W) % GRID_W
    aang = np.concatenate([_rope_1d(row, HEAD64 // 2, 10000.0), _rope_1d(col, HEAD64 // 2, 10000.0)], -1)
    ac = np.concatenate([np.cos(aang), np.cos(aang)], -1).astype(np.float32)
    asn = np.concatenate([np.sin(aang), np.sin(aang)], -1).astype(np.float32)
    qcos = np.concatenate([np.ones((N_CTX, 256), np.float32), np.tile(ac, (1, 4))], 0)
    qsin = np.concatenate([np.zeros((N_CTX, 256), np.float32), np.tile(asn, (1, 4))], 0)
    kcos = np.tile(ac, (1, 2))
    ksin = np.tile(asn, (1, 2))
    c = RET_CHUNK
    pos = np.arange(c, dtype=np.float32)
    diff = pos[:, None] - pos[None, :]
    dist = np.stack([np.where(diff >= 0, diff, -1.0), np.where(diff <= 0, -diff, -1.0)]).astype(np.float32)
    ze = np.stack([c - 1.0 - pos, pos]).astype(np.float32)
    xe = np.stack([pos + 1.0, c - pos]).astype(np.float32)
    ze = np.broadcast_to(ze[:, :, None], (2, c, LANES)).copy()
    xe = np.broadcast_to(xe[:, :, None], (2, c, LANES)).copy()
    t = RW_CHUNK
    ti = np.arange(t)
    low_incl = (ti[None, :] <= ti[:, None]).astype(np.float32)
    low_strict = (ti[None, :] < ti[:, None]).astype(np.float32)
    lcum = np.stack([low_incl, low_incl.T])
    eye2 = np.eye(2, dtype=np.float32)
    mbig = []
    for strict, incl in ((low_strict, low_incl), (low_strict.T, low_incl.T)):
        s2 = np.kron(eye2, strict)
        i2 = np.kron(eye2, incl)
        mbig.append(np.block([[s2, s2], [i2, i2]]))
    mbig = np.stack(mbig).astype(np.float32)
    lvl = []
    for j in range(int(math.log2(t))):
        sz = 2 ** j
        same = (ti[:, None] // (2 * sz)) == (ti[None, :] // (2 * sz))
        m = (same & ((ti[:, None] % (2 * sz)) >= sz) & ((ti[None, :] % (2 * sz)) < sz)).astype(np.float32)
        lvl.append(np.kron(eye2, m))
    lvl = np.stack(lvl)
    lvl = np.stack([lvl, lvl.transpose(0, 2, 1)]).astype(np.float32)
    eye = np.eye(LANES, dtype=np.float32)
    bo64 = np.kron(np.eye(8, dtype=np.float32), np.ones((HEAD64, HEAD64), np.float32))
    return dict(rcos=rcos, rsin=rsin, qcos=qcos, qsin=qsin, kcos=kcos, ksin=ksin, dist=dist, ze=ze, xe=xe,
                lcum=lcum, mbig=mbig, lvl=lvl, eye=eye, bo64=bo64)


def _rot_cols(n_heads):
    half = HEAD64 // 2
    idx, sgn = [], []
    for h in range(n_heads):
        base = h * HEAD64
        idx += list(range(base + half, base + HEAD64)) + list(range(base, base + half))
        sgn += [-1.0] * half + [1.0] * half
    return np.array(idx), np.array(sgn, np.float32)


def _derived_weights(w_in, v_down):
    depth = w_in.shape[0]
    rq, rk = w_in[..., M_RQ:M_RK], w_in[..., M_RK:M_RV]
    aq, ak, av = w_in[..., M_AQ:M_AK], w_in[..., M_AK:M_AV], w_in[..., M_AV:M_RW]
    half = HEAD64 // 2
    sign = jnp.asarray([-1.0, 1.0], w_in.dtype).reshape(2, 1)

    def rot(m):
        heads = m.shape[-1] // HEAD64
        halves = m.reshape(depth, D, heads, 2, half)
        return (halves[:, :, :, ::-1, :] * sign).reshape(depth, D, heads * HEAD64)

    dup = lambda m: jnp.concatenate([m[..., 0:64], m[..., 0:64], m[..., 64:128], m[..., 64:128]], -1)
    z = lambda n, d=depth: jnp.zeros((d, D, n), w_in.dtype)
    w_vd = jnp.concatenate([w_in[..., M_GT - RW_TAIL:M_GT],
                            jnp.concatenate([z(32, 1), v_down.astype(w_in.dtype)], 0), z(96)], -1)
    w_x = jnp.concatenate([rot(rq), rot(rk), rot(aq), dup(ak), dup(rot(ak)), dup(av)], -1)
    assert w_x.shape == (depth, D, N_X) and w_vd.shape == (depth, D, 2 * LANES), (w_x.shape, w_vd.shape)
    return w_x, w_vd


def kernel(x, c, ctx, c_ctx, ada_w, ada_b, norm_w, ffn1_w_in, ffn1_w_out, ffn2_w_in, ffn2_w_out, mix_w_in, ret_decay_logit, att_q_norm, att_k_norm, att_sink, rwkv_shift, rwkv_w0, rwkv_w_up, rwkv_a0, rwkv_a_up, rwkv_g_up, rwkv_k_k, rwkv_k_a, rwkv_r_k, rwkv_v0, rwkv_v_down, rwkv_v_up, rwkv_gn_w, rwkv_gn_b, w_branch_ret, w_branch_att, w_branch_rwkv, w_out):
    bsz = x.shape[0]
    depth = ada_w.shape[0]
    ct = bsz * N_CTX
    nt = ct + bsz * N_LAT
    tm = math.gcd(1024, ct)
    ffn_tm = math.gcd(2048, ct)
    tb = {k: jnp.asarray(v) for k, v in _tables().items()}
    bo = tb["bo64"].astype(BF16)
    rcs = jnp.concatenate([tb["rcos"], tb["rsin"]], -1)
    stack_split = lambda w: jnp.stack(_split(w))

    xs = jnp.concatenate([ctx.reshape(ct, D), x.reshape(bsz * N_LAT, D)], 0)
    rows = 8 * ((bsz + 1 + 7) // 8)
    cond = jnp.zeros((rows, D), F32).at[0].set(c_ctx).at[1:bsz + 1].set(c)
    i2, _ = _rot_cols(2)
    vfirst = None
    f1_in, f1_out, f2_in, f2_out = ffn1_w_in, ffn1_w_out, ffn2_w_in, ffn2_w_out
    mix_b = mix_w_in.astype(BF16)
    w_x, w_vd = _derived_weights(mix_b, rwkv_v_down)
    n_mix = mix_b.shape[2]
    assert M_RW % (M_GT - M_RW) == 0
    wb_ret, wb_att = w_branch_ret.astype(BF16), w_branch_att.astype(BF16)
    wb_rw, wb_out = w_branch_rwkv.astype(BF16), w_out.astype(BF16)
    for l in range(depth):
        last = l == depth - 1
        mod = _ada_call(cond, ada_w, ada_b[l], l).reshape(rows, N_ADA, D)
        xs = _ffn_call(xs, mod, norm_w[l, 0], f1_in, f1_out, l, base=0, ct=ct, tm=ffn_tm, tile0=0,
                       n_tiles=nt // ffn_tm)

        proj_rw = _proj_call(xs, mod, norm_w[l, 1], mix_b, (M_GT - M_RW, M_RW // (M_GT - M_RW)), w_vd, l,
                             plan=PLAN_RW, n_cols=N_RW, ct=ct, tm=tm, out_dtype=F32)
        proj = _proj_call(xs, mod, norm_w[l, 1], mix_b, (n_mix, 0), w_x, l,
                          plan=PLAN_REST, n_cols=N_REST, ct=ct, tm=tm // 2, out_dtype=BF16)

        log_g = jax.nn.log_sigmoid(ret_decay_logit[l].astype(F32))
        oret = _ret_call(proj, log_g, rcs, tb["dist"], tb["ze"], tb["xe"], bsz=bsz)

        qw = jnp.tile(att_q_norm[l], 4).reshape(1, 256)
        qwr = jnp.tile(att_q_norm[l][i2[:64]], 4).reshape(1, 256)
        kw = jnp.tile(att_k_norm[l], 2).reshape(1, 128)
        kwr = jnp.tile(att_k_norm[l][i2[:64]], 2).reshape(1, 128)
        att = _att_call(proj, att_sink[l].astype(F32), tb["qcos"], tb["qsin"], tb["kcos"], tb["ksin"],
                        qw, qwr, kw, kwr, bo[:256, :256], bsz=bsz)

        shift = jnp.concatenate([rwkv_shift[l], jnp.tile(jnp.array([[0.0], [1.0], [0.0]], F32), (1, 128))], -1)
        wlo = jnp.zeros((2, LANES, 1024), F32)
        for d in range(2):
            wlo = wlo.at[0, d * 64:(d + 1) * 64, d * 512:(d + 1) * 512].set(rwkv_w_up[l, d])
            wlo = wlo.at[1, d * 64:(d + 1) * 64, d * 512:(d + 1) * 512].set(rwkv_a_up[l, d])
        if l > 0:
            vup = jnp.zeros((128, 512), F32).at[0:32].set(rwkv_v_up[l - 1])
            v0 = rwkv_v0[l - 1]
        else:
            vup = jnp.zeros((128, 512), F32)
            v0 = jnp.zeros((512,), F32)
        vec = jnp.stack([rwkv_k_k[l], rwkv_k_a[l], rwkv_r_k[l].reshape(512), v0,
                         rwkv_w0[l, 0], rwkv_w0[l, 1], rwkv_a0[l, 0], rwkv_a0[l, 1]])
        vf_in = vfirst if l > 0 else jnp.zeros((nt, 512), F32)
        r, v, kk, g, bonus, lw, kd, bv = _rw_prep_call(proj_rw, shift, stack_split(wlo), stack_split(rwkv_g_up[l]),
                                                        stack_split(vup), bo, vec, vf_in, ct=ct, mix_v=l > 0)
        if l == 0:
            vfirst = v
        y = _rw_scan_call(r, v, kk, lw, kd, bv, tb["lcum"], tb["mbig"], tb["lvl"], tb["eye"], bsz=bsz)

        gn = jnp.stack([rwkv_gn_w[l], rwkv_gn_b[l]])
        tile0 = ct // MERGE_TM if last else 0
        xs = _merge_call(xs, mod, proj, oret, att, y, bonus, g, bo, gn, wb_ret, wb_att, wb_rw, wb_out, l,
                         ct=ct, tm=MERGE_TM, tile0=tile0, n_tiles=nt // MERGE_TM - tile0)
        xs = _ffn_call(xs, mod, norm_w[l, 2], f2_in, f2_out, l, base=6, ct=0 if last else ct, tm=ffn_tm, tile0=0,
                       n_tiles=xs.shape[0] // ffn_tm)
    return xs.reshape(bsz, N_LAT, D)
```

```python
import functools
import math

import numpy as np
import jax
import jax.numpy as jnp
from jax import lax
from jax.experimental import pallas as pl
from jax.experimental.pallas import tpu as pltpu

F32 = jnp.float32
BF16 = jnp.bfloat16
HI = lax.Precision.HIGHEST

D = 1024
N_LAT = 2048
N_CTX = 256
GRID_W = 64
N_ADA = 9
D_FF = 2816
NORM_EPS = 1e-6
RET_CHUNK = 128
ATT_BLOCK = 128
WINDOW = 128
RW_CHUNK = 64
RWKV_GN_EPS = 64e-5
RWKV_DECAY_SCALE = 0.6065306597126334
HEAD64 = 64
LANES = 128

N_RW = 2048
C_RQ = 0
C_RV = 2048
C_RG = 3072
C_AQ = 4096
C_AK = 5120
C_AV = 5632
C_GT = 6144
N_REST = 9216

VMEM_LIMIT = 56 * 1024 * 1024


def _dot(a, b, prec=None):
    return jnp.dot(a, b, preferred_element_type=F32, precision=prec)


def _dot_nt(a, b, prec=None):
    return lax.dot_general(a, b, (((1,), (1,)), ((), ())), preferred_element_type=F32, precision=prec)


def _dot_tn(a, b, prec=None):
    return lax.dot_general(a, b, (((0,), (0,)), ((), ())), preferred_element_type=F32, precision=prec)


NN = (((1,), (0,)), ((), ()))
NT = (((1,), (1,)), ((), ()))


def _split(x, terms=2):
    out = []
    for _ in range(terms):
        piece = x.astype(BF16)
        out.append(piece)
        x = x - piece.astype(F32)
    return tuple(out)


def _mm(a, b, dims=NN):
    acc = None
    for i, ai in enumerate(a):
        for j, bj in enumerate(b):
            if i + j < max(len(a), len(b)):
                term = lax.dot_general(ai, bj, dims, preferred_element_type=F32)
                acc = term if acc is None else acc + term
    return acc


def _sigmoid(x):
    return 0.5 * jnp.tanh(0.5 * x) + 0.5


def _params(sem):
    return pltpu.CompilerParams(dimension_semantics=sem, vmem_limit_bytes=VMEM_LIMIT)


def _ada_kernel(c_ref, w_ref, b_ref, o_ref):
    c = c_ref[...]
    o_ref[...] = _dot(c * _sigmoid(c), w_ref[...], HI) + b_ref[...]


def _ada_call(cond, w, b, layer):
    rows = cond.shape[0]
    tn = 1024
    n = w.shape[2]
    return pl.pallas_call(
        _ada_kernel,
        grid=(n // tn,),
        in_specs=[pl.BlockSpec((rows, D), lambda j: (0, 0)),
                  pl.BlockSpec((None, D, tn), lambda j: (layer, 0, j)),
                  pl.BlockSpec((1, tn), lambda j: (0, j))],
        out_specs=pl.BlockSpec((rows, tn), lambda j: (0, j)),
        out_shape=jax.ShapeDtypeStruct((rows, n), F32),
        compiler_params=_params(("arbitrary",)),
        name="ada",
    )(cond, w, b.reshape(1, n))


def _mod_row(tok0, ct):
    return jnp.where(tok0 < ct, 0, 1 + (tok0 - ct) // N_LAT)


def _norm_mod(x, nw, shift, scale):
    y = x * lax.rsqrt(jnp.mean(x * x, -1, keepdims=True) + NORM_EPS) * nw
    return y * (1.0 + scale) + shift


def _ffn_kernel(x_ref, mod_ref, nw_ref, wg_ref, wu_ref, wo_ref, o_ref, h_sc, acc_sc, *, base, n_ff):
    j = pl.program_id(1)

    @pl.when(j == 0)
    def _():
        h = _norm_mod(x_ref[...], nw_ref[...], mod_ref[0, base:base + 1, :], mod_ref[0, base + 1:base + 2, :])
        h_sc[...] = h.astype(BF16)
        acc_sc[...] = jnp.zeros_like(acc_sc)

    h = h_sc[...]
    g = _dot(h, wg_ref[...].astype(BF16))
    u = _dot(h, wu_ref[...].astype(BF16))
    act = g * _sigmoid(g) * u
    acc_sc[...] += _dot(act.astype(BF16), wo_ref[...].astype(BF16))

    @pl.when(j == n_ff - 1)
    def _():
        o_ref[...] = x_ref[...] + 0.5 * mod_ref[0, base + 2:base + 3, :] * acc_sc[...]


def _ffn_call(x, mod, nw, w_in, w_out, layer, *, base, ct, tm, tile0, n_tiles):
    tf = 256
    n_ff = D_FF // tf
    nt = x.shape[0]
    kern = functools.partial(_ffn_kernel, base=base, n_ff=n_ff)
    return pl.pallas_call(
        kern,
        grid=(n_tiles, n_ff),
        in_specs=[pl.BlockSpec((tm, D), lambda i, j: (i + tile0, 0)),
                  pl.BlockSpec((1, N_ADA, D), lambda i, j: (_mod_row((i + tile0) * tm, ct), 0, 0)),
                  pl.BlockSpec((1, D), lambda i, j: (0, 0)),
                  pl.BlockSpec((None, D, tf), lambda i, j: (layer, 0, j)),
                  pl.BlockSpec((None, D, tf), lambda i, j: (layer, 0, j + n_ff)),
                  pl.BlockSpec((None, tf, D), lambda i, j: (layer, j, 0))],
        out_specs=pl.BlockSpec((tm, D), lambda i, j: (i, 0)),
        out_shape=jax.ShapeDtypeStruct((n_tiles * tm, D), F32),
        scratch_shapes=[pltpu.VMEM((tm, D), BF16), pltpu.VMEM((tm, D), F32)],
        compiler_params=_params(("parallel", "arbitrary")),
        name="ffn",
    )(x, mod, nw.reshape(1, D), w_in, w_in, w_out)


PROJ_TN = 512
MIX_OFF = np.cumsum([0, 512, 512, 1024, 1024, 512, 128, 128, 1920, 3072])
M_RQ, M_RK, M_RV, M_RG, M_AQ, M_AK, M_AV, M_RW, M_GT = (int(v) for v in MIX_OFF[:9])
X_RQ, X_RK, X_AQ, X_AK, X_AKR, X_AV, N_X = 0, 512, 1024, 1536, 1792, 2048, 2304
PLAN_REST = ((0, M_RQ, 512, C_RQ), (1, X_RQ, 512, C_RQ + 512), (0, M_RK, 512, C_RQ + 1024), (1, X_RK, 512, C_RQ + 1536),
             (0, M_RV, 1024, C_RV), (0, M_RG, 1024, C_RG),
             (0, M_AQ, 512, C_AQ), (1, X_AQ, 512, C_AQ + 512),
             (1, X_AK, 256, C_AK), (1, X_AKR, 256, C_AK + 256), (1, X_AV, 256, C_AV), (None, 0, 256, C_AV + 256),
             (0, M_GT, 3072, C_GT))
RW_TAIL = 128
PLAN_RW = ((0, 0, 1920 - RW_TAIL, 0), (1, 0, 2 * LANES, 1920 - RW_TAIL))


def _proj_kernel(x_ref, mod_ref, nw_ref, wa_ref, wb_ref, o_ref, *, plan):
    h = _norm_mod(x_ref[...], nw_ref[...], mod_ref[0, 3:4, :], mod_ref[0, 4:5, :]).astype(BF16)
    for which, src, width, dst in plan:
        for c in range(0, width, PROJ_TN):
            w = min(PROJ_TN, width - c)
            if which is None:
                o_ref[:, dst + c:dst + c + w] = jnp.zeros((x_ref.shape[0], w), o_ref.dtype)
            else:
                w_ref = (wa_ref, wb_ref)[which]
                o_ref[:, dst + c:dst + c + w] = _dot(h, w_ref[:, src + c:src + c + w]).astype(o_ref.dtype)


def _proj_call(x, mod, nw, wa, wa_block, wb, layer, *, plan, n_cols, ct, tm, out_dtype):
    nt = x.shape[0]
    resident = lambda width, col: pl.BlockSpec((None, D, width), lambda i: (layer, 0, col),
                                               pipeline_mode=pl.Buffered(1))
    return pl.pallas_call(
        functools.partial(_proj_kernel, plan=plan),
        grid=(nt // tm,),
        in_specs=[pl.BlockSpec((tm, D), lambda i: (i, 0)),
                  pl.BlockSpec((1, N_ADA, D), lambda i: (_mod_row(i * tm, ct), 0, 0)),
                  pl.BlockSpec((1, D), lambda i: (0, 0)),
                  resident(*wa_block), resident(wb.shape[2], 0)],
        out_specs=pl.BlockSpec((tm, n_cols), lambda i: (i, 0)),
        out_shape=jax.ShapeDtypeStruct((nt, n_cols), out_dtype),
        compiler_params=_params(("parallel",)),
        name="proj",
    )(x, mod, nw.reshape(1, D), wa, wb)


def _scan_blocks(b, d, s, *, nc_ctx, nc_lat, bsz):
    in_ctx = s < nc_ctx
    cs = jnp.where(d == 0, s, nc_ctx - 1 - s)
    ls = jnp.where(d == 0, s - nc_ctx, nc_lat - 1 - (s - nc_ctx))
    tok = jnp.where(in_ctx, b * nc_ctx + cs, bsz * nc_ctx + b * nc_lat + ls)
    pos = jnp.where(in_ctx, cs, nc_ctx + ls)
    return tok, pos


RET_STEP_CHUNKS = 2


def _ret_kernel(lg_ref, qkv_ref, cs_ref, dist_ref, ze_ref, xe_ref, o_ref, st_sc, dec_sc, xi_sc, zeta_sc):
    d = pl.program_id(1)
    s = pl.program_id(2)

    heads = range(8)

    @pl.when(s == 0)
    def _():
        st_sc[...] = jnp.zeros_like(st_sc)
        dist = dist_ref[...]
        for h in heads:
            lg = lg_ref[d, h]
            dec_sc[h] = jnp.where(dist >= 0.0, jnp.exp(lg * dist), 0.0)
            xi_sc[h] = jnp.exp(lg * xe_ref[...])
            zeta_sc[h] = jnp.exp(lg * ze_ref[...])

    c = RET_CHUNK
    lane = lax.broadcasted_iota(jnp.int32, (1, LANES), 1)
    hmask = [((lane // HEAD64) == e).astype(F32) for e in range(2)]
    starts = [pl.multiple_of(jnp.where(d == 0, i, RET_STEP_CHUNKS - 1 - i) * c, c) for i in range(RET_STEP_CHUNKS)]
    qms, kps, vhs = [], [], []
    for st0 in starts:
        rows = pl.ds(st0, c)
        cos = cs_ref[rows, 0:512]
        sin = cs_ref[rows, 512:1024]
        q = qkv_ref[rows, 0:512] * cos + qkv_ref[rows, 512:1024] * sin
        k = (qkv_ref[rows, 1024:1536] * cos + qkv_ref[rows, 1536:2048] * sin) * (HEAD64 ** -0.5)
        kps.append([k[:, p * LANES:(p + 1) * LANES].astype(BF16) for p in range(4)])
        qms.append([(q[:, (h // 2) * LANES:(h // 2 + 1) * LANES] * hmask[h % 2]).astype(BF16) for h in heads])
        vhs.append([qkv_ref[rows, C_RV + h * LANES:C_RV + (h + 1) * LANES] for h in heads])
    chunks = range(RET_STEP_CHUNKS)
    scs = [[(_dot_nt(qms[i][h], kps[i][h // 2]) * dec_sc[h]).astype(BF16) for h in heads] for i in chunks]
    intra = [[_dot(scs[i][h], vhs[i][h].astype(BF16)) for h in heads] for i in chunks]
    us = [[_dot_tn(kps[i][h // 2], (vhs[i][h] * zeta_sc[h]).astype(BF16)) for h in heads] for i in chunks]
    chunk_len = jnp.full((1, LANES), float(c), F32)
    decay = [jnp.exp(lg_ref[d, h] * chunk_len) for h in heads]
    sts = [st_sc[h] for h in heads]
    for i in chunks:
        inter = [_dot(qms[i][h], sts[h].astype(BF16)) * xi_sc[h] for h in heads]
        for h in heads:
            o_ref[pl.ds(starts[i], c), h * LANES:(h + 1) * LANES] = (intra[i][h] + inter[h]).astype(o_ref.dtype)
        sts = [sts[h] * decay[h] + us[i][h] for h in heads]
    for h in heads:
        st_sc[h] = sts[h]


def _ret_call(proj, log_g, cos_sin, dist, ze, xe, *, bsz):
    nt = proj.shape[0]
    c = RET_CHUNK
    rows = RET_STEP_CHUNKS * c
    nc_ctx, nc_lat = N_CTX // rows, N_LAT // rows
    steps = nc_ctx + nc_lat
    blocks = functools.partial(_scan_blocks, nc_ctx=nc_ctx, nc_lat=nc_lat, bsz=bsz)
    assert C_RQ == 0 and C_RV == 2048 and C_RG == 3072
    pos_map = lambda b, d, s, lg: (blocks(b, d, s)[1], 0)
    dir_map = lambda b, d, s, lg: (d, 0, 0)
    grid_spec = pltpu.PrefetchScalarGridSpec(
        num_scalar_prefetch=1,
        grid=(bsz, 2, steps),
        in_specs=[pl.BlockSpec((rows, C_RG), lambda b, d, s, lg: (blocks(b, d, s)[0], 0)),
                  pl.BlockSpec((rows, 1024), pos_map),
                  pl.BlockSpec((None, c, c), dir_map),
                  pl.BlockSpec((None, c, LANES), dir_map),
                  pl.BlockSpec((None, c, LANES), dir_map)],
        out_specs=pl.BlockSpec((None, rows, 1024), lambda b, d, s, lg: (d, blocks(b, d, s)[0], 0)),
        scratch_shapes=[pltpu.VMEM((8, LANES, LANES), F32), pltpu.VMEM((8, c, c), F32),
                        pltpu.VMEM((8, c, LANES), F32), pltpu.VMEM((8, c, LANES), F32)],
    )
    return pl.pallas_call(
        _ret_kernel,
        grid_spec=grid_spec,
        out_shape=jax.ShapeDtypeStruct((2, nt, 1024), BF16),
        compiler_params=_params(("parallel", "parallel", "arbitrary")),
        name="retention",
    )(log_g, proj, cos_sin, dist, ze, xe)


NEG = -1e30


ATT_KV_PER_STEP = 2


def _att_kernel(sink_ref, q_ref, qr_ref, qcos_ref, qsin_ref, qw_ref, qwr_ref, bo_ref,
                kl_ref, klr_ref, vl_ref, kcos_ref, ksin_ref, kc_ref, vc_ref, kw_ref, kwr_ref,
                o_ref, kn_sc, kcn_sc, *, n_ctx_blk):
    hk0 = pl.program_id(1) * ATT_KV_PER_STEP
    jb = pl.program_id(2)
    blk = ATT_BLOCK
    kv_lanes = [slice(e * LANES, (e + 1) * LANES) for e in range(ATT_KV_PER_STEP)]

    @pl.when(jb == 0)
    def _():
        for sl in kv_lanes:
            kl = kl_ref[:, sl].astype(F32)
            rs = lax.rsqrt(jnp.sum(kl * kl, -1, keepdims=True) * (0.5 / HEAD64) + NORM_EPS)
            kn = ((kl * kw_ref[...]) * kcos_ref[...]
                  + (klr_ref[:, sl].astype(F32) * kwr_ref[...]) * ksin_ref[...]) * rs
            kn_sc[:, sl] = kn.astype(BF16)
            kc = kc_ref[:, sl].astype(F32)
            rc = lax.rsqrt(jnp.sum(kc * kc, -1, keepdims=True) * (0.5 / HEAD64) + NORM_EPS)
            kcn_sc[:, sl] = (kc * kw_ref[...] * rc).astype(BF16)

    q = q_ref[...].astype(F32)
    mean_sq = _mm(_split(q * q), (bo_ref[...],)) * (1.0 / HEAD64)
    rs = lax.rsqrt(mean_sq + NORM_EPS)
    qn = ((q * qw_ref[...]) * qcos_ref[...]
          + (qr_ref[...].astype(F32) * qwr_ref[...]) * qsin_ref[...]) * rs * (HEAD64 ** -0.5)

    reach = jnp.where(jb >= n_ctx_blk, WINDOW, -1)
    lb = jnp.maximum(jb - n_ctx_blk, 0)
    n_win = 3 * blk
    start = jnp.clip((lb - 1) * blk, 0, N_LAT - n_win)
    start = pl.multiple_of(start, blk)
    n_keys = n_win + N_CTX
    kall = [jnp.concatenate([kn_sc[pl.ds(start, n_win), sl], kcn_sc[:, sl]], axis=0) for sl in kv_lanes]
    vall = [jnp.concatenate([vl_ref[pl.ds(start, n_win), sl], vc_ref[:, sl]], axis=0) for sl in kv_lanes]
    qpos = lb * blk + lax.broadcasted_iota(jnp.int32, (blk, n_keys), 0)
    col = lax.broadcasted_iota(jnp.int32, (blk, n_keys), 1)
    valid = (col >= n_win) | (jnp.abs(start + col - qpos) <= reach)
    lane = lax.broadcasted_iota(jnp.int32, (1, LANES), 1)
    hmask = [((lane // HEAD64) == e).astype(F32) for e in range(2)]
    groups = range(4 * ATT_KV_PER_STEP)
    sinks = [sink_ref[hk0 * 4 + g] for g in groups]
    qgs = [(qn[:, (g // 2) * LANES:(g // 2 + 1) * LANES] * hmask[g % 2]).astype(BF16) for g in groups]
    ss = [jnp.where(valid, _dot_nt(qgs[g], kall[g // 4]), NEG) for g in groups]
    ms = [jnp.maximum(jnp.max(ss[g], -1, keepdims=True), sinks[g]) for g in groups]
    ps = [jnp.exp(ss[g] - ms[g]) for g in groups]
    dens = [jnp.sum(ps[g], -1, keepdims=True) + jnp.exp(sinks[g] - ms[g]) for g in groups]
    ogs = [_dot(ps[g].astype(BF16), vall[g // 4]) / dens[g] * hmask[g % 2] for g in groups]
    o_ref[...] = jnp.concatenate([ogs[2 * p] + ogs[2 * p + 1] for p in range(2 * ATT_KV_PER_STEP)],
                                 axis=-1).astype(o_ref.dtype)


def _att_call(proj, sink, qcos, qsin, kcos, ksin, qw, qwr, kw, kwr, bo, *, bsz):
    nt = proj.shape[0]
    blk = ATT_BLOCK
    n_ctx_blk, n_lat_blk = N_CTX // blk, N_LAT // blk
    steps = n_ctx_blk + n_lat_blk
    qwid = 256 * ATT_KV_PER_STEP
    kwid = LANES * ATT_KV_PER_STEP

    def qtok(b, jb):
        return jnp.where(jb < n_ctx_blk, b * n_ctx_blk + jb, bsz * n_ctx_blk + b * n_lat_blk + (jb - n_ctx_blk))

    def qmap(col):
        return lambda b, hk, jb, sk: (qtok(b, jb), col // qwid + hk)

    qpos_map = lambda b, hk, jb, sk: (jb, 0)
    const = lambda b, hk, jb, sk: (0, 0)
    ct_rows = bsz * N_CTX

    def lat_map(col):
        return lambda b, hk, jb, sk: (ct_rows // N_LAT + b, col // kwid + hk)

    def ctx_map(col):
        return lambda b, hk, jb, sk: (b, col // kwid + hk)

    grid_spec = pltpu.PrefetchScalarGridSpec(
        num_scalar_prefetch=1,
        grid=(bsz, 2 // ATT_KV_PER_STEP, steps),
        in_specs=[pl.BlockSpec((blk, qwid), qmap(C_AQ)),
                  pl.BlockSpec((blk, qwid), qmap(C_AQ + 512)),
                  pl.BlockSpec((blk, qwid), qpos_map),
                  pl.BlockSpec((blk, qwid), qpos_map),
                  pl.BlockSpec((1, qwid), const),
                  pl.BlockSpec((1, qwid), const),
                  pl.BlockSpec((qwid, qwid), const),
                  pl.BlockSpec((N_LAT, kwid), lat_map(C_AK)),
                  pl.BlockSpec((N_LAT, kwid), lat_map(C_AK + 256)),
                  pl.BlockSpec((N_LAT, kwid), lat_map(C_AV)),
                  pl.BlockSpec((N_LAT, LANES), const),
                  pl.BlockSpec((N_LAT, LANES), const),
                  pl.BlockSpec((N_CTX, kwid), ctx_map(C_AK)),
                  pl.BlockSpec((N_CTX, kwid), ctx_map(C_AV)),
                  pl.BlockSpec((1, LANES), const),
                  pl.BlockSpec((1, LANES), const)],
        out_specs=pl.BlockSpec((blk, qwid), lambda b, hk, jb, sk: (qtok(b, jb), hk)),
        scratch_shapes=[pltpu.VMEM((N_LAT, kwid), BF16), pltpu.VMEM((N_CTX, kwid), BF16)],
    )
    kern = functools.partial(_att_kernel, n_ctx_blk=n_ctx_blk)
    return pl.pallas_call(
        kern,
        grid_spec=grid_spec,
        out_shape=jax.ShapeDtypeStruct((nt, 512), BF16),
        compiler_params=_params(("parallel", "parallel", "arbitrary")),
        name="attention",
    )(sink, proj, proj, qcos, qsin, qw, qwr, bo, proj, proj, proj, kcos, ksin, proj, proj, kw, kwr)


RW_TM = 256
RW_PREP_GROUPS = 2


def _rw_prep_kernel(cur_ref, prev_ref, next_ref, shift_ref, wlo_ref, gup_ref, vup_ref, bo_ref, vec_ref, vf_ref,
                    r_o, v_o, kk_o, g_o, bonus_o, lw_o, kd_o, bv_o, *, ct, mix_v):
    i = pl.program_id(0)
    tm = RW_TM
    tok0 = i * tm
    lat_off = tok0 - ct
    is_start = jnp.where(tok0 < ct, tok0 % N_CTX == 0, lat_off % N_LAT == 0)
    is_end = jnp.where(tok0 < ct, (tok0 + tm) % N_CTX == 0, (lat_off + tm) % N_LAT == 0)
    cur = cur_ref[...]
    row = lax.broadcasted_iota(jnp.int32, (tm, 1), 0)
    prev_row = jnp.where(is_start, 0.0, prev_ref[7:8, :])
    next_row = jnp.where(is_end, 0.0, next_ref[0:1, :])
    prev = jnp.where(row == 0, prev_row, pltpu.roll(cur, 1, 0))
    nxt = jnp.where(row == tm - 1, next_row, pltpu.roll(cur, tm - 1, 0))
    c = prev * shift_ref[0:1, :] + cur * shift_ref[1:2, :] + nxt * shift_ref[2:3, :]

    vec = vec_ref[...]
    k_k, k_a, r_k, v0 = vec[0:1], vec[1:2], vec[2:3], vec[3:4]
    bo = (bo_ref[...],)
    rows = [slice(i * (tm // RW_PREP_GROUPS), (i + 1) * (tm // RW_PREP_GROUPS)) for i in range(RW_PREP_GROUPS)]
    grp = range(RW_PREP_GROUPS)
    r = [c[rw, 0:512] for rw in rows]
    k = [c[rw, 512:1024] for rw in rows]
    v = [c[rw, 1024:1536] for rw in rows]
    lo = [c[rw, 1536:1792] for rw in rows]
    gd = [c[rw, 1792:1920] for rw in rows]
    vl = [c[rw, 1920:2048] for rw in rows]
    if mix_v:
        mixv = [_sigmoid(v0 + _mm(_split(vl[i]), (vup_ref[0], vup_ref[1]))) for i in grp]
        v = [v[i] + (vf_ref[rows[i], :] - v[i]) * mixv[i] for i in grp]
    g = [_mm(_split(_sigmoid(gd[i])), (gup_ref[0], gup_ref[1])) for i in grp]
    kk = [k[i] * k_k for i in grp]
    ss = [_mm(_split(kk[i] * kk[i]), bo) for i in grp]
    kk = [kk[i] / jnp.maximum(jnp.sqrt(ss[i]), 1e-12) for i in grp]
    z_lora = [_mm(_split(jnp.tanh(lo[i][:, 0:LANES])), (wlo_ref[0, 0], wlo_ref[1, 0])) for i in grp]
    a_lora = [_mm(_split(lo[i][:, LANES:2 * LANES]), (wlo_ref[0, 1], wlo_ref[1, 1])) for i in grp]
    ksum = [jnp.zeros_like(k[i]) for i in grp]
    for d in range(2):
        a = [_sigmoid(vec[6 + d:7 + d] + a_lora[i][:, d * 512:(d + 1) * 512]) for i in grp]
        kd = [k[i] * (1.0 + (a[i] - 1.0) * k_a) for i in grp]
        for i in grp:
            lw_o[d, rows[i], :] = -RWKV_DECAY_SCALE * _sigmoid(vec[4 + d:5 + d] + z_lora[i][:, d * 512:(d + 1) * 512])
            kd_o[d, rows[i], :] = kd[i]
            bv_o[d, rows[i], :] = a[i] * kk[i]
        ksum = [ksum[i] + kd[i] for i in grp]
    bonus = [_mm(_split(r[i] * ksum[i] * r_k), bo) * v[i] for i in grp]
    for i in grp:
        bonus_o[rows[i], :] = bonus[i]
        r_o[rows[i], :] = r[i]
        v_o[rows[i], :] = v[i]
        kk_o[rows[i], :] = kk[i]
        g_o[rows[i], :] = g[i]


def _rw_prep_call(proj, shift, wlo, gup, vup, bo, vec, vfirst, *, ct, mix_v):
    nt = proj.shape[0]
    tm = RW_TM
    nblk8 = nt // 8
    tok = pl.BlockSpec((tm, 512), lambda i: (i, 0))
    dirs = pl.BlockSpec((2, tm, 512), lambda i: (0, i, 0))
    const = lambda i: (0, 0)
    kern = functools.partial(_rw_prep_kernel, ct=ct, mix_v=mix_v)
    one = jax.ShapeDtypeStruct((nt, 512), F32)
    two = jax.ShapeDtypeStruct((2, nt, 512), F32)
    return pl.pallas_call(
        kern,
        grid=(nt // tm,),
        in_specs=[pl.BlockSpec((tm, N_RW), lambda i: (i, 0)),
                  pl.BlockSpec((8, N_RW), lambda i: (jnp.maximum(i * (tm // 8) - 1, 0), 0)),
                  pl.BlockSpec((8, N_RW), lambda i: (jnp.minimum((i + 1) * (tm // 8), nblk8 - 1), 0)),
                  pl.BlockSpec((3, N_RW), const),
                  pl.BlockSpec((2, 2, LANES, 1024), lambda i: (0, 0, 0, 0)),
                  pl.BlockSpec((2, 128, 512), lambda i: (0, 0, 0)),
                  pl.BlockSpec((2, 128, 512), lambda i: (0, 0, 0)),
                  pl.BlockSpec((512, 512), const),
                  pl.BlockSpec((8, 512), const),
                  tok],
        out_specs=[tok, tok, tok, tok, tok, dirs, dirs, dirs],
        out_shape=[one, one, one, one, one, two, two, two],
        compiler_params=_params(("parallel",)),
        name="rwkv_prep",
    )(proj, proj, proj, shift, wlo, gup, vup, bo, vec, vfirst)


SCAN_TERMS = dict(v=1, x=1, y=1, ak=1, ao=1, m=1, rhs=1, pq=1, rb=1, rk=1, bp=1, kp=1, st=1, yq=1, g=1)
RW_STEP_CHUNKS = 4


def _rw_scan_kernel(r_ref, v_ref, kk_ref, lw_ref, kd_ref, bv_ref, lcum_ref, mbig_ref, lvl_ref, eye_ref,
                    y_ref, st_sc):
    d = pl.program_id(1)
    s = pl.program_id(2)
    t = RW_CHUNK
    nt_ = SCAN_TERMS

    @pl.when(s == 0)
    def _():
        st_sc[...] = jnp.zeros_like(st_sc)

    lcum = (lcum_ref[...].astype(BF16),)
    mbig = mbig_ref[...]
    eye = eye_ref[...]
    lane = lax.broadcasted_iota(jnp.int32, (1, LANES), 1)
    m_e = (lane < HEAD64).astype(F32)
    m_o = 1.0 - m_e

    def stack2(x):
        return jnp.concatenate([x * m_e, x * m_o], axis=0)

    pairs = range(4)
    sls = [slice(p * LANES, (p + 1) * LANES) for p in pairs]
    starts = [pl.multiple_of(jnp.where(d == 0, c, RW_STEP_CHUNKS - 1 - c) * t, t) for c in range(RW_STEP_CHUNKS)]
    aw2, rt2, bi2, ki2, bp2t, kp2t, v2s, w_tot = [], [], [], [], [], [], [], []
    for st0 in starts:
        rows = pl.ds(st0, t)
        lw = lw_ref[rows, :]
        cum = _mm(lcum, _split(lw, 3))
        tot = jnp.sum(lw, axis=0, keepdims=True)
        kk, kd, bv = kk_ref[rows, :], kd_ref[rows, :], bv_ref[rows, :]
        aw = -kk * jnp.exp(cum - lw)
        rt = r_ref[rows, :] * jnp.exp(cum)
        e_inv = jnp.exp(-cum)
        e_rem = jnp.exp(tot - cum)
        v = v_ref[rows, :]
        aw2 += [stack2(aw[:, sl]) for sl in sls]
        rt2 += [stack2(rt[:, sl]) for sl in sls]
        bi2 += [stack2((bv * e_inv)[:, sl]) for sl in sls]
        ki2 += [stack2((kd * e_inv)[:, sl]) for sl in sls]
        bp2t += [stack2((bv * e_rem)[:, sl]).T for sl in sls]
        kp2t += [stack2((kd * e_rem)[:, sl]).T for sl in sls]
        v2s += [_split(stack2(v[:, sl]), nt_["v"]) for sl in sls]
        w_tot += [jnp.exp(tot)[:, sl] for sl in sls]

    items = range(RW_STEP_CHUNKS * 4)
    big = [_mm(_split(jnp.concatenate([aw2[i], rt2[i]], axis=0), nt_["x"]),
               _split(jnp.concatenate([bi2[i], ki2[i]], axis=0), nt_["y"]), NT) * mbig for i in items]
    a_ab = [b[0:2 * t, 0:2 * t] for b in big]
    a_ak = [b[0:2 * t, 2 * t:4 * t] for b in big]
    a_rb = [b[2 * t:4 * t, 0:2 * t] for b in big]
    a_rk = [b[2 * t:4 * t, 2 * t:4 * t] for b in big]
    akv = [_mm(_split(a_ak[i], nt_["ak"]), v2s[i]) for i in items]
    inv = [eye + a * lvl_ref[0] for a in a_ab]
    for j in range(1, int(math.log2(t))):
        ms = [_split(m, nt_["m"]) for m in inv]
        low = [_split(_mm(_split(a_ab[i] * lvl_ref[j], nt_["ao"]), ms[i]), nt_["m"]) for i in items]
        inv = [inv[i] + _mm(ms[i], low[i]) for i in items]
    pqs = [_split(_mm(_split(inv[i], nt_["m"]),
                      _split(jnp.concatenate([aw2[i], akv[i]], axis=1), nt_["rhs"])), nt_["pq"]) for i in items]
    tmp = [_mm(_split(a_rb[i], nt_["rb"]), pqs[i]) for i in items]
    ark_v = [_mm(_split(a_rk[i], nt_["rk"]), v2s[i]) for i in items]
    gu = [_mm(_split(bp2t[i], nt_["bp"]), pqs[i]) for i in items]
    kpv = [_mm(_split(kp2t[i], nt_["kp"]), v2s[i]) for i in items]
    yq = [_split(rt2[i] + tmp[i][:, 0:LANES], nt_["yq"]) for i in items]
    y0 = [tmp[i][:, LANES:2 * LANES] + ark_v[i] for i in items]
    g_mat = [_split(gu[i][:, 0:LANES] + eye * w_tot[i], nt_["g"]) for i in items]
    u_mat = [gu[i][:, LANES:2 * LANES] + kpv[i] for i in items]

    st = [st_sc[p] for p in pairs]
    for c in range(RW_STEP_CHUNKS):
        sts = [_split(st[p], nt_["st"]) for p in pairs]
        y2 = [_mm(yq[4 * c + p], sts[p]) + y0[4 * c + p] for p in pairs]
        st = [_mm(g_mat[4 * c + p], sts[p]) + u_mat[4 * c + p] for p in pairs]
        for p in pairs:
            y_ref[pl.ds(starts[c], t), sls[p]] = y2[p][0:t] + y2[p][t:2 * t]
    for p in pairs:
        st_sc[p] = st[p]


def _rw_scan_call(r, v, kk, lw, kd, bv, lcum, mbig, lvl, eye, *, bsz):
    nt = r.shape[0]
    t = RW_CHUNK
    rows = RW_STEP_CHUNKS * t
    nc_ctx, nc_lat = N_CTX // rows, N_LAT // rows
    steps = nc_ctx + nc_lat
    blocks = functools.partial(_scan_blocks, nc_ctx=nc_ctx, nc_lat=nc_lat, bsz=bsz)
    tok = pl.BlockSpec((rows, 512), lambda b, d, s: (blocks(b, d, s)[0], 0))
    tokd = pl.BlockSpec((None, rows, 512), lambda b, d, s: (d, blocks(b, d, s)[0], 0))
    n_lvl = lvl.shape[1]
    return pl.pallas_call(
        _rw_scan_kernel,
        grid=(bsz, 2, steps),
        in_specs=[tok, tok, tok, tokd, tokd, tokd,
                  pl.BlockSpec((None, t, t), lambda b, d, s: (d, 0, 0)),
                  pl.BlockSpec((None, 4 * t, 4 * t), lambda b, d, s: (d, 0, 0)),
                  pl.BlockSpec((None, n_lvl, 2 * t, 2 * t), lambda b, d, s: (d, 0, 0, 0)),
                  pl.BlockSpec((LANES, LANES), lambda b, d, s: (0, 0))],
        out_specs=tokd,
        out_shape=jax.ShapeDtypeStruct((2, nt, 512), F32),
        scratch_shapes=[pltpu.VMEM((4, LANES, LANES), F32)],
        compiler_params=_params(("parallel", "parallel", "arbitrary")),
        name="rwkv_scan",
    )(r, v, kk, lw, kd, bv, lcum, mbig, lvl, eye)


MERGE_TM = 512


def _merge_kernel(x_ref, mod_ref, oret_ref, rg_ref, att_ref, y_ref, bonus_ref, g_ref, gt_ref,
                  bo_ref, gn_ref, wr_ref, wa_ref, ww_ref, wo_ref, o_ref):
    o = oret_ref[0].astype(F32) + oret_ref[1].astype(F32)
    parts = []
    for h in range(8):
        oh = o[:, h * LANES:(h + 1) * LANES]
        mu = jnp.mean(oh, -1, keepdims=True)
        dv = oh - mu
        var = jnp.mean(dv * dv, -1, keepdims=True)
        parts.append(dv * lax.rsqrt(var + NORM_EPS))
    rg = rg_ref[...].astype(F32)
    ret = (rg * _sigmoid(rg)) * jnp.concatenate(parts, axis=-1)

    y = y_ref[0] + y_ref[1]
    bo = (bo_ref[...],)
    mu = _mm(_split(y), bo) * (1.0 / HEAD64)
    dy = y - mu
    var = _mm(_split(dy * dy), bo) * (1.0 / HEAD64)
    yn = dy * lax.rsqrt(var + RWKV_GN_EPS) * gn_ref[0:1, :] + gn_ref[1:2, :]
    rw = (yn + bonus_ref[...]) * g_ref[...]

    merged = (_sigmoid(gt_ref[:, 0:D].astype(F32)) * _dot(ret.astype(BF16), wr_ref[...])
              + _sigmoid(gt_ref[:, D:2 * D].astype(F32)) * _dot(att_ref[...].astype(BF16), wa_ref[...])
              + _sigmoid(gt_ref[:, 2 * D:3 * D].astype(F32)) * _dot(rw.astype(BF16), ww_ref[...]))
    out = _dot(merged.astype(BF16), wo_ref[...])
    o_ref[...] = x_ref[...] + mod_ref[0, 5:6, :] * out


def _merge_call(x, mod, proj, oret, att, y, bonus, g, bo, gn, wr, wa, ww, wo, layer, *, ct, tm, tile0, n_tiles):
    tok = lambda w, col=0: pl.BlockSpec((tm, w), lambda i: (i + tile0, col))
    tok2 = lambda w: pl.BlockSpec((2, tm, w), lambda i: (0, i + tile0, 0))
    const = lambda i: (0, 0)
    wspec = lambda rows: pl.BlockSpec((None, rows, D), lambda i: (layer, 0, 0), pipeline_mode=pl.Buffered(1))
    assert C_GT % (3 * D) == 0
    return pl.pallas_call(
        _merge_kernel,
        grid=(n_tiles,),
        in_specs=[tok(D),
                  pl.BlockSpec((1, N_ADA, D), lambda i: (_mod_row((i + tile0) * tm, ct), 0, 0)),
                  tok2(1024), tok(1024, C_RG // 1024), tok(512), tok2(512), tok(512), tok(512),
                  tok(3 * D, C_GT // (3 * D)),
                  pl.BlockSpec((512, 512), const),
                  pl.BlockSpec((2, 512), const),
                  wspec(1024), wspec(512), wspec(512), wspec(D)],
        out_specs=pl.BlockSpec((tm, D), lambda i: (i, 0)),
        out_shape=jax.ShapeDtypeStruct((n_tiles * tm, D), F32),
        compiler_params=_params(("parallel",)),
        name="merge",
    )(x, mod, oret, proj, att, y, bonus, g, proj, bo, gn, wr, wa, ww, wo)


def _rope_1d(pos, dim, base):
    n_freq = dim // 2
    inv = np.power(np.float32(base), -(np.arange(n_freq, dtype=np.float32) / np.float32(n_freq))).astype(np.float32)
    return pos.astype(np.float32)[:, None] * inv[None, :]


def _tables():
    ang = _rope_1d(np.arange(N_CTX + N_LAT), HEAD64, 10000.0)
    rcos = np.tile(np.concatenate([np.cos(ang), np.cos(ang)], -1), (1, 8)).astype(np.float32)
    rsin = np.tile(np.concatenate([np.sin(ang), np.sin(ang)], -1), (1, 8)).astype(np.float32)
    rows = N_LAT // GRID_W
    row = np.repeat(np.arange(rows), GRID_W)
    col = np.arange(rows * GRID_W) % GRID_W
    aang = np.concatenate([_rope_1d(row, HEAD64 // 2, 10000.0), _rope_1d(col, HEAD64 // 2, 10000.0)], -1)
    ac = np.concatenate([np.cos(aang), np.cos(aang)], -1).astype(np.float32)
    asn = np.concatenate([np.sin(aang), np.sin(aang)], -1).astype(np.float32)
    q_cols = 256 * ATT_KV_PER_STEP
    qcos = np.concatenate([np.ones((N_CTX, q_cols), np.float32), np.tile(ac, (1, q_cols // HEAD64))], 0)
    qsin = np.concatenate([np.zeros((N_CTX, q_cols), np.float32), np.tile(asn, (1, q_cols // HEAD64))], 0)
    kcos = np.tile(ac, (1, 2))
    ksin = np.tile(asn, (1, 2))
    c = RET_CHUNK
    pos = np.arange(c, dtype=np.float32)
    diff = pos[:, None] - pos[None, :]
    dist = np.stack([np.where(diff >= 0, diff, -1.0), np.where(diff <= 0, -diff, -1.0)]).astype(np.float32)
    ze = np.stack([c - 1.0 - pos, pos]).astype(np.float32)
    xe = np.stack([pos + 1.0, c - pos]).astype(np.float32)
    ze = np.broadcast_to(ze[:, :, None], (2, c, LANES)).copy()
    xe = np.broadcast_to(xe[:, :, None], (2, c, LANES)).copy()
    t = RW_CHUNK
    ti = np.arange(t)
    low_incl = (ti[None, :] <= ti[:, None]).astype(np.float32)
    low_strict = (ti[None, :] < ti[:, None]).astype(np.float32)
    lcum = np.stack([low_incl, low_incl.T])
    eye2 = np.eye(2, dtype=np.float32)
    mbig = []
    for strict, incl in ((low_strict, low_incl), (low_strict.T, low_incl.T)):
        s2 = np.kron(eye2, strict)
        i2 = np.kron(eye2, incl)
        mbig.append(np.block([[s2, s2], [i2, i2]]))
    mbig = np.stack(mbig).astype(np.float32)
    lvl = []
    for j in range(int(math.log2(t))):
        sz = 2 ** j
        same = (ti[:, None] // (2 * sz)) == (ti[None, :] // (2 * sz))
        m = (same & ((ti[:, None] % (2 * sz)) >= sz) & ((ti[None, :] % (2 * sz)) < sz)).astype(np.float32)
        lvl.append(np.kron(eye2, m))
    lvl = np.stack(lvl)
    lvl = np.stack([lvl, lvl.transpose(0, 2, 1)]).astype(np.float32)
    eye = np.eye(LANES, dtype=np.float32)
    bo64 = np.kron(np.eye(8, dtype=np.float32), np.ones((HEAD64, HEAD64), np.float32))
    return dict(rcos=rcos, rsin=rsin, qcos=qcos, qsin=qsin, kcos=kcos, ksin=ksin, dist=dist, ze=ze, xe=xe,
                lcum=lcum, mbig=mbig, lvl=lvl, eye=eye, bo64=bo64)


def _rot_cols(n_heads):
    half = HEAD64 // 2
    idx, sgn = [], []
    for h in range(n_heads):
        base = h * HEAD64
        idx += list(range(base + half, base + HEAD64)) + list(range(base, base + half))
        sgn += [-1.0] * half + [1.0] * half
    return np.array(idx), np.array(sgn, np.float32)


def _derived_weights(w_in, v_down):
    depth = w_in.shape[0]
    rq, rk = w_in[..., M_RQ:M_RK], w_in[..., M_RK:M_RV]
    aq, ak, av = w_in[..., M_AQ:M_AK], w_in[..., M_AK:M_AV], w_in[..., M_AV:M_RW]
    half = HEAD64 // 2
    sign = jnp.asarray([-1.0, 1.0], w_in.dtype).reshape(2, 1)

    def rot(m):
        heads = m.shape[-1] // HEAD64
        halves = m.reshape(depth, D, heads, 2, half)
        return (halves[:, :, :, ::-1, :] * sign).reshape(depth, D, heads * HEAD64)

    dup = lambda m: jnp.concatenate([m[..., 0:64], m[..., 0:64], m[..., 64:128], m[..., 64:128]], -1)
    z = lambda n, d=depth: jnp.zeros((d, D, n), w_in.dtype)
    w_vd = jnp.concatenate([w_in[..., M_GT - RW_TAIL:M_GT],
                            jnp.concatenate([z(32, 1), v_down.astype(w_in.dtype)], 0), z(96)], -1)
    w_x = jnp.concatenate([rot(rq), rot(rk), rot(aq), dup(ak), dup(rot(ak)), dup(av)], -1)
    assert w_x.shape == (depth, D, N_X) and w_vd.shape == (depth, D, 2 * LANES), (w_x.shape, w_vd.shape)
    return w_x, w_vd


def kernel(x, c, ctx, c_ctx, ada_w, ada_b, norm_w, ffn1_w_in, ffn1_w_out, ffn2_w_in, ffn2_w_out, mix_w_in, ret_decay_logit, att_q_norm, att_k_norm, att_sink, rwkv_shift, rwkv_w0, rwkv_w_up, rwkv_a0, rwkv_a_up, rwkv_g_up, rwkv_k_k, rwkv_k_a, rwkv_r_k, rwkv_v0, rwkv_v_down, rwkv_v_up, rwkv_gn_w, rwkv_gn_b, w_branch_ret, w_branch_att, w_branch_rwkv, w_out):
    bsz = x.shape[0]
    depth = ada_w.shape[0]
    ct = bsz * N_CTX
    nt = ct + bsz * N_LAT
    tm = math.gcd(1024, ct)
    ffn_tm = math.gcd(2048, ct)
    tb = {k: jnp.asarray(v) for k, v in _tables().items()}
    bo = tb["bo64"].astype(BF16)
    rcs = jnp.concatenate([tb["rcos"], tb["rsin"]], -1)
    stack_split = lambda w: jnp.stack(_split(w))

    xs = jnp.concatenate([ctx.reshape(ct, D), x.reshape(bsz * N_LAT, D)], 0)
    rows = 8 * ((bsz + 1 + 7) // 8)
    cond = jnp.zeros((rows, D), F32).at[0].set(c_ctx).at[1:bsz + 1].set(c)
    i2, _ = _rot_cols(2)
    vfirst = None
    f1_in, f1_out, f2_in, f2_out = ffn1_w_in, ffn1_w_out, ffn2_w_in, ffn2_w_out
    mix_b = mix_w_in.astype(BF16)
    w_x, w_vd = _derived_weights(mix_b, rwkv_v_down)
    n_mix = mix_b.shape[2]
    assert M_RW % (M_GT - M_RW) == 0
    wb_ret, wb_att = w_branch_ret.astype(BF16), w_branch_att.astype(BF16)
    wb_rw, wb_out = w_branch_rwkv.astype(BF16), w_out.astype(BF16)
    for l in range(depth):
        last = l == depth - 1
        mod = _ada_call(cond, ada_w, ada_b[l], l).reshape(rows, N_ADA, D)
        xs = _ffn_call(xs, mod, norm_w[l, 0], f1_in, f1_out, l, base=0, ct=ct, tm=ffn_tm, tile0=0,
                       n_tiles=nt // ffn_tm)

        proj_rw = _proj_call(xs, mod, norm_w[l, 1], mix_b, (M_GT - M_RW, M_RW // (M_GT - M_RW)), w_vd, l,
                             plan=PLAN_RW, n_cols=N_RW, ct=ct, tm=tm, out_dtype=F32)
        proj = _proj_call(xs, mod, norm_w[l, 1], mix_b, (n_mix, 0), w_x, l,
                          plan=PLAN_REST, n_cols=N_REST, ct=ct, tm=tm // 2, out_dtype=BF16)

        log_g = jax.nn.log_sigmoid(ret_decay_logit[l].astype(F32))
        oret = _ret_call(proj, log_g, rcs, tb["dist"], tb["ze"], tb["xe"], bsz=bsz)

        q_cols = 256 * ATT_KV_PER_STEP
        qw = jnp.tile(att_q_norm[l], q_cols // HEAD64).reshape(1, q_cols)
        qwr = jnp.tile(att_q_norm[l][i2[:64]], q_cols // HEAD64).reshape(1, q_cols)
        kw = jnp.tile(att_k_norm[l], 2).reshape(1, 128)
        kwr = jnp.tile(att_k_norm[l][i2[:64]], 2).reshape(1, 128)
        att = _att_call(proj, att_sink[l].astype(F32), tb["qcos"], tb["qsin"], tb["kcos"], tb["ksin"],
                        qw, qwr, kw, kwr, bo[:q_cols, :q_cols], bsz=bsz)

        shift = jnp.concatenate([rwkv_shift[l], jnp.tile(jnp.array([[0.0], [1.0], [0.0]], F32), (1, 128))], -1)
        wlo = jnp.zeros((2, LANES, 1024), F32)
        for d in range(2):
            wlo = wlo.at[0, d * 64:(d + 1) * 64, d * 512:(d + 1) * 512].set(rwkv_w_up[l, d])
            wlo = wlo.at[1, d * 64:(d + 1) * 64, d * 512:(d + 1) * 512].set(rwkv_a_up[l, d])
        if l > 0:
            vup = jnp.zeros((128, 512), F32).at[0:32].set(rwkv_v_up[l - 1])
            v0 = rwkv_v0[l - 1]
        else:
            vup = jnp.zeros((128, 512), F32)
            v0 = jnp.zeros((512,), F32)
        vec = jnp.stack([rwkv_k_k[l], rwkv_k_a[l], rwkv_r_k[l].reshape(512), v0,
                         rwkv_w0[l, 0], rwkv_w0[l, 1], rwkv_a0[l, 0], rwkv_a0[l, 1]])
        vf_in = vfirst if l > 0 else proj_rw
        r, v, kk, g, bonus, lw, kd, bv = _rw_prep_call(proj_rw, shift, stack_split(wlo), stack_split(rwkv_g_up[l]),
                                                        stack_split(vup), bo, vec, vf_in, ct=ct, mix_v=l > 0)
        if l == 0:
            vfirst = v
        y = _rw_scan_call(r, v, kk, lw, kd, bv, tb["lcum"], tb["mbig"], tb["lvl"], tb["eye"], bsz=bsz)

        gn = jnp.stack([rwkv_gn_w[l], rwkv_gn_b[l]])
        tile0 = ct // MERGE_TM if last else 0
        xs = _merge_call(xs, mod, proj, oret, att, y, bonus, g, bo, gn, wb_ret, wb_att, wb_rw, wb_out, l,
                         ct=ct, tm=MERGE_TM, tile0=tile0, n_tiles=nt // MERGE_TM - tile0)
        xs = _ffn_call(xs, mod, norm_w[l, 2], f2_in, f2_out, l, base=6, ct=0 if last else ct, tm=ffn_tm, tile0=0,
                       n_tiles=xs.shape[0] // ffn_tm)
    return xs.reshape(bsz, N_LAT, D)
```

```python
import functools
import math

import numpy as np
import jax
import jax.numpy as jnp
from jax import lax
from jax.experimental import pallas as pl
from jax.experimental.pallas import tpu as pltpu

F32 = jnp.float32
BF16 = jnp.bfloat16
HI = lax.Precision.HIGHEST

D = 1024
N_LAT = 2048
N_CTX = 256
GRID_W = 64
N_ADA = 9
D_FF = 2816
NORM_EPS = 1e-6
RET_CHUNK = 128
ATT_BLOCK = 128
WINDOW = 128
RW_CHUNK = 64
RWKV_GN_EPS = 64e-5
RWKV_DECAY_SCALE = 0.6065306597126334
HEAD64 = 64
LANES = 128

N_RW = 2048
C_RQ = 0
C_RV = 2048
C_RG = 3072
C_AQ = 4096
C_AK = 5120
C_AV = 5632
C_GT = 6144
N_REST = 9216

VMEM_LIMIT = 56 * 1024 * 1024


def _dot(a, b, prec=None):
    return jnp.dot(a, b, preferred_element_type=F32, precision=prec)


def _dot_nt(a, b, prec=None):
    return lax.dot_general(a, b, (((1,), (1,)), ((), ())), preferred_element_type=F32, precision=prec)


def _dot_tn(a, b, prec=None):
    return lax.dot_general(a, b, (((0,), (0,)), ((), ())), preferred_element_type=F32, precision=prec)


NN = (((1,), (0,)), ((), ()))
NT = (((1,), (1,)), ((), ()))


def _split(x, terms=2):
    out = []
    for _ in range(terms):
        piece = x.astype(BF16)
        out.append(piece)
        x = x - piece.astype(F32)
    return tuple(out)


def _mm(a, b, dims=NN):
    acc = None
    for i, ai in enumerate(a):
        for j, bj in enumerate(b):
            if i + j < max(len(a), len(b)):
                term = lax.dot_general(ai, bj, dims, preferred_element_type=F32)
                acc = term if acc is None else acc + term
    return acc


def _sigmoid(x):
    return 0.5 * jnp.tanh(0.5 * x) + 0.5


def _params(sem):
    return pltpu.CompilerParams(dimension_semantics=sem, vmem_limit_bytes=VMEM_LIMIT)


def _ada_kernel(c_ref, w_ref, b_ref, o_ref):
    c = c_ref[...]
    o_ref[...] = _dot(c * _sigmoid(c), w_ref[...], HI) + b_ref[...]


def _ada_call(cond, w, b, layer):
    rows = cond.shape[0]
    tn = 1024
    n = w.shape[2]
    return pl.pallas_call(
        _ada_kernel,
        grid=(n // tn,),
        in_specs=[pl.BlockSpec((rows, D), lambda j: (0, 0)),
                  pl.BlockSpec((None, D, tn), lambda j: (layer, 0, j)),
                  pl.BlockSpec((1, tn), lambda j: (0, j))],
        out_specs=pl.BlockSpec((rows, tn), lambda j: (0, j)),
        out_shape=jax.ShapeDtypeStruct((rows, n), F32),
        compiler_params=_params(("arbitrary",)),
        name="ada",
    )(cond, w, b.reshape(1, n))


def _mod_row(tok0, ct):
    return jnp.where(tok0 < ct, 0, 1 + (tok0 - ct) // N_LAT)


def _norm_mod(x, nw, shift, scale):
    y = x * lax.rsqrt(jnp.mean(x * x, -1, keepdims=True) + NORM_EPS) * nw
    return y * (1.0 + scale) + shift


def _ffn_kernel(x_ref, mod_ref, nw_ref, wg_ref, wu_ref, wo_ref, o_ref, h_sc, acc_sc, *, base, n_ff):
    j = pl.program_id(1)

    @pl.when(j == 0)
    def _():
        h = _norm_mod(x_ref[...], nw_ref[...], mod_ref[0, base:base + 1, :], mod_ref[0, base + 1:base + 2, :])
        h_sc[...] = h.astype(BF16)
        acc_sc[...] = jnp.zeros_like(acc_sc)

    h = h_sc[...]
    g = _dot(h, wg_ref[...].astype(BF16))
    u = _dot(h, wu_ref[...].astype(BF16))
    act = g * _sigmoid(g) * u
    acc_sc[...] += _dot(act.astype(BF16), wo_ref[...].astype(BF16))

    @pl.when(j == n_ff - 1)
    def _():
        o_ref[...] = x_ref[...] + 0.5 * mod_ref[0, base + 2:base + 3, :] * acc_sc[...]


def _ffn_call(x, mod, nw, w_in, w_out, layer, *, base, ct, tm, tile0, n_tiles):
    tf = 256
    n_ff = D_FF // tf
    nt = x.shape[0]
    kern = functools.partial(_ffn_kernel, base=base, n_ff=n_ff)
    return pl.pallas_call(
        kern,
        grid=(n_tiles, n_ff),
        in_specs=[pl.BlockSpec((tm, D), lambda i, j: (i + tile0, 0)),
                  pl.BlockSpec((1, N_ADA, D), lambda i, j: (_mod_row((i + tile0) * tm, ct), 0, 0)),
                  pl.BlockSpec((1, D), lambda i, j: (0, 0)),
                  pl.BlockSpec((None, D, tf), lambda i, j: (layer, 0, j)),
                  pl.BlockSpec((None, D, tf), lambda i, j: (layer, 0, j + n_ff)),
                  pl.BlockSpec((None, tf, D), lambda i, j: (layer, j, 0))],
        out_specs=pl.BlockSpec((tm, D), lambda i, j: (i, 0)),
        out_shape=jax.ShapeDtypeStruct((n_tiles * tm, D), F32),
        scratch_shapes=[pltpu.VMEM((tm, D), BF16), pltpu.VMEM((tm, D), F32)],
        compiler_params=_params(("parallel", "arbitrary")),
        name="ffn",
    )(x, mod, nw.reshape(1, D), w_in, w_in, w_out)


PROJ_TN = 512
MIX_OFF = np.cumsum([0, 512, 512, 1024, 1024, 512, 128, 128, 1920, 3072])
M_RQ, M_RK, M_RV, M_RG, M_AQ, M_AK, M_AV, M_RW, M_GT = (int(v) for v in MIX_OFF[:9])
X_RQ, X_RK, X_AQ, X_AK, X_AKR, X_AV, N_X = 0, 512, 1024, 1536, 1792, 2048, 2304
PLAN_REST = ((0, M_RQ, 512, C_RQ), (1, X_RQ, 512, C_RQ + 512), (0, M_RK, 512, C_RQ + 1024), (1, X_RK, 512, C_RQ + 1536),
             (0, M_RV, 1024, C_RV), (0, M_RG, 1024, C_RG),
             (0, M_AQ, 512, C_AQ), (1, X_AQ, 512, C_AQ + 512),
             (1, X_AK, 256, C_AK), (1, X_AKR, 256, C_AK + 256), (1, X_AV, 256, C_AV), (None, 0, 256, C_AV + 256),
             (0, M_GT, 3072, C_GT))
RW_TAIL = 128
PLAN_RW = ((0, 0, 1920 - RW_TAIL, 0), (1, 0, 2 * LANES, 1920 - RW_TAIL))


def _proj_kernel(x_ref, mod_ref, nw_ref, wa_ref, wb_ref, o_ref, *, plan):
    h = _norm_mod(x_ref[...], nw_ref[...], mod_ref[0, 3:4, :], mod_ref[0, 4:5, :]).astype(BF16)
    for which, src, width, dst in plan:
        for c in range(0, width, PROJ_TN):
            w = min(PROJ_TN, width - c)
            if which is None:
                o_ref[:, dst + c:dst + c + w] = jnp.zeros((x_ref.shape[0], w), o_ref.dtype)
            else:
                w_ref = (wa_ref, wb_ref)[which]
                o_ref[:, dst + c:dst + c + w] = _dot(h, w_ref[:, src + c:src + c + w]).astype(o_ref.dtype)


def _proj_call(x, mod, nw, wa, wa_block, wb, layer, *, plan, n_cols, ct, tm, out_dtype):
    nt = x.shape[0]
    resident = lambda width, col: pl.BlockSpec((None, D, width), lambda i: (layer, 0, col),
                                               pipeline_mode=pl.Buffered(1))
    return pl.pallas_call(
        functools.partial(_proj_kernel, plan=plan),
        grid=(nt // tm,),
        in_specs=[pl.BlockSpec((tm, D), lambda i: (i, 0)),
                  pl.BlockSpec((1, N_ADA, D), lambda i: (_mod_row(i * tm, ct), 0, 0)),
                  pl.BlockSpec((1, D), lambda i: (0, 0)),
                  resident(*wa_block), resident(wb.shape[2], 0)],
        out_specs=pl.BlockSpec((tm, n_cols), lambda i: (i, 0)),
        out_shape=jax.ShapeDtypeStruct((nt, n_cols), out_dtype),
        compiler_params=_params(("parallel",)),
        name="proj",
    )(x, mod, nw.reshape(1, D), wa, wb)


def _scan_blocks(b, d, s, *, nc_ctx, nc_lat, bsz):
    in_ctx = s < nc_ctx
    cs = jnp.where(d == 0, s, nc_ctx - 1 - s)
    ls = jnp.where(d == 0, s - nc_ctx, nc_lat - 1 - (s - nc_ctx))
    tok = jnp.where(in_ctx, b * nc_ctx + cs, bsz * nc_ctx + b * nc_lat + ls)
    pos = jnp.where(in_ctx, cs, nc_ctx + ls)
    return tok, pos


RET_STEP_CHUNKS = 2


def _ret_kernel(lg_ref, qkv_ref, cs_ref, dist_ref, ze_ref, xe_ref, o_ref, st_sc, dec_sc, xi_sc, zeta_sc):
    d = pl.program_id(1)
    s = pl.program_id(2)

    heads = range(8)

    @pl.when(s == 0)
    def _():
        st_sc[...] = jnp.zeros_like(st_sc)
        dist = dist_ref[...]
        for h in heads:
            lg = lg_ref[d, h]
            dec_sc[h] = jnp.where(dist >= 0.0, jnp.exp(lg * dist), 0.0)
            xi_sc[h] = jnp.exp(lg * xe_ref[...])
            zeta_sc[h] = jnp.exp(lg * ze_ref[...])

    c = RET_CHUNK
    lane = lax.broadcasted_iota(jnp.int32, (1, LANES), 1)
    hmask = [((lane // HEAD64) == e).astype(F32) for e in range(2)]
    starts = [pl.multiple_of(jnp.where(d == 0, i, RET_STEP_CHUNKS - 1 - i) * c, c) for i in range(RET_STEP_CHUNKS)]
    qms, kps, vhs = [], [], []
    for st0 in starts:
        rows = pl.ds(st0, c)
        cos = cs_ref[rows, 0:512]
        sin = cs_ref[rows, 512:1024]
        q = qkv_ref[rows, 0:512] * cos + qkv_ref[rows, 512:1024] * sin
        k = (qkv_ref[rows, 1024:1536] * cos + qkv_ref[rows, 1536:2048] * sin) * (HEAD64 ** -0.5)
        kps.append([k[:, p * LANES:(p + 1) * LANES].astype(BF16) for p in range(4)])
        qms.append([(q[:, (h // 2) * LANES:(h // 2 + 1) * LANES] * hmask[h % 2]).astype(BF16) for h in heads])
        vhs.append([qkv_ref[rows, C_RV + h * LANES:C_RV + (h + 1) * LANES] for h in heads])
    chunks = range(RET_STEP_CHUNKS)
    scs = [[(_dot_nt(qms[i][h], kps[i][h // 2]) * dec_sc[h]).astype(BF16) for h in heads] for i in chunks]
    intra = [[_dot(scs[i][h], vhs[i][h].astype(BF16)) for h in heads] for i in chunks]
    us = [[_dot_tn(kps[i][h // 2], (vhs[i][h] * zeta_sc[h]).astype(BF16)) for h in heads] for i in chunks]
    chunk_len = jnp.full((1, LANES), float(c), F32)
    decay = [jnp.exp(lg_ref[d, h] * chunk_len) for h in heads]
    sts = [st_sc[h] for h in heads]
    for i in chunks:
        inter = [_dot(qms[i][h], sts[h].astype(BF16)) * xi_sc[h] for h in heads]
        for h in heads:
            o_ref[pl.ds(starts[i], c), h * LANES:(h + 1) * LANES] = (intra[i][h] + inter[h]).astype(o_ref.dtype)
        sts = [sts[h] * decay[h] + us[i][h] for h in heads]
    for h in heads:
        st_sc[h] = sts[h]


def _ret_call(proj, log_g, cos_sin, dist, ze, xe, *, bsz):
    nt = proj.shape[0]
    c = RET_CHUNK
    rows = RET_STEP_CHUNKS * c
    nc_ctx, nc_lat = N_CTX // rows, N_LAT // rows
    steps = nc_ctx + nc_lat
    blocks = functools.partial(_scan_blocks, nc_ctx=nc_ctx, nc_lat=nc_lat, bsz=bsz)
    assert C_RQ == 0 and C_RV == 2048 and C_RG == 3072
    pos_map = lambda b, d, s, lg: (blocks(b, d, s)[1], 0)
    dir_map = lambda b, d, s, lg: (d, 0, 0)
    grid_spec = pltpu.PrefetchScalarGridSpec(
        num_scalar_prefetch=1,
        grid=(bsz, 2, steps),
        in_specs=[pl.BlockSpec((rows, C_RG), lambda b, d, s, lg: (blocks(b, d, s)[0], 0)),
                  pl.BlockSpec((rows, 1024), pos_map),
                  pl.BlockSpec((None, c, c), dir_map),
                  pl.BlockSpec((None, c, LANES), dir_map),
                  pl.BlockSpec((None, c, LANES), dir_map)],
        out_specs=pl.BlockSpec((None, rows, 1024), lambda b, d, s, lg: (d, blocks(b, d, s)[0], 0)),
        scratch_shapes=[pltpu.VMEM((8, LANES, LANES), F32), pltpu.VMEM((8, c, c), F32),
                        pltpu.VMEM((8, c, LANES), F32), pltpu.VMEM((8, c, LANES), F32)],
    )
    return pl.pallas_call(
        _ret_kernel,
        grid_spec=grid_spec,
        out_shape=jax.ShapeDtypeStruct((2, nt, 1024), BF16),
        compiler_params=_params(("parallel", "parallel", "arbitrary")),
        name="retention",
    )(log_g, proj, cos_sin, dist, ze, xe)


NEG = -1e30


ATT_KV_PER_STEP = 2


def _att_kernel(sink_ref, q_ref, qr_ref, qcos_ref, qsin_ref, qw_ref, qwr_ref, bo_ref,
                kl_ref, klr_ref, vl_ref, kcos_ref, ksin_ref, kc_ref, vc_ref, kw_ref, kwr_ref,
                o_ref, kn_sc, kcn_sc, *, n_ctx_blk):
    hk0 = pl.program_id(1) * ATT_KV_PER_STEP
    jb = pl.program_id(2)
    blk = ATT_BLOCK
    kv_lanes = [slice(e * LANES, (e + 1) * LANES) for e in range(ATT_KV_PER_STEP)]

    @pl.when(jb == 0)
    def _():
        for sl in kv_lanes:
            kl = kl_ref[:, sl].astype(F32)
            rs = lax.rsqrt(jnp.sum(kl * kl, -1, keepdims=True) * (0.5 / HEAD64) + NORM_EPS)
            kn = ((kl * kw_ref[...]) * kcos_ref[...]
                  + (klr_ref[:, sl].astype(F32) * kwr_ref[...]) * ksin_ref[...]) * rs
            kn_sc[:, sl] = kn.astype(BF16)
            kc = kc_ref[:, sl].astype(F32)
            rc = lax.rsqrt(jnp.sum(kc * kc, -1, keepdims=True) * (0.5 / HEAD64) + NORM_EPS)
            kcn_sc[:, sl] = (kc * kw_ref[...] * rc).astype(BF16)

    q = q_ref[...].astype(F32)
    mean_sq = _mm(_split(q * q), (bo_ref[...],)) * (1.0 / HEAD64)
    rs = lax.rsqrt(mean_sq + NORM_EPS)
    qn = ((q * qw_ref[...]) * qcos_ref[...]
          + (qr_ref[...].astype(F32) * qwr_ref[...]) * qsin_ref[...]) * rs * (HEAD64 ** -0.5)

    reach = jnp.where(jb >= n_ctx_blk, WINDOW, -1)
    lb = jnp.maximum(jb - n_ctx_blk, 0)
    n_win = 3 * blk
    start = jnp.clip((lb - 1) * blk, 0, N_LAT - n_win)
    start = pl.multiple_of(start, blk)
    n_keys = n_win + N_CTX
    kall = [jnp.concatenate([kn_sc[pl.ds(start, n_win), sl], kcn_sc[:, sl]], axis=0) for sl in kv_lanes]
    vall = [jnp.concatenate([vl_ref[pl.ds(start, n_win), sl], vc_ref[:, sl]], axis=0) for sl in kv_lanes]
    qpos = lb * blk + lax.broadcasted_iota(jnp.int32, (blk, n_keys), 0)
    col = lax.broadcasted_iota(jnp.int32, (blk, n_keys), 1)
    valid = (col >= n_win) | (jnp.abs(start + col - qpos) <= reach)
    lane = lax.broadcasted_iota(jnp.int32, (1, LANES), 1)
    hmask = [((lane // HEAD64) == e).astype(F32) for e in range(2)]
    groups = range(4 * ATT_KV_PER_STEP)
    sinks = [sink_ref[hk0 * 4 + g] for g in groups]
    qgs = [(qn[:, (g // 2) * LANES:(g // 2 + 1) * LANES] * hmask[g % 2]).astype(BF16) for g in groups]
    ss = [jnp.where(valid, _dot_nt(qgs[g], kall[g // 4]), NEG) for g in groups]
    ms = [jnp.maximum(jnp.max(ss[g], -1, keepdims=True), sinks[g]) for g in groups]
    ps = [jnp.exp(ss[g] - ms[g]) for g in groups]
    dens = [jnp.sum(ps[g], -1, keepdims=True) + jnp.exp(sinks[g] - ms[g]) for g in groups]
    ogs = [_dot(ps[g].astype(BF16), vall[g // 4]) / dens[g] * hmask[g % 2] for g in groups]
    o_ref[...] = jnp.concatenate([ogs[2 * p] + ogs[2 * p + 1] for p in range(2 * ATT_KV_PER_STEP)],
                                 axis=-1).astype(o_ref.dtype)


def _att_call(proj, sink, qcos, qsin, kcos, ksin, qw, qwr, kw, kwr, bo, *, bsz):
    nt = proj.shape[0]
    blk = ATT_BLOCK
    n_ctx_blk, n_lat_blk = N_CTX // blk, N_LAT // blk
    steps = n_ctx_blk + n_lat_blk
    qwid = 256 * ATT_KV_PER_STEP
    kwid = LANES * ATT_KV_PER_STEP

    def qtok(b, jb):
        return jnp.where(jb < n_ctx_blk, b * n_ctx_blk + jb, bsz * n_ctx_blk + b * n_lat_blk + (jb - n_ctx_blk))

    def qmap(col):
        return lambda b, hk, jb, sk: (qtok(b, jb), col // qwid + hk)

    qpos_map = lambda b, hk, jb, sk: (jb, 0)
    const = lambda b, hk, jb, sk: (0, 0)
    ct_rows = bsz * N_CTX

    def lat_map(col):
        return lambda b, hk, jb, sk: (ct_rows // N_LAT + b, col // kwid + hk)

    def ctx_map(col):
        return lambda b, hk, jb, sk: (b, col // kwid + hk)

    grid_spec = pltpu.PrefetchScalarGridSpec(
        num_scalar_prefetch=1,
        grid=(bsz, 2 // ATT_KV_PER_STEP, steps),
        in_specs=[pl.BlockSpec((blk, qwid), qmap(C_AQ)),
                  pl.BlockSpec((blk, qwid), qmap(C_AQ + 512)),
                  pl.BlockSpec((blk, qwid), qpos_map),
                  pl.BlockSpec((blk, qwid), qpos_map),
                  pl.BlockSpec((1, qwid), const),
                  pl.BlockSpec((1, qwid), const),
                  pl.BlockSpec((qwid, qwid), const),
                  pl.BlockSpec((N_LAT, kwid), lat_map(C_AK)),
                  pl.BlockSpec((N_LAT, kwid), lat_map(C_AK + 256)),
                  pl.BlockSpec((N_LAT, kwid), lat_map(C_AV)),
                  pl.BlockSpec((N_LAT, LANES), const),
                  pl.BlockSpec((N_LAT, LANES), const),
                  pl.BlockSpec((N_CTX, kwid), ctx_map(C_AK)),
                  pl.BlockSpec((N_CTX, kwid), ctx_map(C_AV)),
                  pl.BlockSpec((1, LANES), const),
                  pl.BlockSpec((1, LANES), const)],
        out_specs=pl.BlockSpec((blk, qwid), lambda b, hk, jb, sk: (qtok(b, jb), hk)),
        scratch_shapes=[pltpu.VMEM((N_LAT, kwid), BF16), pltpu.VMEM((N_CTX, kwid), BF16)],
    )
    kern = functools.partial(_att_kernel, n_ctx_blk=n_ctx_blk)
    return pl.pallas_call(
        kern,
        grid_spec=grid_spec,
        out_shape=jax.ShapeDtypeStruct((nt, 512), BF16),
        compiler_params=_params(("parallel", "parallel", "arbitrary")),
        name="attention",
    )(sink, proj, proj, qcos, qsin, qw, qwr, bo, proj, proj, proj, kcos, ksin, proj, proj, kw, kwr)


RW_TM = 256
RW_PREP_GROUPS = 2


def _rw_prep_kernel(cur_ref, prev_ref, next_ref, shift_ref, wlo_ref, gup_ref, vup_ref, bo_ref, vec_ref, vf_ref,
                    r_o, v_o, kk_o, g_o, bonus_o, lw_o, kd_o, bv_o, *, ct, mix_v):
    i = pl.program_id(0)
    tm = RW_TM
    tok0 = i * tm
    lat_off = tok0 - ct
    is_start = jnp.where(tok0 < ct, tok0 % N_CTX == 0, lat_off % N_LAT == 0)
    is_end = jnp.where(tok0 < ct, (tok0 + tm) % N_CTX == 0, (lat_off + tm) % N_LAT == 0)
    cur = cur_ref[...]
    row = lax.broadcasted_iota(jnp.int32, (tm, 1), 0)
    prev_row = jnp.where(is_start, 0.0, prev_ref[7:8, :])
    next_row = jnp.where(is_end, 0.0, next_ref[0:1, :])
    prev = jnp.where(row == 0, prev_row, pltpu.roll(cur, 1, 0))
    nxt = jnp.where(row == tm - 1, next_row, pltpu.roll(cur, tm - 1, 0))
    c = prev * shift_ref[0:1, :] + cur * shift_ref[1:2, :] + nxt * shift_ref[2:3, :]

    vec = vec_ref[...]
    k_k, k_a, r_k, v0 = vec[0:1], vec[1:2], vec[2:3], vec[3:4]
    bo = (bo_ref[...],)
    rows = [slice(i * (tm // RW_PREP_GROUPS), (i + 1) * (tm // RW_PREP_GROUPS)) for i in range(RW_PREP_GROUPS)]
    grp = range(RW_PREP_GROUPS)
    r = [c[rw, 0:512] for rw in rows]
    k = [c[rw, 512:1024] for rw in rows]
    v = [c[rw, 1024:1536] for rw in rows]
    lo = [c[rw, 1536:1792] for rw in rows]
    gd = [c[rw, 1792:1920] for rw in rows]
    vl = [c[rw, 1920:2048] for rw in rows]
    if mix_v:
        mixv = [_sigmoid(v0 + _mm(_split(vl[i]), (vup_ref[0], vup_ref[1]))) for i in grp]
        v = [v[i] + (vf_ref[rows[i], :].astype(F32) - v[i]) * mixv[i] for i in grp]
    g = [_mm(_split(_sigmoid(gd[i])), (gup_ref[0], gup_ref[1])) for i in grp]
    kk = [k[i] * k_k for i in grp]
    ss = [_mm(_split(kk[i] * kk[i]), bo) for i in grp]
    kk = [kk[i] / jnp.maximum(jnp.sqrt(ss[i]), 1e-12) for i in grp]
    z_lora = [_mm(_split(jnp.tanh(lo[i][:, 0:LANES])), (wlo_ref[0, 0], wlo_ref[1, 0])) for i in grp]
    a_lora = [_mm(_split(lo[i][:, LANES:2 * LANES]), (wlo_ref[0, 1], wlo_ref[1, 1])) for i in grp]
    ksum = [jnp.zeros_like(k[i]) for i in grp]
    for d in range(2):
        a = [_sigmoid(vec[6 + d:7 + d] + a_lora[i][:, d * 512:(d + 1) * 512]) for i in grp]
        kd = [k[i] * (1.0 + (a[i] - 1.0) * k_a) for i in grp]
        for i in grp:
            lw_o[d, rows[i], :] = -RWKV_DECAY_SCALE * _sigmoid(vec[4 + d:5 + d] + z_lora[i][:, d * 512:(d + 1) * 512])
            kd_o[d, rows[i], :] = kd[i].astype(kd_o.dtype)
            bv_o[d, rows[i], :] = (a[i] * kk[i]).astype(bv_o.dtype)
        ksum = [ksum[i] + kd[i] for i in grp]
    bonus = [_mm(_split(r[i] * ksum[i] * r_k), bo) * v[i] for i in grp]
    for i in grp:
        bonus_o[rows[i], :] = bonus[i]
        r_o[rows[i], :] = r[i].astype(r_o.dtype)
        v_o[rows[i], :] = v[i].astype(v_o.dtype)
        kk_o[rows[i], :] = kk[i].astype(kk_o.dtype)
        g_o[rows[i], :] = g[i]


def _rw_prep_call(proj, shift, wlo, gup, vup, bo, vec, vfirst, *, ct, mix_v):
    nt = proj.shape[0]
    tm = RW_TM
    nblk8 = nt // 8
    tok = pl.BlockSpec((tm, 512), lambda i: (i, 0))
    dirs = pl.BlockSpec((2, tm, 512), lambda i: (0, i, 0))
    const = lambda i: (0, 0)
    kern = functools.partial(_rw_prep_kernel, ct=ct, mix_v=mix_v)
    one = lambda dt: jax.ShapeDtypeStruct((nt, 512), dt)
    two = lambda dt: jax.ShapeDtypeStruct((2, nt, 512), dt)
    return pl.pallas_call(
        kern,
        grid=(nt // tm,),
        in_specs=[pl.BlockSpec((tm, N_RW), lambda i: (i, 0)),
                  pl.BlockSpec((8, N_RW), lambda i: (jnp.maximum(i * (tm // 8) - 1, 0), 0)),
                  pl.BlockSpec((8, N_RW), lambda i: (jnp.minimum((i + 1) * (tm // 8), nblk8 - 1), 0)),
                  pl.BlockSpec((3, N_RW), const),
                  pl.BlockSpec((2, 2, LANES, 1024), lambda i: (0, 0, 0, 0)),
                  pl.BlockSpec((2, 128, 512), lambda i: (0, 0, 0)),
                  pl.BlockSpec((2, 128, 512), lambda i: (0, 0, 0)),
                  pl.BlockSpec((512, 512), const),
                  pl.BlockSpec((8, 512), const),
                  tok],
        out_specs=[tok, tok, tok, tok, tok, dirs, dirs, dirs],
        out_shape=[one(BF16), one(BF16), one(BF16), one(F32), one(F32), two(F32), two(BF16), two(BF16)],
        compiler_params=_params(("parallel",)),
        name="rwkv_prep",
    )(proj, proj, proj, shift, wlo, gup, vup, bo, vec, vfirst)


SCAN_TERMS = dict(v=1, x=1, y=1, ak=1, ao=1, m=1, rhs=1, pq=1, rb=1, rk=1, bp=1, kp=1, st=1, yq=1, g=1)
RW_STEP_CHUNKS = 4


def _rw_scan_kernel(r_ref, v_ref, kk_ref, lw_ref, kd_ref, bv_ref, lcum_ref, mbig_ref, lvl_ref, eye_ref,
                    y_ref, st_sc):
    d = pl.program_id(1)
    s = pl.program_id(2)
    t = RW_CHUNK
    nt_ = SCAN_TERMS

    @pl.when(s == 0)
    def _():
        st_sc[...] = jnp.zeros_like(st_sc)

    lcum = (lcum_ref[...].astype(BF16),)
    mbig = mbig_ref[...]
    eye = eye_ref[...]
    lane = lax.broadcasted_iota(jnp.int32, (1, LANES), 1)
    m_e = (lane < HEAD64).astype(F32)
    m_o = 1.0 - m_e

    def stack2(x):
        return jnp.concatenate([x * m_e, x * m_o], axis=0)

    pairs = range(4)
    sls = [slice(p * LANES, (p + 1) * LANES) for p in pairs]
    starts = [pl.multiple_of(jnp.where(d == 0, c, RW_STEP_CHUNKS - 1 - c) * t, t) for c in range(RW_STEP_CHUNKS)]
    aw2, rt2, bi2, ki2, bp2t, kp2t, v2s, w_tot = [], [], [], [], [], [], [], []
    for st0 in starts:
        rows = pl.ds(st0, t)
        lw = lw_ref[rows, :]
        cum = _mm(lcum, _split(lw, 3))
        tot = jnp.sum(lw, axis=0, keepdims=True)
        kk, kd, bv = kk_ref[rows, :], kd_ref[rows, :], bv_ref[rows, :]
        aw = -kk * jnp.exp(cum - lw)
        rt = r_ref[rows, :] * jnp.exp(cum)
        e_inv = jnp.exp(-cum)
        e_rem = jnp.exp(tot - cum)
        v = v_ref[rows, :]
        aw2 += [stack2(aw[:, sl]) for sl in sls]
        rt2 += [stack2(rt[:, sl]) for sl in sls]
        bi2 += [stack2((bv * e_inv)[:, sl]) for sl in sls]
        ki2 += [stack2((kd * e_inv)[:, sl]) for sl in sls]
        bp2t += [stack2((bv * e_rem)[:, sl]).T for sl in sls]
        kp2t += [stack2((kd * e_rem)[:, sl]).T for sl in sls]
        v2s += [_split(stack2(v[:, sl]), nt_["v"]) for sl in sls]
        w_tot += [jnp.exp(tot)[:, sl] for sl in sls]

    items = range(RW_STEP_CHUNKS * 4)
    big = [_mm(_split(jnp.concatenate([aw2[i], rt2[i]], axis=0), nt_["x"]),
               _split(jnp.concatenate([bi2[i], ki2[i]], axis=0), nt_["y"]), NT) * mbig for i in items]
    a_ab = [b[0:2 * t, 0:2 * t] for b in big]
    a_ak = [b[0:2 * t, 2 * t:4 * t] for b in big]
    a_rb = [b[2 * t:4 * t, 0:2 * t] for b in big]
    a_rk = [b[2 * t:4 * t, 2 * t:4 * t] for b in big]
    akv = [_mm(_split(a_ak[i], nt_["ak"]), v2s[i]) for i in items]
    inv = [eye + a * lvl_ref[0] for a in a_ab]
    for j in range(1, int(math.log2(t))):
        ms = [_split(m, nt_["m"]) for m in inv]
        low = [_split(_mm(_split(a_ab[i] * lvl_ref[j], nt_["ao"]), ms[i]), nt_["m"]) for i in items]
        inv = [inv[i] + _mm(ms[i], low[i]) for i in items]
    pqs = [_split(_mm(_split(inv[i], nt_["m"]),
                      _split(jnp.concatenate([aw2[i], akv[i]], axis=1), nt_["rhs"])), nt_["pq"]) for i in items]
    tmp = [_mm(_split(a_rb[i], nt_["rb"]), pqs[i]) for i in items]
    ark_v = [_mm(_split(a_rk[i], nt_["rk"]), v2s[i]) for i in items]
    gu = [_mm(_split(bp2t[i], nt_["bp"]), pqs[i]) for i in items]
    kpv = [_mm(_split(kp2t[i], nt_["kp"]), v2s[i]) for i in items]
    yq = [_split(rt2[i] + tmp[i][:, 0:LANES], nt_["yq"]) for i in items]
    y0 = [tmp[i][:, LANES:2 * LANES] + ark_v[i] for i in items]
    g_mat = [_split(gu[i][:, 0:LANES] + eye * w_tot[i], nt_["g"]) for i in items]
    u_mat = [gu[i][:, LANES:2 * LANES] + kpv[i] for i in items]

    st = [st_sc[p] for p in pairs]
    for c in range(RW_STEP_CHUNKS):
        sts = [_split(st[p], nt_["st"]) for p in pairs]
        y2 = [_mm(yq[4 * c + p], sts[p]) + y0[4 * c + p] for p in pairs]
        st = [_mm(g_mat[4 * c + p], sts[p]) + u_mat[4 * c + p] for p in pairs]
        for p in pairs:
            y_ref[pl.ds(starts[c], t), sls[p]] = y2[p][0:t] + y2[p][t:2 * t]
    for p in pairs:
        st_sc[p] = st[p]


def _rw_scan_call(r, v, kk, lw, kd, bv, lcum, mbig, lvl, eye, *, bsz):
    nt = r.shape[0]
    t = RW_CHUNK
    rows = RW_STEP_CHUNKS * t
    nc_ctx, nc_lat = N_CTX // rows, N_LAT // rows
    steps = nc_ctx + nc_lat
    blocks = functools.partial(_scan_blocks, nc_ctx=nc_ctx, nc_lat=nc_lat, bsz=bsz)
    tok = pl.BlockSpec((rows, 512), lambda b, d, s: (blocks(b, d, s)[0], 0))
    tokd = pl.BlockSpec((None, rows, 512), lambda b, d, s: (d, blocks(b, d, s)[0], 0))
    n_lvl = lvl.shape[1]
    return pl.pallas_call(
        _rw_scan_kernel,
        grid=(bsz, 2, steps),
        in_specs=[tok, tok, tok, tokd, tokd, tokd,
                  pl.BlockSpec((None, t, t), lambda b, d, s: (d, 0, 0)),
                  pl.BlockSpec((None, 4 * t, 4 * t), lambda b, d, s: (d, 0, 0)),
                  pl.BlockSpec((None, n_lvl, 2 * t, 2 * t), lambda b, d, s: (d, 0, 0, 0)),
                  pl.BlockSpec((LANES, LANES), lambda b, d, s: (0, 0))],
        out_specs=tokd,
        out_shape=jax.ShapeDtypeStruct((2, nt, 512), F32),
        scratch_shapes=[pltpu.VMEM((4, LANES, LANES), F32)],
        compiler_params=_params(("parallel", "parallel", "arbitrary")),
        name="rwkv_scan",
    )(r, v, kk, lw, kd, bv, lcum, mbig, lvl, eye)


MERGE_TM = 512


def _merge_kernel(x_ref, mod_ref, oret_ref, rg_ref, att_ref, y_ref, bonus_ref, g_ref, gt_ref,
                  bo_ref, gn_ref, wr_ref, wa_ref, ww_ref, wo_ref, o_ref):
    o = oret_ref[0].astype(F32) + oret_ref[1].astype(F32)
    parts = []
    for h in range(8):
        oh = o[:, h * LANES:(h + 1) * LANES]
        mu = jnp.mean(oh, -1, keepdims=True)
        dv = oh - mu
        var = jnp.mean(dv * dv, -1, keepdims=True)
        parts.append(dv * lax.rsqrt(var + NORM_EPS))
    rg = rg_ref[...].astype(F32)
    ret = (rg * _sigmoid(rg)) * jnp.concatenate(parts, axis=-1)

    y = y_ref[0] + y_ref[1]
    bo = (bo_ref[...],)
    mu = _mm(_split(y), bo) * (1.0 / HEAD64)
    dy = y - mu
    var = _mm(_split(dy * dy), bo) * (1.0 / HEAD64)
    yn = dy * lax.rsqrt(var + RWKV_GN_EPS) * gn_ref[0:1, :] + gn_ref[1:2, :]
    rw = (yn + bonus_ref[...]) * g_ref[...]

    merged = (_sigmoid(gt_ref[:, 0:D].astype(F32)) * _dot(ret.astype(BF16), wr_ref[...])
              + _sigmoid(gt_ref[:, D:2 * D].astype(F32)) * _dot(att_ref[...].astype(BF16), wa_ref[...])
              + _sigmoid(gt_ref[:, 2 * D:3 * D].astype(F32)) * _dot(rw.astype(BF16), ww_ref[...]))
    out = _dot(merged.astype(BF16), wo_ref[...])
    o_ref[...] = x_ref[...] + mod_ref[0, 5:6, :] * out


def _merge_call(x, mod, proj, oret, att, y, bonus, g, bo, gn, wr, wa, ww, wo, layer, *, ct, tm, tile0, n_tiles):
    tok = lambda w, col=0: pl.BlockSpec((tm, w), lambda i: (i + tile0, col))
    tok2 = lambda w: pl.BlockSpec((2, tm, w), lambda i: (0, i + tile0, 0))
    const = lambda i: (0, 0)
    wspec = lambda rows: pl.BlockSpec((None, rows, D), lambda i: (layer, 0, 0), pipeline_mode=pl.Buffered(1))
    assert C_GT % (3 * D) == 0
    return pl.pallas_call(
        _merge_kernel,
        grid=(n_tiles,),
        in_specs=[tok(D),
                  pl.BlockSpec((1, N_ADA, D), lambda i: (_mod_row((i + tile0) * tm, ct), 0, 0)),
                  tok2(1024), tok(1024, C_RG // 1024), tok(512), tok2(512), tok(512), tok(512),
                  tok(3 * D, C_GT // (3 * D)),
                  pl.BlockSpec((512, 512), const),
                  pl.BlockSpec((2, 512), const),
                  wspec(1024), wspec(512), wspec(512), wspec(D)],
        out_specs=pl.BlockSpec((tm, D), lambda i: (i, 0)),
        out_shape=jax.ShapeDtypeStruct((n_tiles * tm, D), F32),
        compiler_params=_params(("parallel",)),
        name="merge",
    )(x, mod, oret, proj, att, y, bonus, g, proj, bo, gn, wr, wa, ww, wo)


def _rope_1d(pos, dim, base):
    n_freq = dim // 2
    inv = np.power(np.float32(base), -(np.arange(n_freq, dtype=np.float32) / np.float32(n_freq))).astype(np.float32)
    return pos.astype(np.float32)[:, None] * inv[None, :]


def _tables():
    ang = _rope_1d(np.arange(N_CTX + N_LAT), HEAD64, 10000.0)
    rcos = np.tile(np.concatenate([np.cos(ang), np.cos(ang)], -1), (1, 8)).astype(np.float32)
    rsin = np.tile(np.concatenate([np.sin(ang), np.sin(ang)], -1), (1, 8)).astype(np.float32)
    rows = N_LAT // GRID_W
    row = np.repeat(np.arange(rows), GRID_W)
    col = np.arange(rows * GRID_W) % GRID_W
    aang = np.concatenate([_rope_1d(row, HEAD64 // 2, 10000.0), _rope_1d(col, HEAD64 // 2, 10000.0)], -1)
    ac = np.concatenate([np.cos(aang), np.cos(aang)], -1).astype(np.float32)
    asn = np.concatenate([np.sin(aang), np.sin(aang)], -1).astype(np.float32)
    q_cols = 256 * ATT_KV_PER_STEP
    qcos = np.concatenate([np.ones((N_CTX, q_cols), np.float32), np.tile(ac, (1, q_cols // HEAD64))], 0)
    qsin = np.concatenate([np.zeros((N_CTX, q_cols), np.float32), np.tile(asn, (1, q_cols // HEAD64))], 0)
    kcos = np.tile(ac, (1, 2))
    ksin = np.tile(asn, (1, 2))
    c = RET_CHUNK
    pos = np.arange(c, dtype=np.float32)
    diff = pos[:, None] - pos[None, :]
    dist = np.stack([np.where(diff >= 0, diff, -1.0), np.where(diff <= 0, -diff, -1.0)]).astype(np.float32)
    ze = np.stack([c - 1.0 - pos, pos]).astype(np.float32)
    xe = np.stack([pos + 1.0, c - pos]).astype(np.float32)
    ze = np.broadcast_to(ze[:, :, None], (2, c, LANES)).copy()
    xe = np.broadcast_to(xe[:, :, None], (2, c, LANES)).copy()
    t = RW_CHUNK
    ti = np.arange(t)
    low_incl = (ti[None, :] <= ti[:, None]).astype(np.float32)
    low_strict = (ti[None, :] < ti[:, None]).astype(np.float32)
    lcum = np.stack([low_incl, low_incl.T])
    eye2 = np.eye(2, dtype=np.float32)
    mbig = []
    for strict, incl in ((low_strict, low_incl), (low_strict.T, low_incl.T)):
        s2 = np.kron(eye2, strict)
        i2 = np.kron(eye2, incl)
        mbig.append(np.block([[s2, s2], [i2, i2]]))
    mbig = np.stack(mbig).astype(np.float32)
    lvl = []
    for j in range(int(math.log2(t))):
        sz = 2 ** j
        same = (ti[:, None] // (2 * sz)) == (ti[None, :] // (2 * sz))
        m = (same & ((ti[:, None] % (2 * sz)) >= sz) & ((ti[None, :] % (2 * sz)) < sz)).astype(np.float32)
        lvl.append(np.kron(eye2, m))
    lvl = np.stack(lvl)
    lvl = np.stack([lvl, lvl.transpose(0, 2, 1)]).astype(np.float32)
    eye = np.eye(LANES, dtype=np.float32)
    bo64 = np.kron(np.eye(8, dtype=np.float32), np.ones((HEAD64, HEAD64), np.float32))
    return dict(rcos=rcos, rsin=rsin, qcos=qcos, qsin=qsin, kcos=kcos, ksin=ksin, dist=dist, ze=ze, xe=xe,
                lcum=lcum, mbig=mbig, lvl=lvl, eye=eye, bo64=bo64)


def _rot_cols(n_heads):
    half = HEAD64 // 2
    idx, sgn = [], []
    for h in range(n_heads):
        base = h * HEAD64
        idx += list(range(base + half, base + HEAD64)) + list(range(base, base + half))
        sgn += [-1.0] * half + [1.0] * half
    return np.array(idx), np.array(sgn, np.float32)


def _derived_weights(w_in, v_down):
    depth = w_in.shape[0]
    rq, rk = w_in[..., M_RQ:M_RK], w_in[..., M_RK:M_RV]
    aq, ak, av = w_in[..., M_AQ:M_AK], w_in[..., M_AK:M_AV], w_in[..., M_AV:M_RW]
    half = HEAD64 // 2
    sign = jnp.asarray([-1.0, 1.0], w_in.dtype).reshape(2, 1)

    def rot(m):
        heads = m.shape[-1] // HEAD64
        halves = m.reshape(depth, D, heads, 2, half)
        return (halves[:, :, :, ::-1, :] * sign).reshape(depth, D, heads * HEAD64)

    dup = lambda m: jnp.concatenate([m[..., 0:64], m[..., 0:64], m[..., 64:128], m[..., 64:128]], -1)
    z = lambda n, d=depth: jnp.zeros((d, D, n), w_in.dtype)
    w_vd = jnp.concatenate([w_in[..., M_GT - RW_TAIL:M_GT],
                            jnp.concatenate([z(32, 1), v_down.astype(w_in.dtype)], 0), z(96)], -1)
    w_x = jnp.concatenate([rot(rq), rot(rk), rot(aq), dup(ak), dup(rot(ak)), dup(av)], -1)
    assert w_x.shape == (depth, D, N_X) and w_vd.shape == (depth, D, 2 * LANES), (w_x.shape, w_vd.shape)
    return w_x, w_vd


def kernel(x, c, ctx, c_ctx, ada_w, ada_b, norm_w, ffn1_w_in, ffn1_w_out, ffn2_w_in, ffn2_w_out, mix_w_in, ret_decay_logit, att_q_norm, att_k_norm, att_sink, rwkv_shift, rwkv_w0, rwkv_w_up, rwkv_a0, rwkv_a_up, rwkv_g_up, rwkv_k_k, rwkv_k_a, rwkv_r_k, rwkv_v0, rwkv_v_down, rwkv_v_up, rwkv_gn_w, rwkv_gn_b, w_branch_ret, w_branch_att, w_branch_rwkv, w_out):
    bsz = x.shape[0]
    depth = ada_w.shape[0]
    ct = bsz * N_CTX
    nt = ct + bsz * N_LAT
    tm = math.gcd(1024, ct)
    ffn_tm = math.gcd(2048, ct)
    tb = {k: jnp.asarray(v) for k, v in _tables().items()}
    bo = tb["bo64"].astype(BF16)
    rcs = jnp.concatenate([tb["rcos"], tb["rsin"]], -1)
    stack_split = lambda w: jnp.stack(_split(w))

    xs = jnp.concatenate([ctx.reshape(ct, D), x.reshape(bsz * N_LAT, D)], 0)
    rows = 8 * ((bsz + 1 + 7) // 8)
    cond = jnp.zeros((rows, D), F32).at[0].set(c_ctx).at[1:bsz + 1].set(c)
    i2, _ = _rot_cols(2)
    vfirst = None
    f1_in, f1_out, f2_in, f2_out = ffn1_w_in, ffn1_w_out, ffn2_w_in, ffn2_w_out
    mix_b = mix_w_in.astype(BF16)
    w_x, w_vd = _derived_weights(mix_b, rwkv_v_down)
    n_mix = mix_b.shape[2]
    assert M_RW % (M_GT - M_RW) == 0
    wb_ret, wb_att = w_branch_ret.astype(BF16), w_branch_att.astype(BF16)
    wb_rw, wb_out = w_branch_rwkv.astype(BF16), w_out.astype(BF16)
    for l in range(depth):
        last = l == depth - 1
        mod = _ada_call(cond, ada_w, ada_b[l], l).reshape(rows, N_ADA, D)
        xs = _ffn_call(xs, mod, norm_w[l, 0], f1_in, f1_out, l, base=0, ct=ct, tm=ffn_tm, tile0=0,
                       n_tiles=nt // ffn_tm)

        proj_rw = _proj_call(xs, mod, norm_w[l, 1], mix_b, (M_GT - M_RW, M_RW // (M_GT - M_RW)), w_vd, l,
                             plan=PLAN_RW, n_cols=N_RW, ct=ct, tm=tm, out_dtype=F32)
        proj = _proj_call(xs, mod, norm_w[l, 1], mix_b, (n_mix, 0), w_x, l,
                          plan=PLAN_REST, n_cols=N_REST, ct=ct, tm=tm // 2, out_dtype=BF16)

        log_g = jax.nn.log_sigmoid(ret_decay_logit[l].astype(F32))
        oret = _ret_call(proj, log_g, rcs, tb["dist"], tb["ze"], tb["xe"], bsz=bsz)

        q_cols = 256 * ATT_KV_PER_STEP
        qw = jnp.tile(att_q_norm[l], q_cols // HEAD64).reshape(1, q_cols)
        qwr = jnp.tile(att_q_norm[l][i2[:64]], q_cols // HEAD64).reshape(1, q_cols)
        kw = jnp.tile(att_k_norm[l], 2).reshape(1, 128)
        kwr = jnp.tile(att_k_norm[l][i2[:64]], 2).reshape(1, 128)
        att = _att_call(proj, att_sink[l].astype(F32), tb["qcos"], tb["qsin"], tb["kcos"], tb["ksin"],
                        qw, qwr, kw, kwr, bo[:q_cols, :q_cols], bsz=bsz)

        shift = jnp.concatenate([rwkv_shift[l], jnp.tile(jnp.array([[0.0], [1.0], [0.0]], F32), (1, 128))], -1)
        wlo = jnp.zeros((2, LANES, 1024), F32)
        for d in range(2):
            wlo = wlo.at[0, d * 64:(d + 1) * 64, d * 512:(d + 1) * 512].set(rwkv_w_up[l, d])
            wlo = wlo.at[1, d * 64:(d + 1) * 64, d * 512:(d + 1) * 512].set(rwkv_a_up[l, d])
        if l > 0:
            vup = jnp.zeros((128, 512), F32).at[0:32].set(rwkv_v_up[l - 1])
            v0 = rwkv_v0[l - 1]
        else:
            vup = jnp.zeros((128, 512), F32)
            v0 = jnp.zeros((512,), F32)
        vec = jnp.stack([rwkv_k_k[l], rwkv_k_a[l], rwkv_r_k[l].reshape(512), v0,
                         rwkv_w0[l, 0], rwkv_w0[l, 1], rwkv_a0[l, 0], rwkv_a0[l, 1]])
        vf_in = vfirst if l > 0 else proj_rw
        r, v, kk, g, bonus, lw, kd, bv = _rw_prep_call(proj_rw, shift, stack_split(wlo), stack_split(rwkv_g_up[l]),
                                                        stack_split(vup), bo, vec, vf_in, ct=ct, mix_v=l > 0)
        if l == 0:
            vfirst = v
        y = _rw_scan_call(r, v, kk, lw, kd, bv, tb["lcum"], tb["mbig"], tb["lvl"], tb["eye"], bsz=bsz)

        gn = jnp.stack([rwkv_gn_w[l], rwkv_gn_b[l]])
        tile0 = ct // MERGE_TM if last else 0
        xs = _merge_call(xs, mod, proj, oret, att, y, bonus, g, bo, gn, wb_ret, wb_att, wb_rw, wb_out, l,
                         ct=ct, tm=MERGE_TM, tile0=tile0, n_tiles=nt // MERGE_TM - tile0)
        xs = _ffn_call(xs, mod, norm_w[l, 2], f2_in, f2_out, l, base=6, ct=0 if last else ct, tm=ffn_tm, tile0=0,
                       n_tiles=xs.shape[0] // ffn_tm)
    return xs.reshape(bsz, N_LAT, D)
```

```python
import functools
import math

import numpy as np
import jax
import jax.numpy as jnp
from jax import lax
from jax.experimental import pallas as pl
from jax.experimental.pallas import tpu as pltpu

F32 = jnp.float32
BF16 = jnp.bfloat16
HI = lax.Precision.HIGHEST

D = 1024
N_LAT = 2048
N_CTX = 256
GRID_W = 64
N_ADA = 9
D_FF = 2816
NORM_EPS = 1e-6
RET_CHUNK = 128
ATT_BLOCK = 128
WINDOW = 128
RW_CHUNK = 64
RWKV_GN_EPS = 64e-5
RWKV_DECAY_SCALE = 0.6065306597126334
HEAD64 = 64
LANES = 128

N_RW = 2048
C_RQ = 0
C_RV = 2048
C_RG = 3072
C_AQ = 4096
C_AK = 5120
C_AV = 5632
C_GT = 6144
N_REST = 9216

VMEM_LIMIT = 56 * 1024 * 1024


def _dot(a, b, prec=None):
    return jnp.dot(a, b, preferred_element_type=F32, precision=prec)


def _dot_nt(a, b, prec=None):
    return lax.dot_general(a, b, (((1,), (1,)), ((), ())), preferred_element_type=F32, precision=prec)


def _dot_tn(a, b, prec=None):
    return lax.dot_general(a, b, (((0,), (0,)), ((), ())), preferred_element_type=F32, precision=prec)


NN = (((1,), (0,)), ((), ()))
NT = (((1,), (1,)), ((), ()))


def _split(x, terms=2):
    out = []
    for _ in range(terms):
        piece = x.astype(BF16)
        out.append(piece)
        x = x - piece.astype(F32)
    return tuple(out)


def _mm(a, b, dims=NN):
    acc = None
    for i, ai in enumerate(a):
        for j, bj in enumerate(b):
            if i + j < max(len(a), len(b)):
                term = lax.dot_general(ai, bj, dims, preferred_element_type=F32)
                acc = term if acc is None else acc + term
    return acc


def _sigmoid(x):
    return 0.5 * jnp.tanh(0.5 * x) + 0.5


def _params(sem):
    return pltpu.CompilerParams(dimension_semantics=sem, vmem_limit_bytes=VMEM_LIMIT)


def _ada_kernel(c_ref, w_ref, b_ref, o_ref):
    c = c_ref[...]
    o_ref[...] = _dot(c * _sigmoid(c), w_ref[...], HI) + b_ref[...]


def _ada_call(cond, w, b):
    rows = cond.shape[0]
    tn = 1024
    depth, _, n = w.shape
    return pl.pallas_call(
        _ada_kernel,
        grid=(depth, n // tn),
        in_specs=[pl.BlockSpec((rows, D), lambda l, j: (0, 0)),
                  pl.BlockSpec((None, D, tn), lambda l, j: (l, 0, j)),
                  pl.BlockSpec((None, 1, tn), lambda l, j: (l, 0, j))],
        out_specs=pl.BlockSpec((None, rows, tn), lambda l, j: (l, 0, j)),
        out_shape=jax.ShapeDtypeStruct((depth, rows, n), F32),
        compiler_params=_params(("arbitrary", "arbitrary")),
        name="ada",
    )(cond, w, b.reshape(depth, 1, n))


def _mod_row(tok0, ct):
    return jnp.where(tok0 < ct, 0, 1 + (tok0 - ct) // N_LAT)


def _norm_mod(x, nw, shift, scale):
    y = x * lax.rsqrt(jnp.mean(x * x, -1, keepdims=True) + NORM_EPS) * nw
    return y * (1.0 + scale) + shift


def _ffn_kernel(x_ref, mod_ref, nw_ref, wg_ref, wu_ref, wo_ref, o_ref, h_sc, acc_sc, *, base, n_ff):
    j = pl.program_id(1)

    @pl.when(j == 0)
    def _():
        h = _norm_mod(x_ref[...], nw_ref[...], mod_ref[0, base:base + 1, :], mod_ref[0, base + 1:base + 2, :])
        h_sc[...] = h.astype(BF16)
        acc_sc[...] = jnp.zeros_like(acc_sc)

    h = h_sc[...]
    g = _dot(h, wg_ref[...].astype(BF16))
    u = _dot(h, wu_ref[...].astype(BF16))
    act = g * _sigmoid(g) * u
    acc_sc[...] += _dot(act.astype(BF16), wo_ref[...].astype(BF16))

    @pl.when(j == n_ff - 1)
    def _():
        o_ref[...] = x_ref[...] + 0.5 * mod_ref[0, base + 2:base + 3, :] * acc_sc[...]


def _ffn_call(x, mod, nw, w_in, w_out, layer, *, base, ct, tm, tile0, n_tiles):
    tf = 256
    n_ff = D_FF // tf
    nt = x.shape[0]
    kern = functools.partial(_ffn_kernel, base=base, n_ff=n_ff)
    return pl.pallas_call(
        kern,
        grid=(n_tiles, n_ff),
        in_specs=[pl.BlockSpec((tm, D), lambda i, j: (i + tile0, 0)),
                  pl.BlockSpec((1, N_ADA, D), lambda i, j: (_mod_row((i + tile0) * tm, ct), 0, 0)),
                  pl.BlockSpec((1, D), lambda i, j: (0, 0)),
                  pl.BlockSpec((None, D, tf), lambda i, j: (layer, 0, j)),
                  pl.BlockSpec((None, D, tf), lambda i, j: (layer, 0, j + n_ff)),
                  pl.BlockSpec((None, tf, D), lambda i, j: (layer, j, 0))],
        out_specs=pl.BlockSpec((tm, D), lambda i, j: (i, 0)),
        out_shape=jax.ShapeDtypeStruct((n_tiles * tm, D), F32),
        scratch_shapes=[pltpu.VMEM((tm, D), BF16), pltpu.VMEM((tm, D), F32)],
        compiler_params=_params(("parallel", "arbitrary")),
        name="ffn",
    )(x, mod, nw.reshape(1, D), w_in, w_in, w_out)


PROJ_TN = 512
MIX_OFF = np.cumsum([0, 512, 512, 1024, 1024, 512, 128, 128, 1920, 3072])
M_RQ, M_RK, M_RV, M_RG, M_AQ, M_AK, M_AV, M_RW, M_GT = (int(v) for v in MIX_OFF[:9])
X_RQ, X_RK, X_AQ, X_AK, X_AKR, X_AV, N_X = 0, 512, 1024, 1536, 1792, 2048, 2304
PLAN_REST = ((0, M_RQ, 512, C_RQ), (1, X_RQ, 512, C_RQ + 512), (0, M_RK, 512, C_RQ + 1024), (1, X_RK, 512, C_RQ + 1536),
             (0, M_RV, 1024, C_RV), (0, M_RG, 1024, C_RG),
             (0, M_AQ, 512, C_AQ), (1, X_AQ, 512, C_AQ + 512),
             (1, X_AK, 256, C_AK), (1, X_AKR, 256, C_AK + 256), (1, X_AV, 256, C_AV), (None, 0, 256, C_AV + 256),
             (0, M_GT, 3072, C_GT))
RW_TAIL = 128
PLAN_RW = ((0, 0, 1920 - RW_TAIL, 0), (1, 0, 2 * LANES, 1920 - RW_TAIL))


def _proj_kernel(x_ref, mod_ref, nw_ref, wa_ref, wb_ref, o_ref, *, plan):
    h = _norm_mod(x_ref[...], nw_ref[...], mod_ref[0, 3:4, :], mod_ref[0, 4:5, :]).astype(BF16)
    for which, src, width, dst in plan:
        for c in range(0, width, PROJ_TN):
            w = min(PROJ_TN, width - c)
            if which is None:
                o_ref[:, dst + c:dst + c + w] = jnp.zeros((x_ref.shape[0], w), o_ref.dtype)
            else:
                w_ref = (wa_ref, wb_ref)[which]
                o_ref[:, dst + c:dst + c + w] = _dot(h, w_ref[:, src + c:src + c + w]).astype(o_ref.dtype)


def _proj_call(x, mod, nw, wa, wa_block, wb, layer, *, plan, n_cols, ct, tm, out_dtype):
    nt = x.shape[0]
    resident = lambda width, col: pl.BlockSpec((None, D, width), lambda i: (layer, 0, col),
                                               pipeline_mode=pl.Buffered(1))
    return pl.pallas_call(
        functools.partial(_proj_kernel, plan=plan),
        grid=(nt // tm,),
        in_specs=[pl.BlockSpec((tm, D), lambda i: (i, 0)),
                  pl.BlockSpec((1, N_ADA, D), lambda i: (_mod_row(i * tm, ct), 0, 0)),
                  pl.BlockSpec((1, D), lambda i: (0, 0)),
                  resident(*wa_block), resident(wb.shape[2], 0)],
        out_specs=pl.BlockSpec((tm, n_cols), lambda i: (i, 0)),
        out_shape=jax.ShapeDtypeStruct((nt, n_cols), out_dtype),
        compiler_params=_params(("parallel",)),
        name="proj",
    )(x, mod, nw.reshape(1, D), wa, wb)


def _scan_blocks(b, d, s, *, nc_ctx, nc_lat, bsz):
    in_ctx = s < nc_ctx
    cs = jnp.where(d == 0, s, nc_ctx - 1 - s)
    ls = jnp.where(d == 0, s - nc_ctx, nc_lat - 1 - (s - nc_ctx))
    tok = jnp.where(in_ctx, b * nc_ctx + cs, bsz * nc_ctx + b * nc_lat + ls)
    pos = jnp.where(in_ctx, cs, nc_ctx + ls)
    return tok, pos


RET_STEP_CHUNKS = 2


def _ret_kernel(lg_ref, qkv_ref, cs_ref, dist_ref, ze_ref, xe_ref, o_ref, st_sc, dec_sc, xi_sc, zeta_sc):
    d = pl.program_id(1)
    s = pl.program_id(2)

    heads = range(8)

    @pl.when(s == 0)
    def _():
        st_sc[...] = jnp.zeros_like(st_sc)
        dist = dist_ref[...]
        for h in heads:
            lg = lg_ref[d, h]
            dec_sc[h] = jnp.where(dist >= 0.0, jnp.exp(lg * dist), 0.0)
            xi_sc[h] = jnp.exp(lg * xe_ref[...])
            zeta_sc[h] = jnp.exp(lg * ze_ref[...])

    c = RET_CHUNK
    lane = lax.broadcasted_iota(jnp.int32, (1, LANES), 1)
    hmask = [((lane // HEAD64) == e).astype(F32) for e in range(2)]
    starts = [pl.multiple_of(jnp.where(d == 0, i, RET_STEP_CHUNKS - 1 - i) * c, c) for i in range(RET_STEP_CHUNKS)]
    qms, kps, vhs = [], [], []
    for st0 in starts:
        rows = pl.ds(st0, c)
        cos = cs_ref[rows, 0:512]
        sin = cs_ref[rows, 512:1024]
        q = qkv_ref[rows, 0:512] * cos + qkv_ref[rows, 512:1024] * sin
        k = (qkv_ref[rows, 1024:1536] * cos + qkv_ref[rows, 1536:2048] * sin) * (HEAD64 ** -0.5)
        kps.append([k[:, p * LANES:(p + 1) * LANES].astype(BF16) for p in range(4)])
        qms.append([(q[:, (h // 2) * LANES:(h // 2 + 1) * LANES] * hmask[h % 2]).astype(BF16) for h in heads])
        vhs.append([qkv_ref[rows, C_RV + h * LANES:C_RV + (h + 1) * LANES] for h in heads])
    chunks = range(RET_STEP_CHUNKS)
    scs = [[(_dot_nt(qms[i][h], kps[i][h // 2]) * dec_sc[h]).astype(BF16) for h in heads] for i in chunks]
    intra = [[_dot(scs[i][h], vhs[i][h].astype(BF16)) for h in heads] for i in chunks]
    us = [[_dot_tn(kps[i][h // 2], (vhs[i][h] * zeta_sc[h]).astype(BF16)) for h in heads] for i in chunks]
    chunk_len = jnp.full((1, LANES), float(c), F32)
    decay = [jnp.exp(lg_ref[d, h] * chunk_len) for h in heads]
    sts = [st_sc[h] for h in heads]
    for i in chunks:
        inter = [_dot(qms[i][h], sts[h].astype(BF16)) * xi_sc[h] for h in heads]
        for h in heads:
            o_ref[pl.ds(starts[i], c), h * LANES:(h + 1) * LANES] = (intra[i][h] + inter[h]).astype(o_ref.dtype)
        sts = [sts[h] * decay[h] + us[i][h] for h in heads]
    for h in heads:
        st_sc[h] = sts[h]


def _ret_call(proj, log_g, cos_sin, dist, ze, xe, *, bsz):
    nt = proj.shape[0]
    c = RET_CHUNK
    rows = RET_STEP_CHUNKS * c
    nc_ctx, nc_lat = N_CTX // rows, N_LAT // rows
    steps = nc_ctx + nc_lat
    blocks = functools.partial(_scan_blocks, nc_ctx=nc_ctx, nc_lat=nc_lat, bsz=bsz)
    assert C_RQ == 0 and C_RV == 2048 and C_RG == 3072
    pos_map = lambda b, d, s, lg: (blocks(b, d, s)[1], 0)
    dir_map = lambda b, d, s, lg: (d, 0, 0)
    grid_spec = pltpu.PrefetchScalarGridSpec(
        num_scalar_prefetch=1,
        grid=(bsz, 2, steps),
        in_specs=[pl.BlockSpec((rows, C_RG), lambda b, d, s, lg: (blocks(b, d, s)[0], 0)),
                  pl.BlockSpec((rows, 1024), pos_map),
                  pl.BlockSpec((None, c, c), dir_map),
                  pl.BlockSpec((None, c, LANES), dir_map),
                  pl.BlockSpec((None, c, LANES), dir_map)],
        out_specs=pl.BlockSpec((None, rows, 1024), lambda b, d, s, lg: (d, blocks(b, d, s)[0], 0)),
        scratch_shapes=[pltpu.VMEM((8, LANES, LANES), F32), pltpu.VMEM((8, c, c), F32),
                        pltpu.VMEM((8, c, LANES), F32), pltpu.VMEM((8, c, LANES), F32)],
    )
    return pl.pallas_call(
        _ret_kernel,
        grid_spec=grid_spec,
        out_shape=jax.ShapeDtypeStruct((2, nt, 1024), BF16),
        compiler_params=_params(("parallel", "parallel", "arbitrary")),
        name="retention",
    )(log_g, proj, cos_sin, dist, ze, xe)


NEG = -1e30


ATT_KV_PER_STEP = 2


def _att_kernel(sink_ref, q_ref, qr_ref, qcos_ref, qsin_ref, qw_ref, qwr_ref, bo_ref,
                kl_ref, klr_ref, vl_ref, kcos_ref, ksin_ref, kc_ref, vc_ref, kw_ref, kwr_ref,
                o_ref, kn_sc, kcn_sc, *, n_ctx_blk):
    hk0 = pl.program_id(1) * ATT_KV_PER_STEP
    jb = pl.program_id(2)
    blk = ATT_BLOCK
    kv_lanes = [slice(e * LANES, (e + 1) * LANES) for e in range(ATT_KV_PER_STEP)]

    @pl.when(jb == 0)
    def _():
        for sl in kv_lanes:
            kl = kl_ref[:, sl].astype(F32)
            rs = lax.rsqrt(jnp.sum(kl * kl, -1, keepdims=True) * (0.5 / HEAD64) + NORM_EPS)
            kn = ((kl * kw_ref[...]) * kcos_ref[...]
                  + (klr_ref[:, sl].astype(F32) * kwr_ref[...]) * ksin_ref[...]) * rs
            kn_sc[:, sl] = kn.astype(BF16)
            kc = kc_ref[:, sl].astype(F32)
            rc = lax.rsqrt(jnp.sum(kc * kc, -1, keepdims=True) * (0.5 / HEAD64) + NORM_EPS)
            kcn_sc[:, sl] = (kc * kw_ref[...] * rc).astype(BF16)

    q = q_ref[...].astype(F32)
    mean_sq = _mm(_split(q * q), (bo_ref[...],)) * (1.0 / HEAD64)
    rs = lax.rsqrt(mean_sq + NORM_EPS)
    qn = ((q * qw_ref[...]) * qcos_ref[...]
          + (qr_ref[...].astype(F32) * qwr_ref[...]) * qsin_ref[...]) * rs * (HEAD64 ** -0.5)

    reach = jnp.where(jb >= n_ctx_blk, WINDOW, -1)
    lb = jnp.maximum(jb - n_ctx_blk, 0)
    n_win = 3 * blk
    start = jnp.clip((lb - 1) * blk, 0, N_LAT - n_win)
    start = pl.multiple_of(start, blk)
    n_keys = n_win + N_CTX
    kall = [jnp.concatenate([kn_sc[pl.ds(start, n_win), sl], kcn_sc[:, sl]], axis=0) for sl in kv_lanes]
    vall = [jnp.concatenate([vl_ref[pl.ds(start, n_win), sl], vc_ref[:, sl]], axis=0) for sl in kv_lanes]
    qpos = lb * blk + lax.broadcasted_iota(jnp.int32, (blk, n_keys), 0)
    col = lax.broadcasted_iota(jnp.int32, (blk, n_keys), 1)
    valid = (col >= n_win) | (jnp.abs(start + col - qpos) <= reach)
    lane = lax.broadcasted_iota(jnp.int32, (1, LANES), 1)
    hmask = [((lane // HEAD64) == e).astype(F32) for e in range(2)]
    groups = range(4 * ATT_KV_PER_STEP)
    sinks = [sink_ref[hk0 * 4 + g] for g in groups]
    qgs = [(qn[:, (g // 2) * LANES:(g // 2 + 1) * LANES] * hmask[g % 2]).astype(BF16) for g in groups]
    ss = [jnp.where(valid, _dot_nt(qgs[g], kall[g // 4]), NEG) for g in groups]
    ms = [jnp.maximum(jnp.max(ss[g], -1, keepdims=True), sinks[g]) for g in groups]
    ps = [jnp.exp(ss[g] - ms[g]) for g in groups]
    dens = [jnp.sum(ps[g], -1, keepdims=True) + jnp.exp(sinks[g] - ms[g]) for g in groups]
    ogs = [_dot(ps[g].astype(BF16), vall[g // 4]) / dens[g] * hmask[g % 2] for g in groups]
    o_ref[...] = jnp.concatenate([ogs[2 * p] + ogs[2 * p + 1] for p in range(2 * ATT_KV_PER_STEP)],
                                 axis=-1).astype(o_ref.dtype)


def _att_call(proj, sink, qcos, qsin, kcos, ksin, qw, qwr, kw, kwr, bo, *, bsz):
    nt = proj.shape[0]
    blk = ATT_BLOCK
    n_ctx_blk, n_lat_blk = N_CTX // blk, N_LAT // blk
    steps = n_ctx_blk + n_lat_blk
    qwid = 256 * ATT_KV_PER_STEP
    kwid = LANES * ATT_KV_PER_STEP

    def qtok(b, jb):
        return jnp.where(jb < n_ctx_blk, b * n_ctx_blk + jb, bsz * n_ctx_blk + b * n_lat_blk + (jb - n_ctx_blk))

    def qmap(col):
        return lambda b, hk, jb, sk: (qtok(b, jb), col // qwid + hk)

    qpos_map = lambda b, hk, jb, sk: (jb, 0)
    const = lambda b, hk, jb, sk: (0, 0)
    ct_rows = bsz * N_CTX

    def lat_map(col):
        return lambda b, hk, jb, sk: (ct_rows // N_LAT + b, col // kwid + hk)

    def ctx_map(col):
        return lambda b, hk, jb, sk: (b, col // kwid + hk)

    grid_spec = pltpu.PrefetchScalarGridSpec(
        num_scalar_prefetch=1,
        grid=(bsz, 2 // ATT_KV_PER_STEP, steps),
        in_specs=[pl.BlockSpec((blk, qwid), qmap(C_AQ)),
                  pl.BlockSpec((blk, qwid), qmap(C_AQ + 512)),
                  pl.BlockSpec((blk, qwid), qpos_map),
                  pl.BlockSpec((blk, qwid), qpos_map),
                  pl.BlockSpec((1, qwid), const),
                  pl.BlockSpec((1, qwid), const),
                  pl.BlockSpec((qwid, qwid), const),
                  pl.BlockSpec((N_LAT, kwid), lat_map(C_AK)),
                  pl.BlockSpec((N_LAT, kwid), lat_map(C_AK + 256)),
                  pl.BlockSpec((N_LAT, kwid), lat_map(C_AV)),
                  pl.BlockSpec((N_LAT, LANES), const),
                  pl.BlockSpec((N_LAT, LANES), const),
                  pl.BlockSpec((N_CTX, kwid), ctx_map(C_AK)),
                  pl.BlockSpec((N_CTX, kwid), ctx_map(C_AV)),
                  pl.BlockSpec((1, LANES), const),
                  pl.BlockSpec((1, LANES), const)],
        out_specs=pl.BlockSpec((blk, qwid), lambda b, hk, jb, sk: (qtok(b, jb), hk)),
        scratch_shapes=[pltpu.VMEM((N_LAT, kwid), BF16), pltpu.VMEM((N_CTX, kwid), BF16)],
    )
    kern = functools.partial(_att_kernel, n_ctx_blk=n_ctx_blk)
    return pl.pallas_call(
        kern,
        grid_spec=grid_spec,
        out_shape=jax.ShapeDtypeStruct((nt, 512), BF16),
        compiler_params=_params(("parallel", "parallel", "arbitrary")),
        name="attention",
    )(sink, proj, proj, qcos, qsin, qw, qwr, bo, proj, proj, proj, kcos, ksin, proj, proj, kw, kwr)


RW_TM = 256
RW_PREP_GROUPS = 2


def _rw_prep_kernel(cur_ref, prev_ref, next_ref, shift_ref, wlo_ref, gup_ref, vup_ref, bo_ref, vec_ref, vf_ref,
                    r_o, v_o, kk_o, g_o, bonus_o, lw_o, kd_o, bv_o, *, ct, mix_v):
    i = pl.program_id(0)
    tm = RW_TM
    tok0 = i * tm
    lat_off = tok0 - ct
    is_start = jnp.where(tok0 < ct, tok0 % N_CTX == 0, lat_off % N_LAT == 0)
    is_end = jnp.where(tok0 < ct, (tok0 + tm) % N_CTX == 0, (lat_off + tm) % N_LAT == 0)
    cur = cur_ref[...]
    row = lax.broadcasted_iota(jnp.int32, (tm, 1), 0)
    prev_row = jnp.where(is_start, 0.0, prev_ref[7:8, :])
    next_row = jnp.where(is_end, 0.0, next_ref[0:1, :])
    prev = jnp.where(row == 0, prev_row, pltpu.roll(cur, 1, 0))
    nxt = jnp.where(row == tm - 1, next_row, pltpu.roll(cur, tm - 1, 0))
    c = prev * shift_ref[0:1, :] + cur * shift_ref[1:2, :] + nxt * shift_ref[2:3, :]

    vec = vec_ref[...]
    k_k, k_a, r_k, v0 = vec[0:1], vec[1:2], vec[2:3], vec[3:4]
    bo = (bo_ref[...],)
    rows = [slice(i * (tm // RW_PREP_GROUPS), (i + 1) * (tm // RW_PREP_GROUPS)) for i in range(RW_PREP_GROUPS)]
    grp = range(RW_PREP_GROUPS)
    r = [c[rw, 0:512] for rw in rows]
    k = [c[rw, 512:1024] for rw in rows]
    v = [c[rw, 1024:1536] for rw in rows]
    lo = [c[rw, 1536:1792] for rw in rows]
    gd = [c[rw, 1792:1920] for rw in rows]
    vl = [c[rw, 1920:2048] for rw in rows]
    if mix_v:
        mixv = [_sigmoid(v0 + _mm(_split(vl[i]), (vup_ref[0], vup_ref[1]))) for i in grp]
        v = [v[i] + (vf_ref[rows[i], :].astype(F32) - v[i]) * mixv[i] for i in grp]
    g = [_mm(_split(_sigmoid(gd[i])), (gup_ref[0], gup_ref[1])) for i in grp]
    kk = [k[i] * k_k for i in grp]
    ss = [_mm(_split(kk[i] * kk[i]), bo) for i in grp]
    kk = [kk[i] / jnp.maximum(jnp.sqrt(ss[i]), 1e-12) for i in grp]
    z_lora = [_mm(_split(jnp.tanh(lo[i][:, 0:LANES])), (wlo_ref[0, 0], wlo_ref[1, 0])) for i in grp]
    a_lora = [_mm(_split(lo[i][:, LANES:2 * LANES]), (wlo_ref[0, 1], wlo_ref[1, 1])) for i in grp]
    ksum = [jnp.zeros_like(k[i]) for i in grp]
    for d in range(2):
        a = [_sigmoid(vec[6 + d:7 + d] + a_lora[i][:, d * 512:(d + 1) * 512]) for i in grp]
        kd = [k[i] * (1.0 + (a[i] - 1.0) * k_a) for i in grp]
        for i in grp:
            lw_o[d, rows[i], :] = -RWKV_DECAY_SCALE * _sigmoid(vec[4 + d:5 + d] + z_lora[i][:, d * 512:(d + 1) * 512])
            kd_o[d, rows[i], :] = kd[i].astype(kd_o.dtype)
            bv_o[d, rows[i], :] = (a[i] * kk[i]).astype(bv_o.dtype)
        ksum = [ksum[i] + kd[i] for i in grp]
    bonus = [_mm(_split(r[i] * ksum[i] * r_k), bo) * v[i] for i in grp]
    for i in grp:
        bonus_o[rows[i], :] = bonus[i]
        r_o[rows[i], :] = r[i].astype(r_o.dtype)
        v_o[rows[i], :] = v[i].astype(v_o.dtype)
        kk_o[rows[i], :] = kk[i].astype(kk_o.dtype)
        g_o[rows[i], :] = g[i]


def _rw_prep_call(proj, shift, wlo, gup, vup, bo, vec, vfirst, *, ct, mix_v):
    nt = proj.shape[0]
    tm = RW_TM
    nblk8 = nt // 8
    tok = pl.BlockSpec((tm, 512), lambda i: (i, 0))
    dirs = pl.BlockSpec((2, tm, 512), lambda i: (0, i, 0))
    const = lambda i: (0, 0)
    kern = functools.partial(_rw_prep_kernel, ct=ct, mix_v=mix_v)
    one = lambda dt: jax.ShapeDtypeStruct((nt, 512), dt)
    two = lambda dt: jax.ShapeDtypeStruct((2, nt, 512), dt)
    return pl.pallas_call(
        kern,
        grid=(nt // tm,),
        in_specs=[pl.BlockSpec((tm, N_RW), lambda i: (i, 0)),
                  pl.BlockSpec((8, N_RW), lambda i: (jnp.maximum(i * (tm // 8) - 1, 0), 0)),
                  pl.BlockSpec((8, N_RW), lambda i: (jnp.minimum((i + 1) * (tm // 8), nblk8 - 1), 0)),
                  pl.BlockSpec((3, N_RW), const),
                  pl.BlockSpec((2, 2, LANES, 1024), lambda i: (0, 0, 0, 0)),
                  pl.BlockSpec((2, 128, 512), lambda i: (0, 0, 0)),
                  pl.BlockSpec((2, 128, 512), lambda i: (0, 0, 0)),
                  pl.BlockSpec((512, 512), const),
                  pl.BlockSpec((8, 512), const),
                  tok],
        out_specs=[tok, tok, tok, tok, tok, dirs, dirs, dirs],
        out_shape=[one(BF16), one(BF16), one(BF16), one(F32), one(F32), two(F32), two(BF16), two(BF16)],
        compiler_params=_params(("parallel",)),
        name="rwkv_prep",
    )(proj, proj, proj, shift, wlo, gup, vup, bo, vec, vfirst)


SCAN_TERMS = dict(v=1, x=1, y=1, ak=1, ao=1, m=1, rhs=1, pq=1, rb=1, rk=1, bp=1, kp=1, st=1, yq=1, g=1)
RW_STEP_CHUNKS = 4


def _rw_scan_kernel(r_ref, v_ref, kk_ref, lw_ref, kd_ref, bv_ref, lcum_ref, mbig_ref, lvl_ref, eye_ref,
                    y_ref, st_sc):
    d = pl.program_id(1)
    s = pl.program_id(2)
    t = RW_CHUNK
    nt_ = SCAN_TERMS

    @pl.when(s == 0)
    def _():
        st_sc[...] = jnp.zeros_like(st_sc)

    lcum = (lcum_ref[...].astype(BF16),)
    mbig = mbig_ref[...]
    eye = eye_ref[...]
    lane = lax.broadcasted_iota(jnp.int32, (1, LANES), 1)
    m_e = (lane < HEAD64).astype(F32)
    m_o = 1.0 - m_e

    def stack2(x):
        return jnp.concatenate([x * m_e, x * m_o], axis=0)

    pairs = range(4)
    sls = [slice(p * LANES, (p + 1) * LANES) for p in pairs]
    starts = [pl.multiple_of(jnp.where(d == 0, c, RW_STEP_CHUNKS - 1 - c) * t, t) for c in range(RW_STEP_CHUNKS)]
    aw2, rt2, bi2, ki2, bp2t, kp2t, v2s, w_tot = [], [], [], [], [], [], [], []
    for st0 in starts:
        rows = pl.ds(st0, t)
        lw = lw_ref[rows, :]
        cum = _mm(lcum, _split(lw, 3))
        tot = jnp.sum(lw, axis=0, keepdims=True)
        kk, kd, bv = kk_ref[rows, :], kd_ref[rows, :], bv_ref[rows, :]
        aw = -kk * jnp.exp(cum - lw)
        rt = r_ref[rows, :] * jnp.exp(cum)
        e_inv = jnp.exp(-cum)
        e_rem = jnp.exp(tot - cum)
        v = v_ref[rows, :]
        aw2 += [stack2(aw[:, sl]) for sl in sls]
        rt2 += [stack2(rt[:, sl]) for sl in sls]
        bi2 += [stack2((bv * e_inv)[:, sl]) for sl in sls]
        ki2 += [stack2((kd * e_inv)[:, sl]) for sl in sls]
        bp2t += [stack2((bv * e_rem)[:, sl]).T for sl in sls]
        kp2t += [stack2((kd * e_rem)[:, sl]).T for sl in sls]
        v2s += [_split(stack2(v[:, sl]), nt_["v"]) for sl in sls]
        w_tot += [jnp.exp(tot)[:, sl] for sl in sls]

    items = range(RW_STEP_CHUNKS * 4)
    big = [_mm(_split(jnp.concatenate([aw2[i], rt2[i]], axis=0), nt_["x"]),
               _split(jnp.concatenate([bi2[i], ki2[i]], axis=0), nt_["y"]), NT) * mbig for i in items]
    a_ab = [b[0:2 * t, 0:2 * t] for b in big]
    a_ak = [b[0:2 * t, 2 * t:4 * t] for b in big]
    a_rb = [b[2 * t:4 * t, 0:2 * t] for b in big]
    a_rk = [b[2 * t:4 * t, 2 * t:4 * t] for b in big]
    akv = [_mm(_split(a_ak[i], nt_["ak"]), v2s[i]) for i in items]
    inv = [eye + a * lvl_ref[0] for a in a_ab]
    for j in range(1, int(math.log2(t))):
        ms = [_split(m, nt_["m"]) for m in inv]
        low = [_split(_mm(_split(a_ab[i] * lvl_ref[j], nt_["ao"]), ms[i]), nt_["m"]) for i in items]
        inv = [inv[i] + _mm(ms[i], low[i]) for i in items]
    pqs = [_split(_mm(_split(inv[i], nt_["m"]),
                      _split(jnp.concatenate([aw2[i], akv[i]], axis=1), nt_["rhs"])), nt_["pq"]) for i in items]
    tmp = [_mm(_split(a_rb[i], nt_["rb"]), pqs[i]) for i in items]
    ark_v = [_mm(_split(a_rk[i], nt_["rk"]), v2s[i]) for i in items]
    gu = [_mm(_split(bp2t[i], nt_["bp"]), pqs[i]) for i in items]
    kpv = [_mm(_split(kp2t[i], nt_["kp"]), v2s[i]) for i in items]
    yq = [_split(rt2[i] + tmp[i][:, 0:LANES], nt_["yq"]) for i in items]
    y0 = [tmp[i][:, LANES:2 * LANES] + ark_v[i] for i in items]
    g_mat = [_split(gu[i][:, 0:LANES] + eye * w_tot[i], nt_["g"]) for i in items]
    u_mat = [gu[i][:, LANES:2 * LANES] + kpv[i] for i in items]

    st = [st_sc[p] for p in pairs]
    for c in range(RW_STEP_CHUNKS):
        sts = [_split(st[p], nt_["st"]) for p in pairs]
        y2 = [_mm(yq[4 * c + p], sts[p]) + y0[4 * c + p] for p in pairs]
        st = [_mm(g_mat[4 * c + p], sts[p]) + u_mat[4 * c + p] for p in pairs]
        for p in pairs:
            y_ref[pl.ds(starts[c], t), sls[p]] = y2[p][0:t] + y2[p][t:2 * t]
    for p in pairs:
        st_sc[p] = st[p]


def _rw_scan_call(r, v, kk, lw, kd, bv, lcum, mbig, lvl, eye, *, bsz):
    nt = r.shape[0]
    t = RW_CHUNK
    rows = RW_STEP_CHUNKS * t
    nc_ctx, nc_lat = N_CTX // rows, N_LAT // rows
    steps = nc_ctx + nc_lat
    blocks = functools.partial(_scan_blocks, nc_ctx=nc_ctx, nc_lat=nc_lat, bsz=bsz)
    tok = pl.BlockSpec((rows, 512), lambda b, d, s: (blocks(b, d, s)[0], 0))
    tokd = pl.BlockSpec((None, rows, 512), lambda b, d, s: (d, blocks(b, d, s)[0], 0))
    n_lvl = lvl.shape[1]
    return pl.pallas_call(
        _rw_scan_kernel,
        grid=(bsz, 2, steps),
        in_specs=[tok, tok, tok, tokd, tokd, tokd,
                  pl.BlockSpec((None, t, t), lambda b, d, s: (d, 0, 0)),
                  pl.BlockSpec((None, 4 * t, 4 * t), lambda b, d, s: (d, 0, 0)),
                  pl.BlockSpec((None, n_lvl, 2 * t, 2 * t), lambda b, d, s: (d, 0, 0, 0)),
                  pl.BlockSpec((LANES, LANES), lambda b, d, s: (0, 0))],
        out_specs=tokd,
        out_shape=jax.ShapeDtypeStruct((2, nt, 512), F32),
        scratch_shapes=[pltpu.VMEM((4, LANES, LANES), F32)],
        compiler_params=_params(("parallel", "parallel", "arbitrary")),
        name="rwkv_scan",
    )(r, v, kk, lw, kd, bv, lcum, mbig, lvl, eye)


MERGE_TM = 512


def _merge_kernel(x_ref, mod_ref, oret_ref, rg_ref, att_ref, y_ref, bonus_ref, g_ref, gt_ref,
                  bo_ref, gn_ref, wr_ref, wa_ref, ww_ref, wo_ref, o_ref):
    o = oret_ref[0].astype(F32) + oret_ref[1].astype(F32)
    parts = []
    for h in range(8):
        oh = o[:, h * LANES:(h + 1) * LANES]
        mu = jnp.mean(oh, -1, keepdims=True)
        dv = oh - mu
        var = jnp.mean(dv * dv, -1, keepdims=True)
        parts.append(dv * lax.rsqrt(var + NORM_EPS))
    rg = rg_ref[...].astype(F32)
    ret = (rg * _sigmoid(rg)) * jnp.concatenate(parts, axis=-1)

    y = y_ref[0] + y_ref[1]
    bo = (bo_ref[...],)
    mu = _mm(_split(y), bo) * (1.0 / HEAD64)
    dy = y - mu
    var = _mm(_split(dy * dy), bo) * (1.0 / HEAD64)
    yn = dy * lax.rsqrt(var + RWKV_GN_EPS) * gn_ref[0:1, :] + gn_ref[1:2, :]
    rw = (yn + bonus_ref[...]) * g_ref[...]

    merged = (_sigmoid(gt_ref[:, 0:D].astype(F32)) * _dot(ret.astype(BF16), wr_ref[...])
              + _sigmoid(gt_ref[:, D:2 * D].astype(F32)) * _dot(att_ref[...].astype(BF16), wa_ref[...])
              + _sigmoid(gt_ref[:, 2 * D:3 * D].astype(F32)) * _dot(rw.astype(BF16), ww_ref[...]))
    out = _dot(merged.astype(BF16), wo_ref[...])
    o_ref[...] = x_ref[...] + mod_ref[0, 5:6, :] * out


def _merge_call(x, mod, proj, oret, att, y, bonus, g, bo, gn, wr, wa, ww, wo, layer, *, ct, tm, tile0, n_tiles):
    tok = lambda w, col=0: pl.BlockSpec((tm, w), lambda i: (i + tile0, col))
    tok2 = lambda w: pl.BlockSpec((2, tm, w), lambda i: (0, i + tile0, 0))
    const = lambda i: (0, 0)
    wspec = lambda rows: pl.BlockSpec((None, rows, D), lambda i: (layer, 0, 0), pipeline_mode=pl.Buffered(1))
    assert C_GT % (3 * D) == 0
    return pl.pallas_call(
        _merge_kernel,
        grid=(n_tiles,),
        in_specs=[tok(D),
                  pl.BlockSpec((1, N_ADA, D), lambda i: (_mod_row((i + tile0) * tm, ct), 0, 0)),
                  tok2(1024), tok(1024, C_RG // 1024), tok(512), tok2(512), tok(512), tok(512),
                  tok(3 * D, C_GT // (3 * D)),
                  pl.BlockSpec((512, 512), const),
                  pl.BlockSpec((2, 512), const),
                  wspec(1024), wspec(512), wspec(512), wspec(D)],
        out_specs=pl.BlockSpec((tm, D), lambda i: (i, 0)),
        out_shape=jax.ShapeDtypeStruct((n_tiles * tm, D), F32),
        compiler_params=_params(("parallel",)),
        name="merge",
    )(x, mod, oret, proj, att, y, bonus, g, proj, bo, gn, wr, wa, ww, wo)


def _rope_1d(pos, dim, base):
    n_freq = dim // 2
    inv = np.power(np.float32(base), -(np.arange(n_freq, dtype=np.float32) / np.float32(n_freq))).astype(np.float32)
    return pos.astype(np.float32)[:, None] * inv[None, :]


def _tables():
    ang = _rope_1d(np.arange(N_CTX + N_LAT), HEAD64, 10000.0)
    rcos = np.tile(np.concatenate([np.cos(ang), np.cos(ang)], -1), (1, 8)).astype(np.float32)
    rsin = np.tile(np.concatenate([np.sin(ang), np.sin(ang)], -1), (1, 8)).astype(np.float32)
    rows = N_LAT // GRID_W
    row = np.repeat(np.arange(rows), GRID_W)
    col = np.arange(rows * GRID_W) % GRID_W
    aang = np.concatenate([_rope_1d(row, HEAD64 // 2, 10000.0), _rope_1d(col, HEAD64 // 2, 10000.0)], -1)
    ac = np.concatenate([np.cos(aang), np.cos(aang)], -1).astype(np.float32)
    asn = np.concatenate([np.sin(aang), np.sin(aang)], -1).astype(np.float32)
    q_cols = 256 * ATT_KV_PER_STEP
    qcos = np.concatenate([np.ones((N_CTX, q_cols), np.float32), np.tile(ac, (1, q_cols // HEAD64))], 0)
    qsin = np.concatenate([np.zeros((N_CTX, q_cols), np.float32), np.tile(asn, (1, q_cols // HEAD64))], 0)
    kcos = np.tile(ac, (1, 2))
    ksin = np.tile(asn, (1, 2))
    c = RET_CHUNK
    pos = np.arange(c, dtype=np.float32)
    diff = pos[:, None] - pos[None, :]
    dist = np.stack([np.where(diff >= 0, diff, -1.0), np.where(diff <= 0, -diff, -1.0)]).astype(np.float32)
    ze = np.stack([c - 1.0 - pos, pos]).astype(np.float32)
    xe = np.stack([pos + 1.0, c - pos]).astype(np.float32)
    ze = np.broadcast_to(ze[:, :, None], (2, c, LANES)).copy()
    xe = np.broadcast_to(xe[:, :, None], (2, c, LANES)).copy()
    t = RW_CHUNK
    ti = np.arange(t)
    low_incl = (ti[None, :] <= ti[:, None]).astype(np.float32)
    low_strict = (ti[None, :] < ti[:, None]).astype(np.float32)
    lcum = np.stack([low_incl, low_incl.T])
    eye2 = np.eye(2, dtype=np.float32)
    mbig = []
    for strict, incl in ((low_strict, low_incl), (low_strict.T, low_incl.T)):
        s2 = np.kron(eye2, strict)
        i2 = np.kron(eye2, incl)
        mbig.append(np.block([[s2, s2], [i2, i2]]))
    mbig = np.stack(mbig).astype(np.float32)
    lvl = []
    for j in range(int(math.log2(t))):
        sz = 2 ** j
        same = (ti[:, None] // (2 * sz)) == (ti[None, :] // (2 * sz))
        m = (same & ((ti[:, None] % (2 * sz)) >= sz) & ((ti[None, :] % (2 * sz)) < sz)).astype(np.float32)
        lvl.append(np.kron(eye2, m))
    lvl = np.stack(lvl)
    lvl = np.stack([lvl, lvl.transpose(0, 2, 1)]).astype(np.float32)
    eye = np.eye(LANES, dtype=np.float32)
    bo64 = np.kron(np.eye(8, dtype=np.float32), np.ones((HEAD64, HEAD64), np.float32))
    return dict(rcos=rcos, rsin=rsin, qcos=qcos, qsin=qsin, kcos=kcos, ksin=ksin, dist=dist, ze=ze, xe=xe,
                lcum=lcum, mbig=mbig, lvl=lvl, eye=eye, bo64=bo64)


def _rot_cols(n_heads):
    half = HEAD64 // 2
    idx, sgn = [], []
    for h in range(n_heads):
        base = h * HEAD64
        idx += list(range(base + half, base + HEAD64)) + list(range(base, base + half))
        sgn += [-1.0] * half + [1.0] * half
    return np.array(idx), np.array(sgn, np.float32)


def _derived_weights(w_in, v_down):
    depth = w_in.shape[0]
    rq, rk = w_in[..., M_RQ:M_RK], w_in[..., M_RK:M_RV]
    aq, ak, av = w_in[..., M_AQ:M_AK], w_in[..., M_AK:M_AV], w_in[..., M_AV:M_RW]
    half = HEAD64 // 2
    sign = jnp.asarray([-1.0, 1.0], w_in.dtype).reshape(2, 1)

    def rot(m):
        heads = m.shape[-1] // HEAD64
        halves = m.reshape(depth, D, heads, 2, half)
        return (halves[:, :, :, ::-1, :] * sign).reshape(depth, D, heads * HEAD64)

    dup = lambda m: jnp.concatenate([m[..., 0:64], m[..., 0:64], m[..., 64:128], m[..., 64:128]], -1)
    z = lambda n, d=depth: jnp.zeros((d, D, n), w_in.dtype)
    w_vd = jnp.concatenate([w_in[..., M_GT - RW_TAIL:M_GT],
                            jnp.concatenate([z(32, 1), v_down.astype(w_in.dtype)], 0), z(96)], -1)
    w_x = jnp.concatenate([rot(rq), rot(rk), rot(aq), dup(ak), dup(rot(ak)), dup(av)], -1)
    assert w_x.shape == (depth, D, N_X) and w_vd.shape == (depth, D, 2 * LANES), (w_x.shape, w_vd.shape)
    return w_x, w_vd


def kernel(x, c, ctx, c_ctx, ada_w, ada_b, norm_w, ffn1_w_in, ffn1_w_out, ffn2_w_in, ffn2_w_out, mix_w_in, ret_decay_logit, att_q_norm, att_k_norm, att_sink, rwkv_shift, rwkv_w0, rwkv_w_up, rwkv_a0, rwkv_a_up, rwkv_g_up, rwkv_k_k, rwkv_k_a, rwkv_r_k, rwkv_v0, rwkv_v_down, rwkv_v_up, rwkv_gn_w, rwkv_gn_b, w_branch_ret, w_branch_att, w_branch_rwkv, w_out):
    bsz = x.shape[0]
    depth = ada_w.shape[0]
    ct = bsz * N_CTX
    nt = ct + bsz * N_LAT
    tm = math.gcd(1024, ct)
    ffn_tm = math.gcd(2048, ct)
    tb = {k: jnp.asarray(v) for k, v in _tables().items()}
    bo = tb["bo64"].astype(BF16)
    rcs = jnp.concatenate([tb["rcos"], tb["rsin"]], -1)
    stack_split = lambda w: jnp.stack(_split(w))

    xs = jnp.concatenate([ctx.reshape(ct, D), x.reshape(bsz * N_LAT, D)], 0)
    rows = 8 * ((bsz + 1 + 7) // 8)
    cond = jnp.zeros((rows, D), F32).at[0].set(c_ctx).at[1:bsz + 1].set(c)
    i2, _ = _rot_cols(2)
    vfirst = None
    f1_in, f1_out, f2_in, f2_out = ffn1_w_in, ffn1_w_out, ffn2_w_in, ffn2_w_out
    mix_b = mix_w_in.astype(BF16)
    w_x, w_vd = _derived_weights(mix_b, rwkv_v_down)
    n_mix = mix_b.shape[2]
    assert M_RW % (M_GT - M_RW) == 0
    wb_ret, wb_att = w_branch_ret.astype(BF16), w_branch_att.astype(BF16)
    wb_rw, wb_out = w_branch_rwkv.astype(BF16), w_out.astype(BF16)
    mod_all = _ada_call(cond, ada_w, ada_b).reshape(depth, rows, N_ADA, D)
    for l in range(depth):
        last = l == depth - 1
        mod = mod_all[l]
        xs = _ffn_call(xs, mod, norm_w[l, 0], f1_in, f1_out, l, base=0, ct=ct, tm=ffn_tm, tile0=0,
                       n_tiles=nt // ffn_tm)

        proj_rw = _proj_call(xs, mod, norm_w[l, 1], mix_b, (M_GT - M_RW, M_RW // (M_GT - M_RW)), w_vd, l,
                             plan=PLAN_RW, n_cols=N_RW, ct=ct, tm=tm, out_dtype=F32)
        proj = _proj_call(xs, mod, norm_w[l, 1], mix_b, (n_mix, 0), w_x, l,
                          plan=PLAN_REST, n_cols=N_REST, ct=ct, tm=tm // 2, out_dtype=BF16)

        log_g = jax.nn.log_sigmoid(ret_decay_logit[l].astype(F32))
        oret = _ret_call(proj, log_g, rcs, tb["dist"], tb["ze"], tb["xe"], bsz=bsz)

        q_cols = 256 * ATT_KV_PER_STEP
        qw = jnp.tile(att_q_norm[l], q_cols // HEAD64).reshape(1, q_cols)
        qwr = jnp.tile(att_q_norm[l][i2[:64]], q_cols // HEAD64).reshape(1, q_cols)
        kw = jnp.tile(att_k_norm[l], 2).reshape(1, 128)
        kwr = jnp.tile(att_k_norm[l][i2[:64]], 2).reshape(1, 128)
        att = _att_call(proj, att_sink[l].astype(F32), tb["qcos"], tb["qsin"], tb["kcos"], tb["ksin"],
                        qw, qwr, kw, kwr, bo[:q_cols, :q_cols], bsz=bsz)

        shift = jnp.concatenate([rwkv_shift[l], jnp.tile(jnp.array([[0.0], [1.0], [0.0]], F32), (1, 128))], -1)
        wlo = jnp.zeros((2, LANES, 1024), F32)
        for d in range(2):
            wlo = wlo.at[0, d * 64:(d + 1) * 64, d * 512:(d + 1) * 512].set(rwkv_w_up[l, d])
            wlo = wlo.at[1, d * 64:(d + 1) * 64, d * 512:(d + 1) * 512].set(rwkv_a_up[l, d])
        if l > 0:
            vup = jnp.zeros((128, 512), F32).at[0:32].set(rwkv_v_up[l - 1])
            v0 = rwkv_v0[l - 1]
        else:
            vup = jnp.zeros((128, 512), F32)
            v0 = jnp.zeros((512,), F32)
        vec = jnp.stack([rwkv_k_k[l], rwkv_k_a[l], rwkv_r_k[l].reshape(512), v0,
                         rwkv_w0[l, 0], rwkv_w0[l, 1], rwkv_a0[l, 0], rwkv_a0[l, 1]])
        vf_in = vfirst if l > 0 else proj_rw
        r, v, kk, g, bonus, lw, kd, bv = _rw_prep_call(proj_rw, shift, stack_split(wlo), stack_split(rwkv_g_up[l]),
                                                        stack_split(vup), bo, vec, vf_in, ct=ct, mix_v=l > 0)
        if l == 0:
            vfirst = v
        y = _rw_scan_call(r, v, kk, lw, kd, bv, tb["lcum"], tb["mbig"], tb["lvl"], tb["eye"], bsz=bsz)

        gn = jnp.stack([rwkv_gn_w[l], rwkv_gn_b[l]])
        tile0 = ct // MERGE_TM if last else 0
        xs = _merge_call(xs, mod, proj, oret, att, y, bonus, g, bo, gn, wb_ret, wb_att, wb_rw, wb_out, l,
                         ct=ct, tm=MERGE_TM, tile0=tile0, n_tiles=nt // MERGE_TM - tile0)
        xs = _ffn_call(xs, mod, norm_w[l, 2], f2_in, f2_out, l, base=6, ct=0 if last else ct, tm=ffn_tm, tile0=0,
                       n_tiles=xs.shape[0] // ffn_tm)
    return xs.reshape(bsz, N_LAT, D)
```
